```python
import math
import jax, jax.numpy as jnp
from jax import lax
import numpy as np

D_MODEL = 2048
BATCH = 8
SEQ = 4096
DEPTH = 4

CTX_LEN = 256
GRID_W = 64
N_MIXERS = 2
N_MOD = 9
EPS = 1e-6
D_FF = 256 * ((8 * D_MODEL // 3 + 255) // 256)

DN_HEAD_DIM = 128
DN_QK_HEADS = D_MODEL // 128
DN_V_HEADS = 2 * DN_QK_HEADS
DN_QK_DIM = DN_QK_HEADS * DN_HEAD_DIM
DN_V_DIM = DN_V_HEADS * DN_HEAD_DIM
DN_CONV_DIM = 2 * DN_QK_DIM + DN_V_DIM
DN_CONV_K = 5
DN_CHUNK = 64
DN_IN_DIM = DN_CONV_DIM + DN_V_DIM + 4 * DN_V_HEADS

SG_CHUNK = 128
SG_DIM = 2 * D_MODEL
SG_GROUPS = 16
SG_GROUP_DIM = SG_DIM // SG_GROUPS

N_A = (DEPTH + N_MIXERS - 1) // N_MIXERS
N_B = DEPTH // N_MIXERS

kernel_name = "hybrid_deltanet_gmlp_prefix_dit"


def rmsnorm(x, g):
    xf = x.astype(jnp.float32)
    y = xf * lax.rsqrt(jnp.mean(xf * xf, axis=-1, keepdims=True) + EPS)
    return (y * g.astype(jnp.float32)).astype(x.dtype)


def layernorm(x, g, b):
    xf = x.astype(jnp.float32)
    mu = jnp.mean(xf, axis=-1, keepdims=True)
    var = jnp.mean(jnp.square(xf - mu), axis=-1, keepdims=True)
    y = (xf - mu) * lax.rsqrt(var + EPS) * g.astype(jnp.float32) + b.astype(jnp.float32)
    return y.astype(x.dtype)


def l2norm(t):
    return t * lax.rsqrt(jnp.sum(t * t, axis=-1, keepdims=True) + EPS)


def modulate(h, shift, scale):
    return h * (1 + scale) + shift


def swiglu(h, w_gu, w_down):
    gate, up = jnp.split(h @ w_gu, 2, axis=-1)
    return (jax.nn.silu(gate) * up) @ w_down


def half_ffn(t, g, m, w_gu, w_down):
    shift, scale, gate = m
    return t + 0.5 * gate * swiglu(modulate(rmsnorm(t, g), shift, scale), w_gu, w_down)


def to_scan_order(t, col_major):
    if not col_major:
        return t
    b, s, ch = t.shape
    rows = s // GRID_W
    return t.reshape(b, rows, GRID_W, ch).transpose(0, 2, 1, 3).reshape(b, s, ch)


def from_scan_order(t, col_major):
    if not col_major:
        return t
    b, s, ch = t.shape
    rows = s // GRID_W
    return t.reshape(b, GRID_W, rows, ch).transpose(0, 2, 1, 3).reshape(b, s, ch)


def centred_dwconv(x, w):
    k_taps = w.shape[0]
    p = k_taps // 2
    s = x.shape[1]
    xp = jnp.pad(x, ((0, 0), (p, p), (0, 0)))
    return sum(xp[:, t:t + s] * w[t] for t in range(k_taps))


def gated_delta_chunked(q, k, v, g, beta, state0):
    b_, h_, s_, dk = q.shape
    dv = v.shape[-1]
    cs = DN_CHUNK
    n = s_ // cs

    def chunks(t):
        return t.reshape(b_, h_, n, cs, *t.shape[3:])

    q, k, v, g, beta = chunks(q), chunks(k), chunks(v), chunks(g), chunks(beta)
    g = jnp.cumsum(g, axis=-1)
    tril = jnp.tril(jnp.ones((cs, cs), dtype=bool))
    strict = jnp.tril(jnp.ones((cs, cs), dtype=bool), -1)
    decay = jnp.exp(jnp.where(tril, g[..., :, None] - g[..., None, :], -jnp.inf))
    k_beta = k * beta[..., None]
    a_mat = jnp.where(strict, jnp.einsum('bhncd,bhnsd->bhncs', k_beta, k) * decay, 0.0)
    eye = jnp.eye(cs, dtype=q.dtype)
    rhs = jnp.concatenate([v * beta[..., None], k_beta * jnp.exp(g)[..., None]], axis=-1)
    sol = lax.linalg.triangular_solve(eye + a_mat, rhs, left_side=True, lower=True,
                                      unit_diagonal=True)
    u, w = sol[..., :dv], sol[..., dv:]
    qk = jnp.where(tril, jnp.einsum('bhncd,bhnsd->bhncs', q, k) * decay, 0.0)
    q_dec = q * jnp.exp(g)[..., None]
    k_dec = k * jnp.exp(g[..., -1:] - g)[..., None]
    g_last = jnp.exp(g[..., -1])
    xs = tuple(jnp.moveaxis(t, 2, 0) for t in (qk, q_dec, k_dec, u, w, g_last))

    def step(state, inp):
        qk_i, qd_i, kd_i, u_i, w_i, gl_i = inp
        v_new = u_i - jnp.einsum('bhcd,bhde->bhce', w_i, state)
        o_i = jnp.einsum('bhcd,bhde->bhce', qd_i, state) + jnp.einsum('bhcs,bhse->bhce', qk_i, v_new)
        state = state * gl_i[..., None, None] + jnp.einsum('bhcd,bhce->bhde', kd_i, v_new)
        return state, o_i

    state_fin, o = lax.scan(step, state0, xs)
    o = jnp.moveaxis(o, 0, 2).reshape(b_, h_, s_, dv)
    return o, state_fin


def dn_project(h, w_in, conv_w, a_log, dt_bias):
    b_, s_, _ = h.shape
    p = h @ w_in
    qkv, z, ab = jnp.split(p, [DN_CONV_DIM, DN_CONV_DIM + DN_V_DIM], axis=-1)
    qkv = jax.nn.silu(centred_dwconv(qkv, conv_w)).astype(jnp.float32)
    q, k, v = jnp.split(qkv, [DN_QK_DIM, 2 * DN_QK_DIM], axis=-1)
    rep = DN_V_HEADS // DN_QK_HEADS

    def qk_heads(t):
        t = l2norm(t.reshape(b_, s_, DN_QK_HEADS, DN_HEAD_DIM))
        return jnp.repeat(t, rep, axis=2).transpose(0, 2, 1, 3)

    q = qk_heads(q) * (DN_HEAD_DIM ** -0.5)
    k = qk_heads(k)
    v = v.reshape(b_, s_, DN_V_HEADS, DN_HEAD_DIM).transpose(0, 2, 1, 3)
    ab = ab.astype(jnp.float32).reshape(b_, s_, 2, 2, DN_V_HEADS)
    g = -jnp.exp(a_log.astype(jnp.float32)) * jax.nn.softplus(ab[:, :, :, 0] + dt_bias.astype(jnp.float32))
    beta = jax.nn.sigmoid(ab[:, :, :, 1])
    return q, k, v, g.transpose(2, 0, 3, 1), beta.transpose(2, 0, 3, 1), z


def dn_output(o, z, norm_g, w_out, dtype):
    b_, _, s_, _ = o.shape
    o = o.transpose(0, 2, 1, 3)
    o = o * lax.rsqrt(jnp.mean(o * o, axis=-1, keepdims=True) + EPS) * norm_g.astype(jnp.float32)
    o = o * jax.nn.silu(z.astype(jnp.float32).reshape(b_, s_, DN_V_HEADS, DN_HEAD_DIM))
    return o.reshape(b_, s_, DN_V_DIM).astype(dtype) @ w_out


def deltanet_mixer(hl, hc, w_in, conv_w, a_log, dt_bias, norm_g, w_out, ctx_out):
    ql, kl, vl, gl, bl, zl = dn_project(hl, w_in, conv_w, a_log, dt_bias)
    qc, kc, vc, gc, bc, zc = dn_project(hc, w_in, conv_w, a_log, dt_bias)
    state0 = jnp.zeros((hl.shape[0], DN_V_HEADS, DN_HEAD_DIM, DN_HEAD_DIM), jnp.float32)
    ol = None
    oc = None
    for d, rev in enumerate((False, True)):
        f = (lambda t: jnp.flip(t, axis=2)) if rev else (lambda t: t)
        oc_d, s_ctx = gated_delta_chunked(f(qc), f(kc), f(vc), f(gc[d]), f(bc[d]), state0)
        ol_d, _ = gated_delta_chunked(f(ql), f(kl), f(vl), f(gl[d]), f(bl[d]), s_ctx)
        ol = f(ol_d) if ol is None else ol + f(ol_d)
        if ctx_out:
            oc = f(oc_d) if oc is None else oc + f(oc_d)
    yl = dn_output(ol, zl, norm_g, w_out, hl.dtype)
    yc = dn_output(oc, zc, norm_g, w_out, hc.dtype) if ctx_out else None
    return yl, yc


def chunk_mlp_mixer(h, w_in, ln_g, ln_b, w_s, b_s, w_out):
    b_, s_, _ = h.shape
    u, v = jnp.split(jax.nn.gelu(h @ w_in, approximate=False), 2, axis=-1)
    v = layernorm(v, ln_g, ln_b).reshape(b_, s_ // SG_CHUNK, SG_CHUNK, SG_GROUPS, SG_GROUP_DIM)
    v = jnp.einsum('gpq,bnqgc->bnpgc', w_s, v) + b_s.T[None, None, :, :, None]
    return (u * v.reshape(b_, s_, SG_DIM)) @ w_out


def setup_inputs(seed: int = 0) -> dict:
    key = jax.random.key(seed)
    ks = jax.random.split(key, 24)
    d = D_MODEL

    def nrm(k, shape, s):
        return jax.random.normal(k, shape, jnp.float32) * s

    a_init = jax.random.uniform(ks[12], (N_A, 2, DN_V_HEADS), jnp.float32, 1.0, 16.0)
    dt = jnp.exp(jax.random.uniform(ks[13], (N_A, 2, DN_V_HEADS), jnp.float32,
                                    math.log(1e-3), math.log(1e-1)))
    return {
        'x': nrm(ks[0], (BATCH, SEQ, d), 1.0),
        'c': nrm(ks[1], (BATCH, d), 1.0),
        'ctx': nrm(ks[2], (BATCH, CTX_LEN, d), 1.0),
        'c_ctx': nrm(ks[3], (d,), 1.0),
        'mod_w': nrm(ks[4], (DEPTH, d, N_MOD * d), d ** -0.5),
        'mod_b': nrm(ks[5], (DEPTH, N_MOD * d), 0.02),
        'norm_g': 1.0 + nrm(ks[6], (DEPTH, 3, d), 0.02),
        'ffn_w_gu': nrm(ks[7], (DEPTH, 2, d, 2 * D_FF), d ** -0.5),
        'ffn_w_down': nrm(ks[8], (DEPTH, 2, D_FF, d), D_FF ** -0.5),
        'dn_w_in': nrm(ks[9], (N_A, d, DN_IN_DIM), d ** -0.5),
        'dn_conv_w': nrm(ks[10], (N_A, DN_CONV_K, DN_CONV_DIM), DN_CONV_K ** -0.5),
        'dn_a_log': jnp.log(a_init),
        'dn_dt_bias': dt + jnp.log(-jnp.expm1(-dt)),
        'dn_norm_g': 1.0 + nrm(ks[11], (N_A, DN_HEAD_DIM), 0.02),
        'dn_w_out': nrm(ks[14], (N_A, DN_V_DIM, d), DN_V_DIM ** -0.5),
        'sg_w_in': nrm(ks[15], (N_B, d, 2 * SG_DIM), d ** -0.5),
        'sg_ln_g': 1.0 + nrm(ks[16], (N_B, SG_DIM), 0.02),
        'sg_ln_b': nrm(ks[17], (N_B, SG_DIM), 0.02),
        'sg_w_s': nrm(ks[18], (N_B, SG_GROUPS, SG_CHUNK, SG_CHUNK), SG_CHUNK ** -0.5),
        'sg_b_s': 1.0 + nrm(ks[19], (N_B, SG_GROUPS, SG_CHUNK), 0.02),
        'sg_w_out': nrm(ks[20], (N_B, SG_DIM, d), SG_DIM ** -0.5),
        'final_norm_g': 1.0 + nrm(ks[21], (d,), 0.02),
    }


def reference(x, c, ctx, c_ctx, mod_w, mod_b, norm_g, ffn_w_gu, ffn_w_down, dn_w_in, dn_conv_w,
              dn_a_log, dn_dt_bias, dn_norm_g, dn_w_out, sg_w_in, sg_ln_g, sg_ln_b, sg_w_s, sg_b_s,
              sg_w_out, final_norm_g):
    xl, xc = x, ctx
    sc_l = jax.nn.silu(c)
    sc_c = jax.nn.silu(c_ctx)
    for i in range(DEPTH):
        last = i == DEPTH - 1
        kind = i % N_MIXERS
        j = i // N_MIXERS
        col_major = (j % 2) == 1
        need_ctx = not (last and kind == 1)
        mod_l = jnp.split((sc_l @ mod_w[i] + mod_b[i])[:, None, :], N_MOD, axis=-1)
        mod_c = jnp.split((sc_c @ mod_w[i] + mod_b[i])[None, None, :], N_MOD, axis=-1) if need_ctx else None

        xl = half_ffn(xl, norm_g[i, 0], mod_l[0:3], ffn_w_gu[i, 0], ffn_w_down[i, 0])
        if need_ctx:
            xc = half_ffn(xc, norm_g[i, 0], mod_c[0:3], ffn_w_gu[i, 0], ffn_w_down[i, 0])

        hl = to_scan_order(modulate(rmsnorm(xl, norm_g[i, 1]), mod_l[3], mod_l[4]), col_major)
        hc = modulate(rmsnorm(xc, norm_g[i, 1]), mod_c[3], mod_c[4]) if need_ctx else None
        if kind == 0:
            yl, yc = deltanet_mixer(hl, hc, dn_w_in[j], dn_conv_w[j], dn_a_log[j], dn_dt_bias[j],
                                    dn_norm_g[j], dn_w_out[j], ctx_out=not last)
        else:
            yl = chunk_mlp_mixer(hl, sg_w_in[j], sg_ln_g[j], sg_ln_b[j], sg_w_s[j], sg_b_s[j], sg_w_out[j])
            yc = (chunk_mlp_mixer(hc, sg_w_in[j], sg_ln_g[j], sg_ln_b[j], sg_w_s[j], sg_b_s[j], sg_w_out[j])
                  if not last else None)
        xl = xl + mod_l[5] * from_scan_order(yl, col_major)
        if not last:
            xc = xc + mod_c[5] * yc

        xl = half_ffn(xl, norm_g[i, 2], mod_l[6:9], ffn_w_gu[i, 1], ffn_w_down[i, 1])
        if not last:
            xc = half_ffn(xc, norm_g[i, 2], mod_c[6:9], ffn_w_gu[i, 1], ffn_w_down[i, 1])
    return rmsnorm(xl, final_norm_g)
```

```python
import functools

import numpy as np
import jax
import jax.numpy as jnp
from jax import lax
from jax.experimental import pallas as pl
from jax.experimental.pallas import tpu as pltpu

F32 = jnp.float32
BF16 = jnp.bfloat16
EPS = 1e-6
GRID_W = 64
N_MOD = 9
DN_HEAD_DIM = 128
DN_CHUNK = 64
DN_CONV_K = 5
SG_CHUNK = 128
SG_GROUPS = 16

V7X_LANES = 128
V7X_SUBLANES = 8
V7X_VMEM_BYTES = 64 * 1024 * 1024
VMEM_LIMIT = V7X_VMEM_BYTES - 8 * 1024 * 1024
MOD_ROWS = 16
HIGHEST = lax.Precision.HIGHEST


def _cparams(*sem):
    return pltpu.CompilerParams(dimension_semantics=sem, vmem_limit_bytes=VMEM_LIMIT)


def _dot(a, b):
    return jnp.dot(a, b, preferred_element_type=F32)


def _dot_nt(a, b):
    return lax.dot_general(a, b, (((1,), (1,)), ((), ())), preferred_element_type=F32)


def _dot_tn(a, b):
    return lax.dot_general(a, b, (((0,), (0,)), ((), ())), preferred_element_type=F32)


def _rms(x):
    return x * lax.rsqrt(jnp.mean(x * x, axis=-1, keepdims=True) + EPS)


def _load_rows(ref, ncol):
    if not ncol:
        return ref[...]
    width = ref.shape[1] // ncol
    return jnp.concatenate([ref[:, k * width:(k + 1) * width] for k in range(ncol)], axis=0)


def _store_rows(ref, val, ncol):
    if not ncol:
        ref[...] = val
        return
    width = ref.shape[1] // ncol
    rows = ref.shape[0]
    for k in range(ncol):
        ref[:, k * width:(k + 1) * width] = val[k * rows:(k + 1) * rows, :]


def _row_view(arr, tm, seq, ncol):
    width = arr.shape[1]
    if not ncol:
        return arr, pl.BlockSpec((tm, width), lambda i, *_: (i, 0))
    grid_rows = seq // GRID_W
    per_batch = GRID_W // ncol
    view = arr.reshape(arr.shape[0] // seq, grid_rows, GRID_W * width)
    return view, pl.BlockSpec((None, grid_rows, ncol * width), lambda i, *_: (i // per_batch, 0, i % per_batch))


def _mod_spec(k, d, bidx):
    return pl.BlockSpec((None, 1, d), lambda i, *_: (bidx(i), 0, k))


def _mod_kernel(c_ref, w_ref, b_ref, o_ref):
    s = jax.nn.silu(c_ref[...]).astype(BF16)
    o_ref[...] = _dot(s, w_ref[...].astype(BF16)) + b_ref[...]


def _modulation(cs, mod_w, mod_b):
    depth, d, nm = mod_w.shape
    tn = 1024
    return pl.pallas_call(
        _mod_kernel,
        grid=(depth, nm // tn),
        in_specs=[pl.BlockSpec((MOD_ROWS, d), lambda l, j: (0, 0)),
                  pl.BlockSpec((None, d, tn), lambda l, j: (l, 0, j)),
                  pl.BlockSpec((None, 1, tn), lambda l, j: (l, 0, j))],
        out_specs=pl.BlockSpec((None, MOD_ROWS, tn), lambda l, j: (l, 0, j)),
        out_shape=jax.ShapeDtypeStruct((depth, MOD_ROWS, nm), F32),
        compiler_params=_cparams("parallel", "parallel"),
        name="modulation",
    )(cs, mod_w, mod_b.reshape(depth, 1, nm))


def _ffn_kernel(x_ref, ng_ref, sh_ref, sc_ref, gt_ref, wg_ref, wu_ref, wd_ref, *rest, final_norm):
    if final_norm:
        fg_ref, o_ref, xn_ref = rest
    else:
        o_ref, xn_ref = rest
    j = pl.program_id(1)

    @pl.when(j == 0)
    def _():
        y = _rms(x_ref[...]) * ng_ref[...]
        xn_ref[...] = (y * (1.0 + sc_ref[...]) + sh_ref[...]).astype(BF16)

    xn = xn_ref[...]
    g = _dot(xn, wg_ref[...])
    u = _dot(xn, wu_ref[...])
    h = (jax.nn.silu(g) * u).astype(BF16)
    dn = _dot(h, wd_ref[...])

    @pl.when(j == 0)
    def _():
        o_ref[...] = dn

    @pl.when(j > 0)
    def _():
        o_ref[...] += dn

    @pl.when(j == pl.num_programs(1) - 1)
    def _():
        r = x_ref[...] + 0.5 * gt_ref[...] * o_ref[...]
        if final_norm:
            r = _rms(r) * fg_ref[...]
        o_ref[...] = r


def _half_ffn(x, mods, k0, bidx, norm_g, w_gu, w_down, layer, half, tm, final_g=None):
    rows, d = x.shape
    dff = w_down.shape[2]
    fc = 512
    nf = dff // fc
    row = pl.BlockSpec((tm, d), lambda i, j: (i, 0))
    vec = pl.BlockSpec((1, d), lambda i, j: (0, 0))
    in_specs = [row, vec, _mod_spec(k0, d, bidx), _mod_spec(k0 + 1, d, bidx), _mod_spec(k0 + 2, d, bidx),
                pl.BlockSpec((None, None, d, fc), lambda i, j: (layer, half, 0, j)),
                pl.BlockSpec((None, None, d, fc), lambda i, j: (layer, half, 0, nf + j)),
                pl.BlockSpec((None, None, fc, d), lambda i, j: (layer, half, j, 0))]
    args = [x, norm_g.reshape(1, d), mods, mods, mods, w_gu, w_gu, w_down]
    if final_g is not None:
        in_specs.append(vec)
        args.append(final_g.reshape(1, d))
    return pl.pallas_call(
        functools.partial(_ffn_kernel, final_norm=final_g is not None),
        grid=(rows // tm, nf),
        in_specs=in_specs,
        out_specs=row,
        out_shape=jax.ShapeDtypeStruct((rows, d), F32),
        scratch_shapes=[pltpu.VMEM((tm, d), BF16)],
        compiler_params=_cparams("parallel", "arbitrary"),
        name="half_ffn",
    )(*args)


def _prenorm_linear_kernel(x_ref, ng_ref, sh_ref, sc_ref, w_ref, *rest, ncol, act, has_extra):
    if has_extra:
        wx_ref, o_ref, ox_ref, xn_ref = rest
    else:
        o_ref, xn_ref = rest

    @pl.when(pl.program_id(1) == 0)
    def _():
        y = _rms(_load_rows(x_ref, ncol)) * ng_ref[...]
        xn = (y * (1.0 + sc_ref[...]) + sh_ref[...]).astype(BF16)
        xn_ref[...] = xn
        if has_extra:
            ox_ref[...] = _dot(xn, wx_ref[...])

    y = _dot(xn_ref[...], w_ref[...])
    if act == "gelu":
        y = 0.5 * y * (1.0 + lax.erf(y * (0.5 ** 0.5)))
    o_ref[...] = y.astype(o_ref.dtype)


def _prenorm_linear(x, mods, k0, bidx, norm_g, w, tm, seq, ncol, act=None, w_extra=None, tn=512):
    rows, d = x.shape
    n = w.shape[1]
    xv, xspec = _row_view(x, tm, seq, ncol)
    vec = pl.BlockSpec((1, d), lambda i, j: (0, 0))
    in_specs = [xspec, vec, _mod_spec(k0, d, bidx), _mod_spec(k0 + 1, d, bidx),
                pl.BlockSpec((d, tn), lambda i, j: (0, j))]
    args = [xv, norm_g.reshape(1, d), mods, mods, w]
    out_specs = [pl.BlockSpec((tm, tn), lambda i, j: (i, j))]
    out_shape = [jax.ShapeDtypeStruct((rows, n), F32)]
    if w_extra is not None:
        nx = w_extra.shape[1]
        in_specs.append(pl.BlockSpec((d, nx), lambda i, j: (0, 0)))
        args.append(w_extra)
        out_specs.append(pl.BlockSpec((tm, nx), lambda i, j: (i, 0)))
        out_shape.append(jax.ShapeDtypeStruct((rows, nx), F32))
    outs = pl.pallas_call(
        functools.partial(_prenorm_linear_kernel, ncol=ncol, act=act, has_extra=w_extra is not None),
        grid=(rows // tm, n // tn),
        in_specs=in_specs,
        out_specs=out_specs,
        out_shape=out_shape,
        scratch_shapes=[pltpu.VMEM((tm, d), BF16)],
        compiler_params=_cparams("parallel", "arbitrary"),
        name="mixer_in_proj",
    )(*args)
    return outs if w_extra is not None else outs[0]


_CONV_PAD = V7X_SUBLANES


def _dn_conv_kernel(p_ref, cw_ref, o_ref, xp_ref, *, seq, sub, n_q, n_qk):
    c = pl.program_id(1)
    cb = p_ref.shape[1]
    half = DN_CONV_K // 2
    xp_ref[0:_CONV_PAD, :] = jnp.zeros((_CONV_PAD, cb), F32)
    xp_ref[_CONV_PAD:_CONV_PAD + seq, :] = p_ref[...]
    xp_ref[_CONV_PAD + seq:, :] = jnp.zeros((_CONV_PAD, cb), F32)
    w = cw_ref[...]

    def conv_tile(i):
        r0 = pl.multiple_of(i * sub, sub)
        win = xp_ref[pl.ds(r0, sub + 2 * _CONV_PAD), :]
        acc = None
        for t in range(DN_CONV_K):
            lo = _CONV_PAD - half + t
            term = win[lo:lo + sub, :] * w[t:t + 1, :]
            acc = term if acc is None else acc + term
        return r0, jax.nn.silu(acc)

    @pl.when(c < n_qk)
    def _():
        scale = jnp.where(c < n_q, DN_HEAD_DIM ** -0.5, 1.0).astype(F32)

        def body(i, carry):
            r0, y = conv_tile(i)
            for hh in range(cb // DN_HEAD_DIM):
                seg = y[:, hh * DN_HEAD_DIM:(hh + 1) * DN_HEAD_DIM]
                seg = seg * lax.rsqrt(jnp.sum(seg * seg, axis=-1, keepdims=True) + EPS)
                o_ref[pl.ds(r0, sub), hh * DN_HEAD_DIM:(hh + 1) * DN_HEAD_DIM] = seg * scale
            return carry

        lax.fori_loop(0, seq // sub, body, 0)

    @pl.when(c >= n_qk)
    def _():
        def body(i, carry):
            r0, y = conv_tile(i)
            o_ref[pl.ds(r0, sub), :] = y
            return carry

        lax.fori_loop(0, seq // sub, body, 0)


def _dn_conv(p, conv_w, seq, qk_dim, conv_dim):
    rows = p.shape[0]
    cb = 256
    sub = min(seq, 256)
    return pl.pallas_call(
        functools.partial(_dn_conv_kernel, seq=seq, sub=sub, n_q=qk_dim // cb, n_qk=2 * qk_dim // cb),
        grid=(rows // seq, conv_dim // cb),
        in_specs=[pl.BlockSpec((seq, cb), lambda s, c: (s, c)),
                  pl.BlockSpec((DN_CONV_K, cb), lambda s, c: (0, c))],
        out_specs=pl.BlockSpec((seq, cb), lambda s, c: (s, c)),
        out_shape=jax.ShapeDtypeStruct((rows, conv_dim), F32),
        scratch_shapes=[pltpu.VMEM((seq + 2 * _CONV_PAD, cb), F32)],
        compiler_params=_cparams("parallel", "parallel"),
        name="dn_conv",
    )(p, conv_w)


def _dn_gate_kernel(ab_ref, alog_ref, dtb_ref, pm_ref, gc_ref, gr_ref, *, hv):
    tm, width = ab_ref.shape
    tile = V7X_LANES
    x = ab_ref[...]
    lane_full = lax.broadcasted_iota(jnp.int32, (tm, width), 1)
    lane = lax.broadcasted_iota(jnp.int32, (tile, width), 1)
    is_a = (lane % (2 * hv)) < hv
    is_rev = lane >= 2 * hv
    g = -jnp.exp(alog_ref[...]) * jax.nn.softplus(x + dtb_ref[...])
    raw = jnp.where((lane_full % (2 * hv)) < hv, g, jax.nn.sigmoid(x))
    ri = lax.broadcasted_iota(jnp.int32, (tile, tile), 0)
    ci = lax.broadcasted_iota(jnp.int32, (tile, tile), 1)
    same = (ri // DN_CHUNK) == (ci // DN_CHUNK)
    low = jnp.where(same & (ci <= ri), 1.0, 0.0).astype(F32)
    upp = jnp.where(same & (ci >= ri), 1.0, 0.0).astype(F32)
    for t in range(tm // tile):
        blk = raw[t * tile:(t + 1) * tile, :]
        pre = jnp.dot(low, blk, precision=HIGHEST, preferred_element_type=F32)
        suf = jnp.dot(upp, blk, precision=HIGHEST, preferred_element_type=F32)
        out = jnp.where(is_a, jnp.where(is_rev, suf, pre), blk)
        gc_ref[t * tile:(t + 1) * tile, :] = out
        gr_ref[t] = lax.dot_general(pm_ref[...], out, (((1,), (1,)), ((), ())),
                                    precision=HIGHEST, preferred_element_type=F32)


def _gate_perm(hv):
    pm = np.zeros((4 * hv, 4 * hv), np.float32)
    for hq in range(hv // 2):
        for d in range(2):
            for isb in range(2):
                for j in range(2):
                    pm[hq * 8 + d * 4 + isb * 2 + j, d * 2 * hv + isb * hv + 2 * hq + j] = 1.0
    return jnp.asarray(pm)


def _dn_gates(ab, a_log, dt_bias, hv, tm):
    rows, width = ab.shape
    assert width == 4 * hv == V7X_LANES
    zeros = jnp.zeros_like(a_log)
    alog = jnp.concatenate([a_log, zeros], axis=1).reshape(1, width)
    dtb = jnp.concatenate([dt_bias, zeros], axis=1).reshape(1, width)
    vec = pl.BlockSpec((1, width), lambda i: (0, 0))
    return pl.pallas_call(
        functools.partial(_dn_gate_kernel, hv=hv),
        grid=(rows // tm,),
        in_specs=[pl.BlockSpec((tm, width), lambda i: (i, 0)), vec, vec,
                  pl.BlockSpec((width, width), lambda i: (0, 0))],
        out_specs=[pl.BlockSpec((tm, width), lambda i: (i, 0)),
                   pl.BlockSpec((tm // V7X_LANES, width, V7X_LANES), lambda i: (i, 0, 0))],
        out_shape=[jax.ShapeDtypeStruct((rows, width), F32),
                   jax.ShapeDtypeStruct((rows // V7X_LANES, width, V7X_LANES), F32)],
        compiler_params=_cparams("parallel"),
        name="dn_gates",
    )(ab, alog, dtb, _gate_perm(hv))


_INV_BLOCK = 16


def _unit_tri_inverse(a, ri, ci):
    eye = (ri == ci).astype(F32)
    bi, bj = ri // _INV_BLOCK, ci // _INV_BLOCK

    def mm(p, q):
        return _dot(p.astype(BF16), q.astype(BF16))

    d = jnp.where(bi == bj, a, 0.0)
    t = eye - d
    power = 1
    while 2 * power < _INV_BLOCK:
        d = mm(d, d)
        power *= 2
        t = mm(t, eye + d)
    w = 1
    while w * _INV_BLOCK < a.shape[0]:
        e = jnp.where((bi // (2 * w) == bj // (2 * w)) & (bi // w != bj // w), a, 0.0)
        t = t - mm(mm(t, e), t)
        w *= 2
    return t


def _dn_core_kernel(qc_ref, kc_ref, vc_ref, gcc_ref, grc_ref, ql_ref, kl_ref, vl_ref, gcl_ref, grl_ref,
                    oc_ref, ol_ref, s_ref, *, hv):
    hq = pl.program_id(1)
    dk = DN_HEAD_DIM
    cs = DN_CHUNK
    tile = V7X_LANES
    s_ref[...] = jnp.zeros(s_ref.shape, F32)
    oc_ref[...] = jnp.zeros(oc_ref.shape, F32)
    ol_ref[...] = jnp.zeros(ol_ref.shape, F32)
    ri = lax.broadcasted_iota(jnp.int32, (cs, cs), 0)
    ci = lax.broadcasted_iota(jnp.int32, (cs, cs), 1)
    shift = (tile - 2 * hq) % tile

    def chunk(refs, t, c, d):
        q_ref, k_ref, v_ref, gc_ref, gr_ref, o_ref = refs
        r0 = pl.multiple_of(t * tile, tile) + c * cs
        incl = (ri >= ci) if d == 0 else (ri <= ci)
        strict = (ri > ci) if d == 0 else (ri < ci)
        last = cs - 1 if d == 0 else 0
        k = k_ref[pl.ds(r0, cs), :]
        kb = k.astype(BF16)
        qb = q_ref[pl.ds(r0, cs), :].astype(BF16)
        kk = _dot_nt(kb, kb)
        qk_raw = _dot_nt(qb, kb)
        gct = pltpu.roll(gc_ref[pl.ds(r0, cs), :], shift, 1)
        grt = gr_ref[t]
        for j in range(2):
            si = d * 2 + j
            gcol = gct[:, d * 2 * hv + j:d * 2 * hv + j + 1]
            bcol = gct[:, d * 2 * hv + hv + j:d * 2 * hv + hv + j + 1]
            grow = grt[d * 4 + j:d * 4 + j + 1, c * cs:(c + 1) * cs]
            brow = grt[d * 4 + 2 + j:d * 4 + 3 + j, c * cs:(c + 1) * cs]
            dec = jnp.exp(jnp.where(incl, gcol - grow, -jnp.inf))
            a = jnp.where(strict, kk * dec * bcol, 0.0)
            tinv = _unit_tri_inverse(a, ri, ci)
            tu = (tinv * brow).astype(BF16)
            tw = (tinv * (brow * jnp.exp(grow))).astype(BF16)
            vb = v_ref[pl.ds(r0, cs), j * dk:(j + 1) * dk].astype(BF16)
            u = _dot(tu, vb)
            w = _dot(tw, kb)
            state = s_ref[si]
            sb = state.astype(BF16)
            v_new = (u - _dot(w.astype(BF16), sb)).astype(BF16)
            o = _dot(qb, sb) * jnp.exp(gcol) + _dot((qk_raw * dec).astype(BF16), v_new)
            o_ref[pl.ds(r0, cs), j * dk:(j + 1) * dk] += o
            g_last = gcol[last:last + 1, :]
            k_dec = (k * jnp.exp(g_last - gcol)).astype(BF16)
            s_ref[si] = state * jnp.exp(g_last) + _dot_tn(k_dec, v_new)

    def run(refs, n_tiles):
        def body(i, carry):
            for c in range(tile // cs):
                chunk(refs, i, c, 0)
            for c in reversed(range(tile // cs)):
                chunk(refs, n_tiles - 1 - i, c, 1)
            return carry

        lax.fori_loop(0, n_tiles, body, 0)

    run((qc_ref, kc_ref, vc_ref, gcc_ref, grc_ref, oc_ref), qc_ref.shape[0] // tile)
    run((ql_ref, kl_ref, vl_ref, gcl_ref, grl_ref, ol_ref), ql_ref.shape[0] // tile)


def _dn_core(qkv_c, gc_c, gr_c, qkv_l, gc_l, gr_l, batch, hv):
    dk = DN_HEAD_DIM
    hq_n = hv // 2
    ctx_len = qkv_c.shape[0] // batch
    seq = qkv_l.shape[0] // batch
    width = gc_l.shape[1]

    def specs(t):
        return [pl.BlockSpec((t, dk), lambda b, h: (b, h)),
                pl.BlockSpec((t, dk), lambda b, h: (b, hq_n + h)),
                pl.BlockSpec((t, 2 * dk), lambda b, h: (b, hq_n + h)),
                pl.BlockSpec((t, width), lambda b, h: (b, 0)),
                pl.BlockSpec((t // V7X_LANES, 8, V7X_LANES), lambda b, h: (b, h, 0))]

    return pl.pallas_call(
        functools.partial(_dn_core_kernel, hv=hv),
        grid=(batch, hq_n),
        in_specs=specs(ctx_len) + specs(seq),
        out_specs=[pl.BlockSpec((ctx_len, 2 * dk), lambda b, h: (b, h)),
                   pl.BlockSpec((seq, 2 * dk), lambda b, h: (b, h))],
        out_shape=[jax.ShapeDtypeStruct((batch * ctx_len, hv * dk), F32),
                   jax.ShapeDtypeStruct((batch * seq, hv * dk), F32)],
        scratch_shapes=[pltpu.VMEM((4, dk, dk), F32)],
        compiler_params=_cparams("parallel", "parallel"),
        name="dn_core",
    )(qkv_c, qkv_c, qkv_c, gc_c, gr_c, qkv_l, qkv_l, qkv_l, gc_l, gr_l)


def _dn_out_kernel(o_ref, z_ref, ng_ref, w_ref, x_ref, gt_ref, out_ref, acc_ref, *, ncol):
    k = pl.program_id(1)
    o = o_ref[...]
    dk = DN_HEAD_DIM
    segs = [_rms(o[:, h * dk:(h + 1) * dk]) * ng_ref[...] for h in range(o.shape[1] // dk)]
    on = jnp.concatenate(segs, axis=1) * jax.nn.silu(z_ref[...])
    part = _dot(on.astype(BF16), w_ref[...])

    @pl.when(k == 0)
    def _():
        acc_ref[...] = part

    @pl.when(k > 0)
    def _():
        acc_ref[...] += part

    @pl.when(k == pl.num_programs(1) - 1)
    def _():
        _store_rows(out_ref, _load_rows(x_ref, ncol) + gt_ref[...] * acc_ref[...], ncol)


def _dn_out(o, p, z_col0, norm_g, w_out, x, mods, bidx, tm, seq, ncol):
    rows, d = x.shape
    kdim = w_out.shape[0]
    tk = 512
    xv, xspec = _row_view(x, tm, seq, ncol)
    zb = z_col0 // tk
    out = pl.pallas_call(
        functools.partial(_dn_out_kernel, ncol=ncol),
        grid=(rows // tm, kdim // tk),
        in_specs=[pl.BlockSpec((tm, tk), lambda i, k: (i, k)),
                  pl.BlockSpec((tm, tk), lambda i, k: (i, zb + k)),
                  pl.BlockSpec((1, DN_HEAD_DIM), lambda i, k: (0, 0)),
                  pl.BlockSpec((tk, d), lambda i, k: (k, 0)),
                  xspec, _mod_spec(5, d, bidx)],
        out_specs=xspec,
        out_shape=jax.ShapeDtypeStruct(xv.shape, F32),
        scratch_shapes=[pltpu.VMEM((tm, d), F32)],
        compiler_params=_cparams("parallel", "arbitrary"),
        name="dn_out",
    )(o, p, norm_g.reshape(1, DN_HEAD_DIM), w_out, xv, mods)
    return out.reshape(rows, d)


def _sg_out_kernel(u_ref, v_ref, lg_ref, lb_ref, ws_ref, bs_ref, w_ref, x_ref, gt_ref, out_ref, m_ref, *, ncol):
    v = v_ref[...]
    mu = jnp.mean(v, axis=-1, keepdims=True)
    var = jnp.mean(jnp.square(v - mu), axis=-1, keepdims=True)
    m_ref[...] = ((v - mu) * lax.rsqrt(var + EPS) * lg_ref[...] + lb_ref[...]).astype(BF16)
    tm, sg_dim = v.shape
    gd = sg_dim // SG_GROUPS
    for g in range(SG_GROUPS):
        wsg = ws_ref[g]
        bias = bs_ref[:, g:g + 1]
        for c in range(tm // SG_CHUNK):
            rs = slice(c * SG_CHUNK, (c + 1) * SG_CHUNK)
            cs = slice(g * gd, (g + 1) * gd)
            mixed = _dot(wsg, m_ref[rs, cs]) + bias
            m_ref[rs, cs] = (u_ref[rs, cs] * mixed).astype(BF16)
    y = _dot(m_ref[...], w_ref[...])
    _store_rows(out_ref, _load_rows(x_ref, ncol) + gt_ref[...] * y, ncol)


def _sg_out(uv, ln_g, ln_b, w_s, b_s, w_out, x, mods, bidx, tm, seq, ncol):
    rows, d = x.shape
    sg_dim = w_out.shape[0]
    xv, xspec = _row_view(x, tm, seq, ncol)
    vec = pl.BlockSpec((1, sg_dim), lambda i: (0, 0))
    out = pl.pallas_call(
        functools.partial(_sg_out_kernel, ncol=ncol),
        grid=(rows // tm,),
        in_specs=[pl.BlockSpec((tm, sg_dim), lambda i: (i, 0)),
                  pl.BlockSpec((tm, sg_dim), lambda i: (i, 1)),
                  vec, vec,
                  pl.BlockSpec((SG_GROUPS, SG_CHUNK, SG_CHUNK), lambda i: (0, 0, 0)),
                  pl.BlockSpec((SG_CHUNK, SG_GROUPS), lambda i: (0, 0)),
                  pl.BlockSpec((sg_dim, d), lambda i: (0, 0), pipeline_mode=pl.Buffered(1)),
                  xspec, _mod_spec(5, d, bidx)],
        out_specs=xspec,
        out_shape=jax.ShapeDtypeStruct(xv.shape, F32),
        scratch_shapes=[pltpu.VMEM((tm, sg_dim), BF16)],
        compiler_params=_cparams("parallel"),
        name="sg_out",
    )(uv, uv, ln_g.reshape(1, sg_dim), ln_b.reshape(1, sg_dim), w_s, b_s.T, w_out, xv, mods)
    return out.reshape(rows, d)


def kernel(x, c, ctx, c_ctx, mod_w, mod_b, norm_g, ffn_w_gu, ffn_w_down, dn_w_in, dn_conv_w, dn_a_log, dn_dt_bias,
           dn_norm_g, dn_w_out, sg_w_in, sg_ln_g, sg_ln_b, sg_w_s, sg_b_s, sg_w_out, final_norm_g):
    batch, seq, d = x.shape
    ctx_len = ctx.shape[1]
    depth = mod_w.shape[0]
    nm = mod_w.shape[2]
    hv = dn_a_log.shape[2]
    dk = DN_HEAD_DIM
    qk_dim = hv // 2 * dk
    v_dim = hv * dk
    conv_dim = 2 * qk_dim + v_dim
    assert batch + 1 <= MOD_ROWS

    xl = x.reshape(batch * seq, d)
    xc = ctx.reshape(batch * ctx_len, d)
    tm_l = min(512, seq)
    tm_c = min(512, batch * ctx_len)
    tiles_per_seq = seq // tm_l
    bidx_l = lambda i: i // tiles_per_seq
    bidx_c = lambda i: batch
    ncol_cm = tm_l // (seq // GRID_W)

    cs = jnp.concatenate([c, c_ctx[None, :], jnp.zeros((MOD_ROWS - batch - 1, d), F32)], axis=0)
    mods_all = _modulation(cs, mod_w, mod_b).reshape(depth, MOD_ROWS, 1, nm)

    w_gu = ffn_w_gu.astype(BF16)
    w_down = ffn_w_down.astype(BF16)
    n_mix = 2

    for i in range(depth):
        last = i == depth - 1
        kind = i % n_mix
        j = i // n_mix
        ncol = ncol_cm if (j % 2) == 1 else 0
        need_ctx = not (last and kind == 1)
        mods = mods_all[i]

        xl = _half_ffn(xl, mods, 0, bidx_l, norm_g[i, 0], w_gu, w_down, i, 0, tm_l)
        if need_ctx:
            xc = _half_ffn(xc, mods, 0, bidx_c, norm_g[i, 0], w_gu, w_down, i, 0, tm_c)

        if kind == 0:
            w_in = dn_w_in[j].astype(BF16)
            w_main, w_ab = w_in[:, :conv_dim + v_dim], w_in[:, conv_dim + v_dim:]
            w_out = dn_w_out[j].astype(BF16)
            pl_, ab_l = _prenorm_linear(xl, mods, 3, bidx_l, norm_g[i, 1], w_main, tm_l, seq, ncol, w_extra=w_ab)
            pc_, ab_c = _prenorm_linear(xc, mods, 3, bidx_c, norm_g[i, 1], w_main, tm_c, ctx_len, 0, w_extra=w_ab)
            qkv_l = _dn_conv(pl_, dn_conv_w[j], seq, qk_dim, conv_dim)
            qkv_c = _dn_conv(pc_, dn_conv_w[j], ctx_len, qk_dim, conv_dim)
            gc_l, gr_l = _dn_gates(ab_l, dn_a_log[j], dn_dt_bias[j], hv, tm_l)
            gc_c, gr_c = _dn_gates(ab_c, dn_a_log[j], dn_dt_bias[j], hv, tm_c)
            o_c, o_l = _dn_core(qkv_c, gc_c, gr_c, qkv_l, gc_l, gr_l, batch, hv)
            xl = _dn_out(o_l, pl_, conv_dim, dn_norm_g[j], w_out, xl, mods, bidx_l, tm_l, seq, ncol)
            if not last:
                xc = _dn_out(o_c, pc_, conv_dim, dn_norm_g[j], w_out, xc, mods, bidx_c, tm_c, ctx_len, 0)
        else:
            w_in = sg_w_in[j].astype(BF16)
            w_out = sg_w_out[j].astype(BF16)
            w_s = sg_w_s[j].astype(BF16)
            tm_sl = min(256, seq)
            tm_sc = min(256, batch * ctx_len)
            ncol_s = (tm_sl // (seq // GRID_W)) if ncol else 0
            bidx_sl = lambda i, n=seq // tm_sl: i // n
            uv_l = _prenorm_linear(xl, mods, 3, bidx_l, norm_g[i, 1], w_in, tm_l, seq, ncol, act="gelu")
            xl = _sg_out(uv_l, sg_ln_g[j], sg_ln_b[j], w_s, sg_b_s[j], w_out, xl, mods, bidx_sl, tm_sl, seq, ncol_s)
            if not last:
                uv_c = _prenorm_linear(xc, mods, 3, bidx_c, norm_g[i, 1], w_in, tm_c, ctx_len, 0, act="gelu")
                xc = _sg_out(uv_c, sg_ln_g[j], sg_ln_b[j], w_s, sg_b_s[j], w_out, xc, mods, bidx_c, tm_sc, ctx_len, 0)

        xl = _half_ffn(xl, mods, 6, bidx_l, norm_g[i, 2], w_gu, w_down, i, 1, tm_l,
                       final_g=final_norm_g if last else None)
        if not last:
            xc = _half_ffn(xc, mods, 6, bidx_c, norm_g[i, 2], w_gu, w_down, i, 1, tm_c)

    return xl.reshape(batch, seq, d)
```

```python
import functools

import numpy as np
import jax
import jax.numpy as jnp
from jax import lax
from jax.experimental import pallas as pl
from jax.experimental.pallas import tpu as pltpu

F32 = jnp.float32
BF16 = jnp.bfloat16
EPS = 1e-6
GRID_W = 64
N_MOD = 9
DN_HEAD_DIM = 128
DN_CHUNK = 64
DN_CONV_K = 5
SG_CHUNK = 128
SG_GROUPS = 16

V7X_LANES = 128
V7X_SUBLANES = 8
V7X_VMEM_BYTES = 64 * 1024 * 1024
VMEM_LIMIT = V7X_VMEM_BYTES - 8 * 1024 * 1024
MOD_ROWS = 16
HIGHEST = lax.Precision.HIGHEST


def _cparams(*sem):
    return pltpu.CompilerParams(dimension_semantics=sem, vmem_limit_bytes=VMEM_LIMIT)


def _dot(a, b):
    return jnp.dot(a, b, preferred_element_type=F32)


def _dot_nt(a, b):
    return lax.dot_general(a, b, (((1,), (1,)), ((), ())), preferred_element_type=F32)


def _dot_tn(a, b):
    return lax.dot_general(a, b, (((0,), (0,)), ((), ())), preferred_element_type=F32)


def _rms(x):
    return x * lax.rsqrt(jnp.mean(x * x, axis=-1, keepdims=True) + EPS)


def _load_rows(ref, ncol):
    if not ncol:
        return ref[...]
    width = ref.shape[1] // ncol
    return jnp.concatenate([ref[:, k * width:(k + 1) * width] for k in range(ncol)], axis=0)


def _store_rows(ref, val, ncol):
    if not ncol:
        ref[...] = val
        return
    width = ref.shape[1] // ncol
    rows = ref.shape[0]
    for k in range(ncol):
        ref[:, k * width:(k + 1) * width] = val[k * rows:(k + 1) * rows, :]


def _row_view(arr, tm, seq, ncol):
    width = arr.shape[1]
    if not ncol:
        return arr, pl.BlockSpec((tm, width), lambda i, *_: (i, 0))
    grid_rows = seq // GRID_W
    per_batch = GRID_W // ncol
    view = arr.reshape(arr.shape[0] // seq, grid_rows, GRID_W * width)
    return view, pl.BlockSpec((None, grid_rows, ncol * width), lambda i, *_: (i // per_batch, 0, i % per_batch))


def _mod_spec(k, d, bidx):
    return pl.BlockSpec((None, 1, d), lambda i, *_: (bidx(i), 0, k))


def _mod_kernel(c_ref, w_ref, b_ref, o_ref):
    s = jax.nn.silu(c_ref[...]).astype(BF16)
    o_ref[...] = _dot(s, w_ref[...].astype(BF16)) + b_ref[...]


def _modulation(cs, mod_w, mod_b):
    depth, d, nm = mod_w.shape
    tn = 1024
    return pl.pallas_call(
        _mod_kernel,
        grid=(depth, nm // tn),
        in_specs=[pl.BlockSpec((MOD_ROWS, d), lambda l, j: (0, 0)),
                  pl.BlockSpec((None, d, tn), lambda l, j: (l, 0, j)),
                  pl.BlockSpec((None, 1, tn), lambda l, j: (l, 0, j))],
        out_specs=pl.BlockSpec((None, MOD_ROWS, tn), lambda l, j: (l, 0, j)),
        out_shape=jax.ShapeDtypeStruct((depth, MOD_ROWS, nm), F32),
        compiler_params=_cparams("parallel", "parallel"),
        name="modulation",
    )(cs, mod_w, mod_b.reshape(depth, 1, nm))


def _ffn_kernel(x_ref, ng_ref, sh_ref, sc_ref, gt_ref, wg_ref, wu_ref, wd_ref, *rest, final_norm):
    if final_norm:
        fg_ref, o_ref, xn_ref = rest
    else:
        o_ref, xn_ref = rest
    j = pl.program_id(1)

    @pl.when(j == 0)
    def _():
        y = _rms(x_ref[...]) * ng_ref[...]
        xn_ref[...] = (y * (1.0 + sc_ref[...]) + sh_ref[...]).astype(BF16)

    xn = xn_ref[...]
    g = _dot(xn, wg_ref[...])
    u = _dot(xn, wu_ref[...])
    h = (jax.nn.silu(g) * u).astype(BF16)
    dn = _dot(h, wd_ref[...])

    @pl.when(j == 0)
    def _():
        o_ref[...] = dn

    @pl.when(j > 0)
    def _():
        o_ref[...] += dn

    @pl.when(j == pl.num_programs(1) - 1)
    def _():
        r = x_ref[...] + 0.5 * gt_ref[...] * o_ref[...]
        if final_norm:
            r = _rms(r) * fg_ref[...]
        o_ref[...] = r


def _half_ffn(x, mods, k0, bidx, norm_g, w_gu, w_down, layer, half, tm, final_g=None):
    rows, d = x.shape
    dff = w_down.shape[2]
    fc = 512
    nf = dff // fc
    row = pl.BlockSpec((tm, d), lambda i, j: (i, 0))
    vec = pl.BlockSpec((1, d), lambda i, j: (0, 0))
    in_specs = [row, vec, _mod_spec(k0, d, bidx), _mod_spec(k0 + 1, d, bidx), _mod_spec(k0 + 2, d, bidx),
                pl.BlockSpec((None, None, d, fc), lambda i, j: (layer, half, 0, j)),
                pl.BlockSpec((None, None, d, fc), lambda i, j: (layer, half, 0, nf + j)),
                pl.BlockSpec((None, None, fc, d), lambda i, j: (layer, half, j, 0))]
    args = [x, norm_g.reshape(1, d), mods, mods, mods, w_gu, w_gu, w_down]
    if final_g is not None:
        in_specs.append(vec)
        args.append(final_g.reshape(1, d))
    return pl.pallas_call(
        functools.partial(_ffn_kernel, final_norm=final_g is not None),
        grid=(rows // tm, nf),
        in_specs=in_specs,
        out_specs=row,
        out_shape=jax.ShapeDtypeStruct((rows, d), F32),
        scratch_shapes=[pltpu.VMEM((tm, d), BF16)],
        compiler_params=_cparams("parallel", "arbitrary"),
        name="half_ffn",
    )(*args)


def _prenorm_linear_kernel(x_ref, ng_ref, sh_ref, sc_ref, w_ref, *rest, ncol, act, has_extra):
    if has_extra:
        wx_ref, o_ref, ox_ref, xn_ref = rest
    else:
        o_ref, xn_ref = rest

    @pl.when(pl.program_id(1) == 0)
    def _():
        y = _rms(_load_rows(x_ref, ncol)) * ng_ref[...]
        xn = (y * (1.0 + sc_ref[...]) + sh_ref[...]).astype(BF16)
        xn_ref[...] = xn
        if has_extra:
            ox_ref[...] = _dot(xn, wx_ref[...])

    y = _dot(xn_ref[...], w_ref[...])
    if act == "gelu":
        y = 0.5 * y * (1.0 + lax.erf(y * (0.5 ** 0.5)))
    o_ref[...] = y.astype(o_ref.dtype)


def _prenorm_linear(x, mods, k0, bidx, norm_g, w, tm, seq, ncol, act=None, w_extra=None, tn=512):
    rows, d = x.shape
    n = w.shape[1]
    xv, xspec = _row_view(x, tm, seq, ncol)
    vec = pl.BlockSpec((1, d), lambda i, j: (0, 0))
    in_specs = [xspec, vec, _mod_spec(k0, d, bidx), _mod_spec(k0 + 1, d, bidx),
                pl.BlockSpec((d, tn), lambda i, j: (0, j))]
    args = [xv, norm_g.reshape(1, d), mods, mods, w]
    out_specs = [pl.BlockSpec((tm, tn), lambda i, j: (i, j))]
    out_shape = [jax.ShapeDtypeStruct((rows, n), F32)]
    if w_extra is not None:
        nx = w_extra.shape[1]
        in_specs.append(pl.BlockSpec((d, nx), lambda i, j: (0, 0)))
        args.append(w_extra)
        out_specs.append(pl.BlockSpec((tm, nx), lambda i, j: (i, 0)))
        out_shape.append(jax.ShapeDtypeStruct((rows, nx), F32))
    outs = pl.pallas_call(
        functools.partial(_prenorm_linear_kernel, ncol=ncol, act=act, has_extra=w_extra is not None),
        grid=(rows // tm, n // tn),
        in_specs=in_specs,
        out_specs=out_specs,
        out_shape=out_shape,
        scratch_shapes=[pltpu.VMEM((tm, d), BF16)],
        compiler_params=_cparams("parallel", "arbitrary"),
        name="mixer_in_proj",
    )(*args)
    return outs if w_extra is not None else outs[0]


_CONV_PAD = V7X_SUBLANES


def _dn_conv_kernel(p_ref, cw_ref, o_ref, xp_ref, *, seq, sub, n_q, n_qk):
    c = pl.program_id(1)
    cb = p_ref.shape[1]
    half = DN_CONV_K // 2
    xp_ref[0:_CONV_PAD, :] = jnp.zeros((_CONV_PAD, cb), F32)
    xp_ref[_CONV_PAD:_CONV_PAD + seq, :] = p_ref[...]
    xp_ref[_CONV_PAD + seq:, :] = jnp.zeros((_CONV_PAD, cb), F32)
    w = cw_ref[...]

    def conv_tile(i):
        r0 = pl.multiple_of(i * sub, sub)
        win = xp_ref[pl.ds(r0, sub + 2 * _CONV_PAD), :]
        acc = None
        for t in range(DN_CONV_K):
            lo = _CONV_PAD - half + t
            term = win[lo:lo + sub, :] * w[t:t + 1, :]
            acc = term if acc is None else acc + term
        return r0, jax.nn.silu(acc)

    @pl.when(c < n_qk)
    def _():
        scale = jnp.where(c < n_q, DN_HEAD_DIM ** -0.5, 1.0).astype(F32)

        def body(i, carry):
            r0, y = conv_tile(i)
            for hh in range(cb // DN_HEAD_DIM):
                seg = y[:, hh * DN_HEAD_DIM:(hh + 1) * DN_HEAD_DIM]
                seg = seg * lax.rsqrt(jnp.sum(seg * seg, axis=-1, keepdims=True) + EPS)
                o_ref[pl.ds(r0, sub), hh * DN_HEAD_DIM:(hh + 1) * DN_HEAD_DIM] = seg * scale
            return carry

        lax.fori_loop(0, seq // sub, body, 0)

    @pl.when(c >= n_qk)
    def _():
        def body(i, carry):
            r0, y = conv_tile(i)
            o_ref[pl.ds(r0, sub), :] = y
            return carry

        lax.fori_loop(0, seq // sub, body, 0)


def _dn_conv(p, conv_w, seq, qk_dim, conv_dim):
    rows = p.shape[0]
    cb = 256
    sub = min(seq, 256)
    return pl.pallas_call(
        functools.partial(_dn_conv_kernel, seq=seq, sub=sub, n_q=qk_dim // cb, n_qk=2 * qk_dim // cb),
        grid=(rows // seq, conv_dim // cb),
        in_specs=[pl.BlockSpec((seq, cb), lambda s, c: (s, c)),
                  pl.BlockSpec((DN_CONV_K, cb), lambda s, c: (0, c))],
        out_specs=pl.BlockSpec((seq, cb), lambda s, c: (s, c)),
        out_shape=jax.ShapeDtypeStruct((rows, conv_dim), F32),
        scratch_shapes=[pltpu.VMEM((seq + 2 * _CONV_PAD, cb), F32)],
        compiler_params=_cparams("parallel", "parallel"),
        name="dn_conv",
    )(p, conv_w)


def _dn_gate_kernel(ab_ref, alog_ref, dtb_ref, pm_ref, gc_ref, gr_ref, *, hv):
    tm, width = ab_ref.shape
    tile = V7X_LANES
    x = ab_ref[...]
    lane_full = lax.broadcasted_iota(jnp.int32, (tm, width), 1)
    lane = lax.broadcasted_iota(jnp.int32, (tile, width), 1)
    is_a = (lane % (2 * hv)) < hv
    is_rev = lane >= 2 * hv
    g = -jnp.exp(alog_ref[...]) * jax.nn.softplus(x + dtb_ref[...])
    raw = jnp.where((lane_full % (2 * hv)) < hv, g, jax.nn.sigmoid(x))
    ri = lax.broadcasted_iota(jnp.int32, (tile, tile), 0)
    ci = lax.broadcasted_iota(jnp.int32, (tile, tile), 1)
    same = (ri // DN_CHUNK) == (ci // DN_CHUNK)
    low = jnp.where(same & (ci <= ri), 1.0, 0.0).astype(F32)
    upp = jnp.where(same & (ci >= ri), 1.0, 0.0).astype(F32)
    for t in range(tm // tile):
        blk = raw[t * tile:(t + 1) * tile, :]
        pre = jnp.dot(low, blk, precision=HIGHEST, preferred_element_type=F32)
        suf = jnp.dot(upp, blk, precision=HIGHEST, preferred_element_type=F32)
        out = jnp.where(is_a, jnp.where(is_rev, suf, pre), blk)
        gc_ref[t * tile:(t + 1) * tile, :] = out
        gr_ref[t] = lax.dot_general(pm_ref[...], out, (((1,), (1,)), ((), ())),
                                    precision=HIGHEST, preferred_element_type=F32)


def _gate_perm(hv):
    pm = np.zeros((4 * hv, 4 * hv), np.float32)
    for hq in range(hv // 2):
        for d in range(2):
            for isb in range(2):
                for j in range(2):
                    pm[hq * 8 + d * 4 + isb * 2 + j, d * 2 * hv + isb * hv + 2 * hq + j] = 1.0
    return jnp.asarray(pm)


def _dn_gates(ab, a_log, dt_bias, hv, tm):
    rows, width = ab.shape
    assert width == 4 * hv == V7X_LANES
    zeros = jnp.zeros_like(a_log)
    alog = jnp.concatenate([a_log, zeros], axis=1).reshape(1, width)
    dtb = jnp.concatenate([dt_bias, zeros], axis=1).reshape(1, width)
    vec = pl.BlockSpec((1, width), lambda i: (0, 0))
    return pl.pallas_call(
        functools.partial(_dn_gate_kernel, hv=hv),
        grid=(rows // tm,),
        in_specs=[pl.BlockSpec((tm, width), lambda i: (i, 0)), vec, vec,
                  pl.BlockSpec((width, width), lambda i: (0, 0))],
        out_specs=[pl.BlockSpec((tm, width), lambda i: (i, 0)),
                   pl.BlockSpec((tm // V7X_LANES, width, V7X_LANES), lambda i: (i, 0, 0))],
        out_shape=[jax.ShapeDtypeStruct((rows, width), F32),
                   jax.ShapeDtypeStruct((rows // V7X_LANES, width, V7X_LANES), F32)],
        compiler_params=_cparams("parallel"),
        name="dn_gates",
    )(ab, alog, dtb, _gate_perm(hv))


_INV_BLOCK = 16


def _mm_bf16(ps, qs):
    return [_dot(p.astype(BF16), q.astype(BF16)) for p, q in zip(ps, qs)]


def _unit_tri_inverse(mats, ri, ci):
    eye = (ri == ci).astype(F32)
    bi, bj = ri // _INV_BLOCK, ci // _INV_BLOCK
    ds = [jnp.where(bi == bj, a, 0.0) for a in mats]
    ts = [eye - d for d in ds]
    power = 1
    while 2 * power < _INV_BLOCK:
        ds = _mm_bf16(ds, ds)
        power *= 2
        ts = _mm_bf16(ts, [eye + d for d in ds])
    w = 1
    while w * _INV_BLOCK < mats[0].shape[0]:
        off = (bi // (2 * w) == bj // (2 * w)) & (bi // w != bj // w)
        tes = _mm_bf16(ts, [jnp.where(off, a, 0.0) for a in mats])
        ts = [t - tet for t, tet in zip(ts, _mm_bf16(tes, ts))]
        w *= 2
    return ts


def _dn_core_kernel(qc_ref, kc_ref, vc_ref, gcc_ref, grc_ref, ql_ref, kl_ref, vl_ref, gcl_ref, grl_ref,
                    oc_ref, ol_ref, s_ref, *, hv):
    hq = pl.program_id(1)
    dk = DN_HEAD_DIM
    cs = DN_CHUNK
    tile = V7X_LANES
    s_ref[...] = jnp.zeros(s_ref.shape, F32)
    oc_ref[...] = jnp.zeros(oc_ref.shape, F32)
    ol_ref[...] = jnp.zeros(ol_ref.shape, F32)
    ri = lax.broadcasted_iota(jnp.int32, (cs, cs), 0)
    ci = lax.broadcasted_iota(jnp.int32, (cs, cs), 1)
    shift = (tile - 2 * hq) % tile

    n_sub = tile // cs
    incl = (ri >= ci, ri <= ci)
    strict = (ri > ci, ri < ci)
    last = (cs - 1, 0)

    def tile_step(refs, t_fwd, t_bwd):
        q_ref, k_ref, v_ref, gc_ref, gr_ref, o_ref = refs
        chunks = [(0, t_fwd, c) for c in range(n_sub)] + [(1, t_bwd, c) for c in reversed(range(n_sub))]
        r0s = [pl.multiple_of(t * tile, tile) + c * cs for _, t, c in chunks]
        ks = [k_ref[pl.ds(r0, cs), :] for r0 in r0s]
        qs = [q_ref[pl.ds(r0, cs), :] for r0 in r0s]
        kbs = [k.astype(BF16) for k in ks]
        kks = [_dot_nt(kb, kb) for kb in kbs]
        qks = [_dot_nt(q.astype(BF16), kb) for q, kb in zip(qs, kbs)]
        gcts = [pltpu.roll(gc_ref[pl.ds(r0, cs), :], shift, 1) for r0 in r0s]
        grts = [gr_ref[t] for _, t, _ in chunks]

        scs = [(ic, j) for ic in range(len(chunks)) for j in range(2)]
        dirs = [chunks[ic][0] for ic, _ in scs]
        gcols = [gcts[ic][:, chunks[ic][0] * 2 * hv + j:chunks[ic][0] * 2 * hv + j + 1] for ic, j in scs]
        bcols = [gcts[ic][:, chunks[ic][0] * 2 * hv + hv + j:chunks[ic][0] * 2 * hv + hv + j + 1] for ic, j in scs]
        lanes = [slice(chunks[ic][2] * cs, (chunks[ic][2] + 1) * cs) for ic, _ in scs]
        grows = [grts[ic][chunks[ic][0] * 4 + j:chunks[ic][0] * 4 + j + 1, ln] for (ic, j), ln in zip(scs, lanes)]
        brows = [grts[ic][chunks[ic][0] * 4 + 2 + j:chunks[ic][0] * 4 + 3 + j, ln] for (ic, j), ln in zip(scs, lanes)]
        decs = [jnp.exp(jnp.where(incl[d], gc - gr, -jnp.inf)) for d, gc, gr in zip(dirs, gcols, grows)]
        amats = [jnp.where(strict[d], kks[ic] * dec * bc, 0.0)
                 for d, (ic, _), dec, bc in zip(dirs, scs, decs, bcols)]
        tinvs = _unit_tri_inverse(amats, ri, ci)
        us = _mm_bf16([ti * br for ti, br in zip(tinvs, brows)],
                      [v_ref[pl.ds(r0s[ic], cs), j * dk:(j + 1) * dk] for ic, j in scs])
        ws = _mm_bf16([ti * (br * jnp.exp(gr)) for ti, br, gr in zip(tinvs, brows, grows)],
                      [kbs[ic] for ic, _ in scs])
        wus = [jnp.concatenate([w, u], axis=1).astype(BF16) for w, u in zip(ws, us)]
        qwus = [_dot((qks[ic] * dec).astype(BF16), wu) for (ic, _), dec, wu in zip(scs, decs, wus)]
        g_lasts = [gc[last[d]:last[d] + 1, :] for d, gc in zip(dirs, gcols)]
        q_effs = [(qs[ic] * jnp.exp(gc) - qwu[:, :dk]).astype(BF16) for (ic, _), gc, qwu in zip(scs, gcols, qwus)]
        kwus = [_dot_tn((ks[ic] * jnp.exp(gl - gc)).astype(BF16), wu)
                for (ic, _), gl, gc, wu in zip(scs, g_lasts, gcols, wus)]

        states = [s_ref[si] for si in range(4)]
        for step in range(n_sub):
            idx = [(d * n_sub + step) * 2 + j for d in range(2) for j in range(2)]
            lhs = [jnp.concatenate([kwus[i][:, :dk].astype(BF16), q_effs[i]], axis=0) for i in idx]
            rs = [_dot(l, st.astype(BF16)) for l, st in zip(lhs, states)]
            states = [st * jnp.exp(g_lasts[i]) + kwus[i][:, dk:] - r[:dk, :] for i, st, r in zip(idx, states, rs)]
            for i, r in zip(idx, rs):
                ic, j = scs[i]
                o_ref[pl.ds(r0s[ic], cs), j * dk:(j + 1) * dk] += r[dk:, :] + qwus[i][:, dk:]
        for si in range(4):
            s_ref[si] = states[si]

    def run(refs, n_tiles):
        def body(i, carry):
            tile_step(refs, i, n_tiles - 1 - i)
            return carry

        lax.fori_loop(0, n_tiles, body, 0)

    run((qc_ref, kc_ref, vc_ref, gcc_ref, grc_ref, oc_ref), qc_ref.shape[0] // tile)
    run((ql_ref, kl_ref, vl_ref, gcl_ref, grl_ref, ol_ref), ql_ref.shape[0] // tile)


def _dn_core(qkv_c, gc_c, gr_c, qkv_l, gc_l, gr_l, batch, hv):
    dk = DN_HEAD_DIM
    hq_n = hv // 2
    ctx_len = qkv_c.shape[0] // batch
    seq = qkv_l.shape[0] // batch
    width = gc_l.shape[1]

    def specs(t):
        return [pl.BlockSpec((t, dk), lambda b, h: (b, h)),
                pl.BlockSpec((t, dk), lambda b, h: (b, hq_n + h)),
                pl.BlockSpec((t, 2 * dk), lambda b, h: (b, hq_n + h)),
                pl.BlockSpec((t, width), lambda b, h: (b, 0)),
                pl.BlockSpec((t // V7X_LANES, 8, V7X_LANES), lambda b, h: (b, h, 0))]

    return pl.pallas_call(
        functools.partial(_dn_core_kernel, hv=hv),
        grid=(batch, hq_n),
        in_specs=specs(ctx_len) + specs(seq),
        out_specs=[pl.BlockSpec((ctx_len, 2 * dk), lambda b, h: (b, h)),
                   pl.BlockSpec((seq, 2 * dk), lambda b, h: (b, h))],
        out_shape=[jax.ShapeDtypeStruct((batch * ctx_len, hv * dk), F32),
                   jax.ShapeDtypeStruct((batch * seq, hv * dk), F32)],
        scratch_shapes=[pltpu.VMEM((4, dk, dk), F32)],
        compiler_params=_cparams("parallel", "parallel"),
        name="dn_core",
    )(qkv_c, qkv_c, qkv_c, gc_c, gr_c, qkv_l, qkv_l, qkv_l, gc_l, gr_l)


def _dn_out_kernel(o_ref, z_ref, ng_ref, w_ref, x_ref, gt_ref, out_ref, acc_ref, *, ncol):
    k = pl.program_id(1)
    o = o_ref[...]
    dk = DN_HEAD_DIM
    segs = [_rms(o[:, h * dk:(h + 1) * dk]) * ng_ref[...] for h in range(o.shape[1] // dk)]
    on = jnp.concatenate(segs, axis=1) * jax.nn.silu(z_ref[...])
    part = _dot(on.astype(BF16), w_ref[...])

    @pl.when(k == 0)
    def _():
        acc_ref[...] = part

    @pl.when(k > 0)
    def _():
        acc_ref[...] += part

    @pl.when(k == pl.num_programs(1) - 1)
    def _():
        _store_rows(out_ref, _load_rows(x_ref, ncol) + gt_ref[...] * acc_ref[...], ncol)


def _dn_out(o, p, z_col0, norm_g, w_out, x, mods, bidx, tm, seq, ncol):
    rows, d = x.shape
    kdim = w_out.shape[0]
    tk = 512
    xv, xspec = _row_view(x, tm, seq, ncol)
    zb = z_col0 // tk
    out = pl.pallas_call(
        functools.partial(_dn_out_kernel, ncol=ncol),
        grid=(rows // tm, kdim // tk),
        in_specs=[pl.BlockSpec((tm, tk), lambda i, k: (i, k)),
                  pl.BlockSpec((tm, tk), lambda i, k: (i, zb + k)),
                  pl.BlockSpec((1, DN_HEAD_DIM), lambda i, k: (0, 0)),
                  pl.BlockSpec((tk, d), lambda i, k: (k, 0)),
                  xspec, _mod_spec(5, d, bidx)],
        out_specs=xspec,
        out_shape=jax.ShapeDtypeStruct(xv.shape, F32),
        scratch_shapes=[pltpu.VMEM((tm, d), F32)],
        compiler_params=_cparams("parallel", "arbitrary"),
        name="dn_out",
    )(o, p, norm_g.reshape(1, DN_HEAD_DIM), w_out, xv, mods)
    return out.reshape(rows, d)


def _sg_out_kernel(u_ref, v_ref, lg_ref, lb_ref, ws_ref, bs_ref, w_ref, x_ref, gt_ref, out_ref, m_ref, *, ncol):
    v = v_ref[...]
    mu = jnp.mean(v, axis=-1, keepdims=True)
    var = jnp.mean(jnp.square(v - mu), axis=-1, keepdims=True)
    m_ref[...] = ((v - mu) * lax.rsqrt(var + EPS) * lg_ref[...] + lb_ref[...]).astype(BF16)
    tm, sg_dim = v.shape
    gd = sg_dim // SG_GROUPS
    for g in range(SG_GROUPS):
        wsg = ws_ref[g]
        bias = bs_ref[:, g:g + 1]
        for c in range(tm // SG_CHUNK):
            rs = slice(c * SG_CHUNK, (c + 1) * SG_CHUNK)
            cs = slice(g * gd, (g + 1) * gd)
            mixed = _dot(wsg, m_ref[rs, cs]) + bias
            m_ref[rs, cs] = (u_ref[rs, cs] * mixed).astype(BF16)
    y = _dot(m_ref[...], w_ref[...])
    _store_rows(out_ref, _load_rows(x_ref, ncol) + gt_ref[...] * y, ncol)


def _sg_out(uv, ln_g, ln_b, w_s, b_s, w_out, x, mods, bidx, tm, seq, ncol):
    rows, d = x.shape
    sg_dim = w_out.shape[0]
    xv, xspec = _row_view(x, tm, seq, ncol)
    vec = pl.BlockSpec((1, sg_dim), lambda i: (0, 0))
    out = pl.pallas_call(
        functools.partial(_sg_out_kernel, ncol=ncol),
        grid=(rows // tm,),
        in_specs=[pl.BlockSpec((tm, sg_dim), lambda i: (i, 0)),
                  pl.BlockSpec((tm, sg_dim), lambda i: (i, 1)),
                  vec, vec,
                  pl.BlockSpec((SG_GROUPS, SG_CHUNK, SG_CHUNK), lambda i: (0, 0, 0)),
                  pl.BlockSpec((SG_CHUNK, SG_GROUPS), lambda i: (0, 0)),
                  pl.BlockSpec((sg_dim, d), lambda i: (0, 0), pipeline_mode=pl.Buffered(1)),
                  xspec, _mod_spec(5, d, bidx)],
        out_specs=xspec,
        out_shape=jax.ShapeDtypeStruct(xv.shape, F32),
        scratch_shapes=[pltpu.VMEM((tm, sg_dim), BF16)],
        compiler_params=_cparams("parallel"),
        name="sg_out",
    )(uv, uv, ln_g.reshape(1, sg_dim), ln_b.reshape(1, sg_dim), w_s, b_s.T, w_out, xv, mods)
    return out.reshape(rows, d)


def kernel(x, c, ctx, c_ctx, mod_w, mod_b, norm_g, ffn_w_gu, ffn_w_down, dn_w_in, dn_conv_w, dn_a_log, dn_dt_bias,
           dn_norm_g, dn_w_out, sg_w_in, sg_ln_g, sg_ln_b, sg_w_s, sg_b_s, sg_w_out, final_norm_g):
    batch, seq, d = x.shape
    ctx_len = ctx.shape[1]
    depth = mod_w.shape[0]
    nm = mod_w.shape[2]
    hv = dn_a_log.shape[2]
    dk = DN_HEAD_DIM
    qk_dim = hv // 2 * dk
    v_dim = hv * dk
    conv_dim = 2 * qk_dim + v_dim
    assert batch + 1 <= MOD_ROWS

    xl = x.reshape(batch * seq, d)
    xc = ctx.reshape(batch * ctx_len, d)
    tm_l = min(512, seq)
    tm_c = min(512, batch * ctx_len)
    tiles_per_seq = seq // tm_l
    bidx_l = lambda i: i // tiles_per_seq
    bidx_c = lambda i: batch
    ncol_cm = tm_l // (seq // GRID_W)

    cs = jnp.concatenate([c, c_ctx[None, :], jnp.zeros((MOD_ROWS - batch - 1, d), F32)], axis=0)
    mods_all = _modulation(cs, mod_w, mod_b).reshape(depth, MOD_ROWS, 1, nm)

    w_gu = ffn_w_gu.astype(BF16)
    w_down = ffn_w_down.astype(BF16)
    n_mix = 2

    for i in range(depth):
        last = i == depth - 1
        kind = i % n_mix
        j = i // n_mix
        ncol = ncol_cm if (j % 2) == 1 else 0
        need_ctx = not (last and kind == 1)
        mods = mods_all[i]

        xl = _half_ffn(xl, mods, 0, bidx_l, norm_g[i, 0], w_gu, w_down, i, 0, tm_l)
        if need_ctx:
            xc = _half_ffn(xc, mods, 0, bidx_c, norm_g[i, 0], w_gu, w_down, i, 0, tm_c)

        if kind == 0:
            w_in = dn_w_in[j].astype(BF16)
            w_main, w_ab = w_in[:, :conv_dim + v_dim], w_in[:, conv_dim + v_dim:]
            w_out = dn_w_out[j].astype(BF16)
            pl_, ab_l = _prenorm_linear(xl, mods, 3, bidx_l, norm_g[i, 1], w_main, tm_l, seq, ncol, w_extra=w_ab)
            pc_, ab_c = _prenorm_linear(xc, mods, 3, bidx_c, norm_g[i, 1], w_main, tm_c, ctx_len, 0, w_extra=w_ab)
            qkv_l = _dn_conv(pl_, dn_conv_w[j], seq, qk_dim, conv_dim)
            qkv_c = _dn_conv(pc_, dn_conv_w[j], ctx_len, qk_dim, conv_dim)
            gc_l, gr_l = _dn_gates(ab_l, dn_a_log[j], dn_dt_bias[j], hv, tm_l)
            gc_c, gr_c = _dn_gates(ab_c, dn_a_log[j], dn_dt_bias[j], hv, tm_c)
            o_c, o_l = _dn_core(qkv_c, gc_c, gr_c, qkv_l, gc_l, gr_l, batch, hv)
            xl = _dn_out(o_l, pl_, conv_dim, dn_norm_g[j], w_out, xl, mods, bidx_l, tm_l, seq, ncol)
            if not last:
                xc = _dn_out(o_c, pc_, conv_dim, dn_norm_g[j], w_out, xc, mods, bidx_c, tm_c, ctx_len, 0)
        else:
            w_in = sg_w_in[j].astype(BF16)
            w_out = sg_w_out[j].astype(BF16)
            w_s = sg_w_s[j].astype(BF16)
            tm_sl = min(256, seq)
            tm_sc = min(256, batch * ctx_len)
            ncol_s = (tm_sl // (seq // GRID_W)) if ncol else 0
            bidx_sl = lambda i, n=seq // tm_sl: i // n
            uv_l = _prenorm_linear(xl, mods, 3, bidx_l, norm_g[i, 1], w_in, tm_l, seq, ncol, act="gelu")
            xl = _sg_out(uv_l, sg_ln_g[j], sg_ln_b[j], w_s, sg_b_s[j], w_out, xl, mods, bidx_sl, tm_sl, seq, ncol_s)
            if not last:
                uv_c = _prenorm_linear(xc, mods, 3, bidx_c, norm_g[i, 1], w_in, tm_c, ctx_len, 0, act="gelu")
                xc = _sg_out(uv_c, sg_ln_g[j], sg_ln_b[j], w_s, sg_b_s[j], w_out, xc, mods, bidx_c, tm_sc, ctx_len, 0)

        xl = _half_ffn(xl, mods, 6, bidx_l, norm_g[i, 2], w_gu, w_down, i, 1, tm_l,
                       final_g=final_norm_g if last else None)
        if not last:
            xc = _half_ffn(xc, mods, 6, bidx_c, norm_g[i, 2], w_gu, w_down, i, 1, tm_c)

    return xl.reshape(batch, seq, d)
```

```python
import functools

import numpy as np
import jax
import jax.numpy as jnp
from jax import lax
from jax.experimental import pallas as pl
from jax.experimental.pallas import tpu as pltpu

F32 = jnp.float32
BF16 = jnp.bfloat16
EPS = 1e-6
GRID_W = 64
N_MOD = 9
DN_HEAD_DIM = 128
DN_CHUNK = 64
DN_CONV_K = 5
SG_CHUNK = 128
SG_GROUPS = 16

V7X_LANES = 128
V7X_SUBLANES = 8
V7X_VMEM_BYTES = 64 * 1024 * 1024
VMEM_LIMIT = V7X_VMEM_BYTES - 8 * 1024 * 1024
MOD_ROWS = 16
HIGHEST = lax.Precision.HIGHEST


def _cparams(*sem):
    return pltpu.CompilerParams(dimension_semantics=sem, vmem_limit_bytes=VMEM_LIMIT)


def _dot(a, b):
    return jnp.dot(a, b, preferred_element_type=F32)


def _dot_nt(a, b):
    return lax.dot_general(a, b, (((1,), (1,)), ((), ())), preferred_element_type=F32)


def _dot_tn(a, b):
    return lax.dot_general(a, b, (((0,), (0,)), ((), ())), preferred_element_type=F32)


def _rms(x):
    return x * lax.rsqrt(jnp.mean(x * x, axis=-1, keepdims=True) + EPS)


def _load_rows(ref, ncol):
    if not ncol:
        return ref[...]
    width = ref.shape[1] // ncol
    return jnp.concatenate([ref[:, k * width:(k + 1) * width] for k in range(ncol)], axis=0)


def _store_rows(ref, val, ncol):
    if not ncol:
        ref[...] = val
        return
    width = ref.shape[1] // ncol
    rows = ref.shape[0]
    for k in range(ncol):
        ref[:, k * width:(k + 1) * width] = val[k * rows:(k + 1) * rows, :]


def _row_view(arr, tm, seq, ncol):
    width = arr.shape[1]
    if not ncol:
        return arr, pl.BlockSpec((tm, width), lambda i, *_: (i, 0))
    grid_rows = seq // GRID_W
    per_batch = GRID_W // ncol
    view = arr.reshape(arr.shape[0] // seq, grid_rows, GRID_W * width)
    return view, pl.BlockSpec((None, grid_rows, ncol * width), lambda i, *_: (i // per_batch, 0, i % per_batch))


def _mod_spec(k, d, bidx):
    return pl.BlockSpec((None, 1, d), lambda i, *_: (bidx(i), 0, k))


def _mod_kernel(c_ref, w_ref, b_ref, o_ref):
    s = jax.nn.silu(c_ref[...]).astype(BF16)
    o_ref[...] = _dot(s, w_ref[...].astype(BF16)) + b_ref[...]


def _modulation(cs, mod_w, mod_b):
    depth, d, nm = mod_w.shape
    tn = 1024
    return pl.pallas_call(
        _mod_kernel,
        grid=(depth, nm // tn),
        in_specs=[pl.BlockSpec((MOD_ROWS, d), lambda l, j: (0, 0)),
                  pl.BlockSpec((None, d, tn), lambda l, j: (l, 0, j)),
                  pl.BlockSpec((None, 1, tn), lambda l, j: (l, 0, j))],
        out_specs=pl.BlockSpec((None, MOD_ROWS, tn), lambda l, j: (l, 0, j)),
        out_shape=jax.ShapeDtypeStruct((depth, MOD_ROWS, nm), F32),
        compiler_params=_cparams("parallel", "parallel"),
        name="modulation",
    )(cs, mod_w, mod_b.reshape(depth, 1, nm))


def _ffn_kernel(x_ref, ng_ref, sh_ref, sc_ref, gt_ref, wg_ref, wu_ref, wd_ref, *rest, final_norm):
    if final_norm:
        fg_ref, o_ref, xn_ref = rest
    else:
        o_ref, xn_ref = rest
    j = pl.program_id(1)

    @pl.when(j == 0)
    def _():
        y = _rms(x_ref[...]) * ng_ref[...]
        xn_ref[...] = (y * (1.0 + sc_ref[...]) + sh_ref[...]).astype(BF16)
        o_ref[...] = jnp.zeros(o_ref.shape, F32)

    xn = xn_ref[...]
    g = _dot(xn, wg_ref[...])
    u = _dot(xn, wu_ref[...])
    h = (jax.nn.silu(g) * u).astype(BF16)
    o_ref[...] += _dot(h, wd_ref[...])

    @pl.when(j == pl.num_programs(1) - 1)
    def _():
        r = x_ref[...] + 0.5 * gt_ref[...] * o_ref[...]
        if final_norm:
            r = _rms(r) * fg_ref[...]
        o_ref[...] = r


def _half_ffn(x, mods, k0, bidx, norm_g, w_gu, w_down, layer, half, tm, final_g=None):
    rows, d = x.shape
    dff = w_down.shape[2]
    fc = 512
    nf = dff // fc
    row = pl.BlockSpec((tm, d), lambda i, j: (i, 0))
    vec = pl.BlockSpec((1, d), lambda i, j: (0, 0))
    in_specs = [row, vec, _mod_spec(k0, d, bidx), _mod_spec(k0 + 1, d, bidx), _mod_spec(k0 + 2, d, bidx),
                pl.BlockSpec((None, None, d, fc), lambda i, j: (layer, half, 0, j)),
                pl.BlockSpec((None, None, d, fc), lambda i, j: (layer, half, 0, nf + j)),
                pl.BlockSpec((None, None, fc, d), lambda i, j: (layer, half, j, 0))]
    args = [x, norm_g.reshape(1, d), mods, mods, mods, w_gu, w_gu, w_down]
    if final_g is not None:
        in_specs.append(vec)
        args.append(final_g.reshape(1, d))
    return pl.pallas_call(
        functools.partial(_ffn_kernel, final_norm=final_g is not None),
        grid=(rows // tm, nf),
        in_specs=in_specs,
        out_specs=row,
        out_shape=jax.ShapeDtypeStruct((rows, d), F32),
        scratch_shapes=[pltpu.VMEM((tm, d), BF16)],
        compiler_params=_cparams("parallel", "arbitrary"),
        name="half_ffn",
    )(*args)


def _prenorm_linear_kernel(x_ref, ng_ref, sh_ref, sc_ref, w_ref, *rest, ncol, act, has_extra):
    if has_extra:
        wx_ref, o_ref, ox_ref, xn_ref = rest
    else:
        o_ref, xn_ref = rest

    @pl.when(pl.program_id(1) == 0)
    def _():
        y = _rms(_load_rows(x_ref, ncol)) * ng_ref[...]
        xn = (y * (1.0 + sc_ref[...]) + sh_ref[...]).astype(BF16)
        xn_ref[...] = xn
        if has_extra:
            ox_ref[...] = _dot(xn, wx_ref[...])

    y = _dot(xn_ref[...], w_ref[...])
    if act == "gelu":
        y = 0.5 * y * (1.0 + lax.erf(y * (0.5 ** 0.5)))
    o_ref[...] = y.astype(o_ref.dtype)


def _prenorm_linear(x, mods, k0, bidx, norm_g, w, tm, seq, ncol, act=None, w_extra=None, tn=2048):
    rows, d = x.shape
    n = w.shape[1]
    assert n % tn == 0
    xv, xspec = _row_view(x, tm, seq, ncol)
    vec = pl.BlockSpec((1, d), lambda i, j: (0, 0))
    in_specs = [xspec, vec, _mod_spec(k0, d, bidx), _mod_spec(k0 + 1, d, bidx),
                pl.BlockSpec((d, tn), lambda i, j: (0, j))]
    args = [xv, norm_g.reshape(1, d), mods, mods, w]
    out_specs = [pl.BlockSpec((tm, tn), lambda i, j: (i, j))]
    out_shape = [jax.ShapeDtypeStruct((rows, n), F32)]
    if w_extra is not None:
        nx = w_extra.shape[1]
        in_specs.append(pl.BlockSpec((d, nx), lambda i, j: (0, 0)))
        args.append(w_extra)
        out_specs.append(pl.BlockSpec((tm, nx), lambda i, j: (i, 0)))
        out_shape.append(jax.ShapeDtypeStruct((rows, nx), F32))
    outs = pl.pallas_call(
        functools.partial(_prenorm_linear_kernel, ncol=ncol, act=act, has_extra=w_extra is not None),
        grid=(rows // tm, n // tn),
        in_specs=in_specs,
        out_specs=out_specs,
        out_shape=out_shape,
        scratch_shapes=[pltpu.VMEM((tm, d), BF16)],
        compiler_params=_cparams("parallel", "arbitrary"),
        name="mixer_in_proj",
    )(*args)
    return outs if w_extra is not None else outs[0]


_CONV_PAD = V7X_SUBLANES


def _dn_conv_kernel(p_ref, cw_ref, o_ref, xp_ref, *, seq, sub, n_q, n_qk):
    c = pl.program_id(1)
    cb = p_ref.shape[1]
    half = DN_CONV_K // 2
    xp_ref[0:_CONV_PAD, :] = jnp.zeros((_CONV_PAD, cb), F32)
    xp_ref[_CONV_PAD:_CONV_PAD + seq, :] = p_ref[...]
    xp_ref[_CONV_PAD + seq:, :] = jnp.zeros((_CONV_PAD, cb), F32)
    w = cw_ref[...]

    def conv_tile(i):
        r0 = pl.multiple_of(i * sub, sub)
        win = xp_ref[pl.ds(r0, sub + 2 * _CONV_PAD), :]
        acc = None
        for t in range(DN_CONV_K):
            lo = _CONV_PAD - half + t
            term = win[lo:lo + sub, :] * w[t:t + 1, :]
            acc = term if acc is None else acc + term
        return r0, jax.nn.silu(acc)

    @pl.when(c < n_qk)
    def _():
        scale = jnp.where(c < n_q, DN_HEAD_DIM ** -0.5, 1.0).astype(F32)

        def body(i, carry):
            r0, y = conv_tile(i)
            for hh in range(cb // DN_HEAD_DIM):
                seg = y[:, hh * DN_HEAD_DIM:(hh + 1) * DN_HEAD_DIM]
                seg = seg * lax.rsqrt(jnp.sum(seg * seg, axis=-1, keepdims=True) + EPS)
                o_ref[pl.ds(r0, sub), hh * DN_HEAD_DIM:(hh + 1) * DN_HEAD_DIM] = seg * scale
            return carry

        lax.fori_loop(0, seq // sub, body, 0)

    @pl.when(c >= n_qk)
    def _():
        def body(i, carry):
            r0, y = conv_tile(i)
            o_ref[pl.ds(r0, sub), :] = y
            return carry

        lax.fori_loop(0, seq // sub, body, 0)


def _dn_conv(p, conv_w, seq, qk_dim, conv_dim):
    rows = p.shape[0]
    cb = 256 if seq > 1024 else 1024
    sub = min(seq, 256)
    return pl.pallas_call(
        functools.partial(_dn_conv_kernel, seq=seq, sub=sub, n_q=qk_dim // cb, n_qk=2 * qk_dim // cb),
        grid=(rows // seq, conv_dim // cb),
        in_specs=[pl.BlockSpec((seq, cb), lambda s, c: (s, c)),
                  pl.BlockSpec((DN_CONV_K, cb), lambda s, c: (0, c))],
        out_specs=pl.BlockSpec((seq, cb), lambda s, c: (s, c)),
        out_shape=jax.ShapeDtypeStruct((rows, conv_dim), F32),
        scratch_shapes=[pltpu.VMEM((seq + 2 * _CONV_PAD, cb), F32)],
        compiler_params=_cparams("parallel", "parallel"),
        name="dn_conv",
    )(p, conv_w)


def _dn_gate_kernel(ab_ref, alog_ref, dtb_ref, pm_ref, gc_ref, gr_ref, *, hv):
    tm, width = ab_ref.shape
    tile = V7X_LANES
    x = ab_ref[...]
    lane_full = lax.broadcasted_iota(jnp.int32, (tm, width), 1)
    lane = lax.broadcasted_iota(jnp.int32, (tile, width), 1)
    is_a = (lane % (2 * hv)) < hv
    is_rev = lane >= 2 * hv
    g = -jnp.exp(alog_ref[...]) * jax.nn.softplus(x + dtb_ref[...])
    raw = jnp.where((lane_full % (2 * hv)) < hv, g, jax.nn.sigmoid(x))
    ri = lax.broadcasted_iota(jnp.int32, (tile, tile), 0)
    ci = lax.broadcasted_iota(jnp.int32, (tile, tile), 1)
    same = (ri // DN_CHUNK) == (ci // DN_CHUNK)
    low = jnp.where(same & (ci <= ri), 1.0, 0.0).astype(F32)
    upp = jnp.where(same & (ci >= ri), 1.0, 0.0).astype(F32)
    for t in range(tm // tile):
        blk = raw[t * tile:(t + 1) * tile, :]
        pre = jnp.dot(low, blk, precision=HIGHEST, preferred_element_type=F32)
        suf = jnp.dot(upp, blk, precision=HIGHEST, preferred_element_type=F32)
        out = jnp.where(is_a, jnp.where(is_rev, suf, pre), blk)
        gc_ref[t * tile:(t + 1) * tile, :] = out
        gr_ref[t] = lax.dot_general(pm_ref[...], out, (((1,), (1,)), ((), ())),
                                    precision=HIGHEST, preferred_element_type=F32)


def _gate_perm(hv):
    pm = np.zeros((4 * hv, 4 * hv), np.float32)
    for hq in range(hv // 2):
        for d in range(2):
            for isb in range(2):
                for j in range(2):
                    pm[hq * 8 + d * 4 + isb * 2 + j, d * 2 * hv + isb * hv + 2 * hq + j] = 1.0
    return jnp.asarray(pm)


def _dn_gates(ab, a_log, dt_bias, hv, tm):
    rows, width = ab.shape
    assert width == 4 * hv == V7X_LANES
    zeros = jnp.zeros_like(a_log)
    alog = jnp.concatenate([a_log, zeros], axis=1).reshape(1, width)
    dtb = jnp.concatenate([dt_bias, zeros], axis=1).reshape(1, width)
    vec = pl.BlockSpec((1, width), lambda i: (0, 0))
    return pl.pallas_call(
        functools.partial(_dn_gate_kernel, hv=hv),
        grid=(rows // tm,),
        in_specs=[pl.BlockSpec((tm, width), lambda i: (i, 0)), vec, vec,
                  pl.BlockSpec((width, width), lambda i: (0, 0))],
        out_specs=[pl.BlockSpec((tm, width), lambda i: (i, 0)),
                   pl.BlockSpec((tm // V7X_LANES, width, V7X_LANES), lambda i: (i, 0, 0))],
        out_shape=[jax.ShapeDtypeStruct((rows, width), F32),
                   jax.ShapeDtypeStruct((rows // V7X_LANES, width, V7X_LANES), F32)],
        compiler_params=_cparams("parallel"),
        name="dn_gates",
    )(ab, alog, dtb, _gate_perm(hv))


_INV_BLOCK = 16
_DN_TILES_PER_STEP = 4


def _mm_bf16(ps, qs):
    return [_dot(p.astype(BF16), q.astype(BF16)) for p, q in zip(ps, qs)]


def _unit_tri_inverse(mats, ri, ci):
    eye = (ri == ci).astype(F32)
    bi, bj = ri // _INV_BLOCK, ci // _INV_BLOCK
    ds = [jnp.where(bi == bj, a, 0.0) for a in mats]
    ts = [eye - d for d in ds]
    power = 1
    while 2 * power < _INV_BLOCK:
        ds = _mm_bf16(ds, ds)
        power *= 2
        ts = _mm_bf16(ts, [eye + d for d in ds])
    w = 1
    while w * _INV_BLOCK < mats[0].shape[0]:
        off = (bi // (2 * w) == bj // (2 * w)) & (bi // w != bj // w)
        tes = _mm_bf16(ts, [jnp.where(off, a, 0.0) for a in mats])
        ts = [t - tet for t, tet in zip(ts, _mm_bf16(tes, ts))]
        w *= 2
    return ts


def _dn_core_kernel(qc_ref, kc_ref, vc_ref, gcc_ref, grc_ref, ql_ref, kl_ref, vl_ref, gcl_ref, grl_ref,
                    oc_ref, ol_ref, s_ref, *, hv):
    hq = pl.program_id(1)
    dk = DN_HEAD_DIM
    cs = DN_CHUNK
    tile = V7X_LANES
    s_ref[...] = jnp.zeros(s_ref.shape, F32)
    oc_ref[...] = jnp.zeros(oc_ref.shape, F32)
    ol_ref[...] = jnp.zeros(ol_ref.shape, F32)
    ri = lax.broadcasted_iota(jnp.int32, (cs, cs), 0)
    ci = lax.broadcasted_iota(jnp.int32, (cs, cs), 1)
    shift = (tile - 2 * hq) % tile

    n_sub = tile // cs
    incl = (ri >= ci, ri <= ci)
    strict = (ri > ci, ri < ci)
    last = (cs - 1, 0)

    def tile_step(refs, t_fwd, t_bwd):
        q_ref, k_ref, v_ref, gc_ref, gr_ref, o_ref = refs
        chunks = ([(0, t, c) for t in t_fwd for c in range(n_sub)]
                  + [(1, t, c) for t in t_bwd for c in reversed(range(n_sub))])
        n_steps = len(chunks) // 2
        r0s = [pl.multiple_of(t * tile, tile) + c * cs for _, t, c in chunks]
        ks = [k_ref[pl.ds(r0, cs), :] for r0 in r0s]
        qs = [q_ref[pl.ds(r0, cs), :] for r0 in r0s]
        kbs = [k.astype(BF16) for k in ks]
        kks = [_dot_nt(kb, kb) for kb in kbs]
        qks = [_dot_nt(q.astype(BF16), kb) for q, kb in zip(qs, kbs)]
        gcts = [pltpu.roll(gc_ref[pl.ds(r0, cs), :], shift, 1) for r0 in r0s]
        grts = [gr_ref[t] for _, t, _ in chunks]

        scs = [(ic, j) for ic in range(len(chunks)) for j in range(2)]
        dirs = [chunks[ic][0] for ic, _ in scs]
        gcols = [gcts[ic][:, chunks[ic][0] * 2 * hv + j:chunks[ic][0] * 2 * hv + j + 1] for ic, j in scs]
        bcols = [gcts[ic][:, chunks[ic][0] * 2 * hv + hv + j:chunks[ic][0] * 2 * hv + hv + j + 1] for ic, j in scs]
        lanes = [slice(chunks[ic][2] * cs, (chunks[ic][2] + 1) * cs) for ic, _ in scs]
        grows = [grts[ic][chunks[ic][0] * 4 + j:chunks[ic][0] * 4 + j + 1, ln] for (ic, j), ln in zip(scs, lanes)]
        brows = [grts[ic][chunks[ic][0] * 4 + 2 + j:chunks[ic][0] * 4 + 3 + j, ln] for (ic, j), ln in zip(scs, lanes)]
        decs = [jnp.exp(jnp.where(incl[d], gc - gr, -jnp.inf)) for d, gc, gr in zip(dirs, gcols, grows)]
        amats = [jnp.where(strict[d], kks[ic] * dec * bc, 0.0)
                 for d, (ic, _), dec, bc in zip(dirs, scs, decs, bcols)]
        tinvs = _unit_tri_inverse(amats, ri, ci)
        us = _mm_bf16([ti * br for ti, br in zip(tinvs, brows)],
                      [v_ref[pl.ds(r0s[ic], cs), j * dk:(j + 1) * dk] for ic, j in scs])
        ws = _mm_bf16([ti * (br * jnp.exp(gr)) for ti, br, gr in zip(tinvs, brows, grows)],
                      [kbs[ic] for ic, _ in scs])
        wus = [jnp.concatenate([w, u], axis=1).astype(BF16) for w, u in zip(ws, us)]
        qwus = [_dot((qks[ic] * dec).astype(BF16), wu) for (ic, _), dec, wu in zip(scs, decs, wus)]
        g_lasts = [gc[last[d]:last[d] + 1, :] for d, gc in zip(dirs, gcols)]
        q_effs = [(qs[ic] * jnp.exp(gc) - qwu[:, :dk]).astype(BF16) for (ic, _), gc, qwu in zip(scs, gcols, qwus)]
        kwus = [_dot_tn((ks[ic] * jnp.exp(gl - gc)).astype(BF16), wu)
                for (ic, _), gl, gc, wu in zip(scs, g_lasts, gcols, wus)]

        states = [s_ref[si] for si in range(4)]
        for step in range(n_steps):
            idx = [(d * n_steps + step) * 2 + j for d in range(2) for j in range(2)]
            lhs = [jnp.concatenate([kwus[i][:, :dk].astype(BF16), q_effs[i]], axis=0) for i in idx]
            rs = [_dot(l, st.astype(BF16)) for l, st in zip(lhs, states)]
            states = [st * jnp.exp(g_lasts[i]) + kwus[i][:, dk:] - r[:dk, :] for i, st, r in zip(idx, states, rs)]
            for i, r in zip(idx, rs):
                ic, j = scs[i]
                o_ref[pl.ds(r0s[ic], cs), j * dk:(j + 1) * dk] += r[dk:, :] + qwus[i][:, dk:]
        for si in range(4):
            s_ref[si] = states[si]

    def run(refs, n_tiles):
        per = max(p for p in range(1, _DN_TILES_PER_STEP + 1) if n_tiles % p == 0)

        def body(i, carry):
            tile_step(refs, [i * per + m for m in range(per)], [n_tiles - 1 - i * per - m for m in range(per)])
            return carry

        lax.fori_loop(0, n_tiles // per, body, 0)

    run((qc_ref, kc_ref, vc_ref, gcc_ref, grc_ref, oc_ref), qc_ref.shape[0] // tile)
    run((ql_ref, kl_ref, vl_ref, gcl_ref, grl_ref, ol_ref), ql_ref.shape[0] // tile)


def _dn_core(qkv_c, gc_c, gr_c, qkv_l, gc_l, gr_l, batch, hv):
    dk = DN_HEAD_DIM
    hq_n = hv // 2
    ctx_len = qkv_c.shape[0] // batch
    seq = qkv_l.shape[0] // batch
    width = gc_l.shape[1]

    def specs(t):
        return [pl.BlockSpec((t, dk), lambda b, h: (b, h)),
                pl.BlockSpec((t, dk), lambda b, h: (b, hq_n + h)),
                pl.BlockSpec((t, 2 * dk), lambda b, h: (b, hq_n + h)),
                pl.BlockSpec((t, width), lambda b, h: (b, 0)),
                pl.BlockSpec((t // V7X_LANES, 8, V7X_LANES), lambda b, h: (b, h, 0))]

    return pl.pallas_call(
        functools.partial(_dn_core_kernel, hv=hv),
        grid=(batch, hq_n),
        in_specs=specs(ctx_len) + specs(seq),
        out_specs=[pl.BlockSpec((ctx_len, 2 * dk), lambda b, h: (b, h)),
                   pl.BlockSpec((seq, 2 * dk), lambda b, h: (b, h))],
        out_shape=[jax.ShapeDtypeStruct((batch * ctx_len, hv * dk), F32),
                   jax.ShapeDtypeStruct((batch * seq, hv * dk), F32)],
        scratch_shapes=[pltpu.VMEM((4, dk, dk), F32)],
        compiler_params=_cparams("parallel", "parallel"),
        name="dn_core",
    )(qkv_c, qkv_c, qkv_c, gc_c, gr_c, qkv_l, qkv_l, qkv_l, gc_l, gr_l)


def _dn_out_kernel(o_ref, z_ref, ng_ref, w_ref, x_ref, gt_ref, out_ref, acc_ref, *, ncol):
    k = pl.program_id(1)

    @pl.when(k == 0)
    def _():
        acc_ref[...] = jnp.zeros(acc_ref.shape, F32)

    o = o_ref[...]
    dk = DN_HEAD_DIM
    segs = [_rms(o[:, h * dk:(h + 1) * dk]) * ng_ref[...] for h in range(o.shape[1] // dk)]
    on = jnp.concatenate(segs, axis=1) * jax.nn.silu(z_ref[...])
    acc_ref[...] += _dot(on.astype(BF16), w_ref[...])

    @pl.when(k == pl.num_programs(1) - 1)
    def _():
        _store_rows(out_ref, _load_rows(x_ref, ncol) + gt_ref[...] * acc_ref[...], ncol)


def _dn_out(o, p, z_col0, norm_g, w_out, x, mods, bidx, tm, seq, ncol):
    rows, d = x.shape
    kdim = w_out.shape[0]
    tk = 1024
    xv, xspec = _row_view(x, tm, seq, ncol)
    zb = z_col0 // tk
    out = pl.pallas_call(
        functools.partial(_dn_out_kernel, ncol=ncol),
        grid=(rows // tm, kdim // tk),
        in_specs=[pl.BlockSpec((tm, tk), lambda i, k: (i, k)),
                  pl.BlockSpec((tm, tk), lambda i, k: (i, zb + k)),
                  pl.BlockSpec((1, DN_HEAD_DIM), lambda i, k: (0, 0)),
                  pl.BlockSpec((tk, d), lambda i, k: (k, 0)),
                  xspec, _mod_spec(5, d, bidx)],
        out_specs=xspec,
        out_shape=jax.ShapeDtypeStruct(xv.shape, F32),
        scratch_shapes=[pltpu.VMEM((tm, d), F32)],
        compiler_params=_cparams("parallel", "arbitrary"),
        name="dn_out",
    )(o, p, norm_g.reshape(1, DN_HEAD_DIM), w_out, xv, mods)
    return out.reshape(rows, d)


def _sg_out_kernel(u_ref, v_ref, lg_ref, lb_ref, ws_ref, bs_ref, w_ref, x_ref, gt_ref, out_ref, m_ref, *, ncol):
    v = v_ref[...]
    mu = jnp.mean(v, axis=-1, keepdims=True)
    var = jnp.mean(jnp.square(v - mu), axis=-1, keepdims=True)
    m_ref[...] = ((v - mu) * lax.rsqrt(var + EPS) * lg_ref[...] + lb_ref[...]).astype(BF16)
    tm, sg_dim = v.shape
    gd = sg_dim // SG_GROUPS
    for g in range(SG_GROUPS):
        wsg = ws_ref[g]
        bias = bs_ref[:, g:g + 1]
        for c in range(tm // SG_CHUNK):
            rs = slice(c * SG_CHUNK, (c + 1) * SG_CHUNK)
            cs = slice(g * gd, (g + 1) * gd)
            mixed = _dot(wsg, m_ref[rs, cs]) + bias
            m_ref[rs, cs] = (u_ref[rs, cs] * mixed).astype(BF16)
    y = _dot(m_ref[...], w_ref[...])
    _store_rows(out_ref, _load_rows(x_ref, ncol) + gt_ref[...] * y, ncol)


def _sg_out(uv, ln_g, ln_b, w_s, b_s, w_out, x, mods, bidx, tm, seq, ncol):
    rows, d = x.shape
    sg_dim = w_out.shape[0]
    xv, xspec = _row_view(x, tm, seq, ncol)
    vec = pl.BlockSpec((1, sg_dim), lambda i: (0, 0))
    out = pl.pallas_call(
        functools.partial(_sg_out_kernel, ncol=ncol),
        grid=(rows // tm,),
        in_specs=[pl.BlockSpec((tm, sg_dim), lambda i: (i, 0)),
                  pl.BlockSpec((tm, sg_dim), lambda i: (i, 1)),
                  vec, vec,
                  pl.BlockSpec((SG_GROUPS, SG_CHUNK, SG_CHUNK), lambda i: (0, 0, 0)),
                  pl.BlockSpec((SG_CHUNK, SG_GROUPS), lambda i: (0, 0)),
                  pl.BlockSpec((sg_dim, d), lambda i: (0, 0), pipeline_mode=pl.Buffered(1)),
                  xspec, _mod_spec(5, d, bidx)],
        out_specs=xspec,
        out_shape=jax.ShapeDtypeStruct(xv.shape, F32),
        scratch_shapes=[pltpu.VMEM((tm, sg_dim), BF16)],
        compiler_params=_cparams("parallel"),
        name="sg_out",
    )(uv, uv, ln_g.reshape(1, sg_dim), ln_b.reshape(1, sg_dim), w_s, b_s.T, w_out, xv, mods)
    return out.reshape(rows, d)


def kernel(x, c, ctx, c_ctx, mod_w, mod_b, norm_g, ffn_w_gu, ffn_w_down, dn_w_in, dn_conv_w, dn_a_log, dn_dt_bias,
           dn_norm_g, dn_w_out, sg_w_in, sg_ln_g, sg_ln_b, sg_w_s, sg_b_s, sg_w_out, final_norm_g):
    batch, seq, d = x.shape
    ctx_len = ctx.shape[1]
    depth = mod_w.shape[0]
    nm = mod_w.shape[2]
    hv = dn_a_log.shape[2]
    dk = DN_HEAD_DIM
    qk_dim = hv // 2 * dk
    v_dim = hv * dk
    conv_dim = 2 * qk_dim + v_dim
    assert batch + 1 <= MOD_ROWS

    xl = x.reshape(batch * seq, d)
    xc = ctx.reshape(batch * ctx_len, d)
    tm_l = min(512, seq)
    tm_c = min(512, batch * ctx_len)
    tiles_per_seq = seq // tm_l
    bidx_l = lambda i: i // tiles_per_seq
    bidx_c = lambda i: batch
    ncol_cm = tm_l // (seq // GRID_W)

    cs = jnp.concatenate([c, c_ctx[None, :], jnp.zeros((MOD_ROWS - batch - 1, d), F32)], axis=0)
    mods_all = _modulation(cs, mod_w, mod_b).reshape(depth, MOD_ROWS, 1, nm)

    w_gu = ffn_w_gu.astype(BF16)
    w_down = ffn_w_down.astype(BF16)
    n_mix = 2

    for i in range(depth):
        last = i == depth - 1
        kind = i % n_mix
        j = i // n_mix
        ncol = ncol_cm if (j % 2) == 1 else 0
        need_ctx = not (last and kind == 1)
        mods = mods_all[i]

        xl = _half_ffn(xl, mods, 0, bidx_l, norm_g[i, 0], w_gu, w_down, i, 0, tm_l)
        if need_ctx:
            xc = _half_ffn(xc, mods, 0, bidx_c, norm_g[i, 0], w_gu, w_down, i, 0, tm_c)

        if kind == 0:
            w_in = dn_w_in[j].astype(BF16)
            w_main, w_ab = w_in[:, :conv_dim + v_dim], w_in[:, conv_dim + v_dim:]
            w_out = dn_w_out[j].astype(BF16)
            pl_, ab_l = _prenorm_linear(xl, mods, 3, bidx_l, norm_g[i, 1], w_main, tm_l, seq, ncol, w_extra=w_ab)
            pc_, ab_c = _prenorm_linear(xc, mods, 3, bidx_c, norm_g[i, 1], w_main, tm_c, ctx_len, 0, w_extra=w_ab)
            qkv_l = _dn_conv(pl_, dn_conv_w[j], seq, qk_dim, conv_dim)
            qkv_c = _dn_conv(pc_, dn_conv_w[j], ctx_len, qk_dim, conv_dim)
            gc_l, gr_l = _dn_gates(ab_l, dn_a_log[j], dn_dt_bias[j], hv, tm_l)
            gc_c, gr_c = _dn_gates(ab_c, dn_a_log[j], dn_dt_bias[j], hv, tm_c)
            o_c, o_l = _dn_core(qkv_c, gc_c, gr_c, qkv_l, gc_l, gr_l, batch, hv)
            xl = _dn_out(o_l, pl_, conv_dim, dn_norm_g[j], w_out, xl, mods, bidx_l, tm_l, seq, ncol)
            if not last:
                xc = _dn_out(o_c, pc_, conv_dim, dn_norm_g[j], w_out, xc, mods, bidx_c, tm_c, ctx_len, 0)
        else:
            w_in = sg_w_in[j].astype(BF16)
            w_out = sg_w_out[j].astype(BF16)
            w_s = sg_w_s[j].astype(BF16)
            tm_sl = min(256, seq)
            tm_sc = min(256, batch * ctx_len)
            ncol_s = (tm_sl // (seq // GRID_W)) if ncol else 0
            bidx_sl = lambda i, n=seq // tm_sl: i // n
            uv_l = _prenorm_linear(xl, mods, 3, bidx_l, norm_g[i, 1], w_in, tm_l, seq, ncol, act="gelu")
            xl = _sg_out(uv_l, sg_ln_g[j], sg_ln_b[j], w_s, sg_b_s[j], w_out, xl, mods, bidx_sl, tm_sl, seq, ncol_s)
            if not last:
                uv_c = _prenorm_linear(xc, mods, 3, bidx_c, norm_g[i, 1], w_in, tm_c, ctx_len, 0, act="gelu")
                xc = _sg_out(uv_c, sg_ln_g[j], sg_ln_b[j], w_s, sg_b_s[j], w_out, xc, mods, bidx_c, tm_sc, ctx_len, 0)

        xl = _half_ffn(xl, mods, 6, bidx_l, norm_g[i, 2], w_gu, w_down, i, 1, tm_l,
                       final_g=final_norm_g if last else None)
        if not last:
            xc = _half_ffn(xc, mods, 6, bidx_c, norm_g[i, 2], w_gu, w_down, i, 1, tm_c)

    return xl.reshape(batch, seq, d)
```

```python
import functools

import numpy as np
import jax
import jax.numpy as jnp
from jax import lax
from jax.experimental import pallas as pl
from jax.experimental.pallas import tpu as pltpu

F32 = jnp.float32
BF16 = jnp.bfloat16
EPS = 1e-6
GRID_W = 64
N_MOD = 9
DN_HEAD_DIM = 128
DN_CHUNK = 64
DN_CONV_K = 5
SG_CHUNK = 128
SG_GROUPS = 16

V7X_LANES = 128
V7X_SUBLANES = 8
V7X_VMEM_BYTES = 64 * 1024 * 1024
VMEM_LIMIT = V7X_VMEM_BYTES - 8 * 1024 * 1024
MOD_ROWS = 16
HIGHEST = lax.Precision.HIGHEST


def _cparams(*sem):
    return pltpu.CompilerParams(dimension_semantics=sem, vmem_limit_bytes=VMEM_LIMIT)


def _dot(a, b):
    return jnp.dot(a, b, preferred_element_type=F32)


def _dot_nt(a, b):
    return lax.dot_general(a, b, (((1,), (1,)), ((), ())), preferred_element_type=F32)


def _dot_tn(a, b):
    return lax.dot_general(a, b, (((0,), (0,)), ((), ())), preferred_element_type=F32)


def _rms(x):
    return x * lax.rsqrt(jnp.mean(x * x, axis=-1, keepdims=True) + EPS)


def _load_rows(ref, ncol):
    if not ncol:
        return ref[...]
    width = ref.shape[1] // ncol
    return jnp.concatenate([ref[:, k * width:(k + 1) * width] for k in range(ncol)], axis=0)


def _store_rows(ref, val, ncol):
    if not ncol:
        ref[...] = val
        return
    width = ref.shape[1] // ncol
    rows = ref.shape[0]
    for k in range(ncol):
        ref[:, k * width:(k + 1) * width] = val[k * rows:(k + 1) * rows, :]


def _row_view(arr, tm, seq, ncol):
    width = arr.shape[1]
    if not ncol:
        return arr, pl.BlockSpec((tm, width), lambda i, *_: (i, 0))
    grid_rows = seq // GRID_W
    per_batch = GRID_W // ncol
    view = arr.reshape(arr.shape[0] // seq, grid_rows, GRID_W * width)
    return view, pl.BlockSpec((None, grid_rows, ncol * width), lambda i, *_: (i // per_batch, 0, i % per_batch))


def _mod_spec(k, d, bidx):
    return pl.BlockSpec((None, 1, d), lambda i, *_: (bidx(i), 0, k))


def _mod_kernel(c_ref, w_ref, b_ref, o_ref):
    s = jax.nn.silu(c_ref[...]).astype(BF16)
    o_ref[...] = _dot(s, w_ref[...].astype(BF16)) + b_ref[...]


def _modulation(cs, mod_w, mod_b):
    depth, d, nm = mod_w.shape
    tn = 1024
    return pl.pallas_call(
        _mod_kernel,
        grid=(depth, nm // tn),
        in_specs=[pl.BlockSpec((MOD_ROWS, d), lambda l, j: (0, 0)),
                  pl.BlockSpec((None, d, tn), lambda l, j: (l, 0, j)),
                  pl.BlockSpec((None, 1, tn), lambda l, j: (l, 0, j))],
        out_specs=pl.BlockSpec((None, MOD_ROWS, tn), lambda l, j: (l, 0, j)),
        out_shape=jax.ShapeDtypeStruct((depth, MOD_ROWS, nm), F32),
        compiler_params=_cparams("parallel", "parallel"),
        name="modulation",
    )(cs, mod_w, mod_b.reshape(depth, 1, nm))


def _ffn_kernel(x_ref, ng_ref, sh_ref, sc_ref, gt_ref, wg_ref, wu_ref, wd_ref, *rest, final_norm):
    if final_norm:
        fg_ref, o_ref, xn_ref = rest
    else:
        o_ref, xn_ref = rest
    j = pl.program_id(1)

    @pl.when(j == 0)
    def _():
        y = _rms(x_ref[...]) * ng_ref[...]
        xn_ref[...] = (y * (1.0 + sc_ref[...]) + sh_ref[...]).astype(BF16)
        o_ref[...] = jnp.zeros(o_ref.shape, F32)

    xn = xn_ref[...]
    g = _dot(xn, wg_ref[...])
    u = _dot(xn, wu_ref[...])
    h = (jax.nn.silu(g) * u).astype(BF16)
    o_ref[...] += _dot(h, wd_ref[...])

    @pl.when(j == pl.num_programs(1) - 1)
    def _():
        r = x_ref[...] + 0.5 * gt_ref[...] * o_ref[...]
        if final_norm:
            r = _rms(r) * fg_ref[...]
        o_ref[...] = r


def _half_ffn(x, mods, k0, bidx, norm_g, w_gu, w_down, layer, half, tm, final_g=None):
    rows, d = x.shape
    dff = w_down.shape[2]
    fc = 512
    nf = dff // fc
    row = pl.BlockSpec((tm, d), lambda i, j: (i, 0))
    vec = pl.BlockSpec((1, d), lambda i, j: (0, 0))
    in_specs = [row, vec, _mod_spec(k0, d, bidx), _mod_spec(k0 + 1, d, bidx), _mod_spec(k0 + 2, d, bidx),
                pl.BlockSpec((None, None, d, fc), lambda i, j: (layer, half, 0, j)),
                pl.BlockSpec((None, None, d, fc), lambda i, j: (layer, half, 0, nf + j)),
                pl.BlockSpec((None, None, fc, d), lambda i, j: (layer, half, j, 0))]
    args = [x, norm_g.reshape(1, d), mods, mods, mods, w_gu, w_gu, w_down]
    if final_g is not None:
        in_specs.append(vec)
        args.append(final_g.reshape(1, d))
    return pl.pallas_call(
        functools.partial(_ffn_kernel, final_norm=final_g is not None),
        grid=(rows // tm, nf),
        in_specs=in_specs,
        out_specs=row,
        out_shape=jax.ShapeDtypeStruct((rows, d), F32),
        scratch_shapes=[pltpu.VMEM((tm, d), BF16)],
        compiler_params=_cparams("parallel", "arbitrary"),
        name="half_ffn",
    )(*args)


def _prenorm_linear_kernel(x_ref, ng_ref, sh_ref, sc_ref, w_ref, *rest, ncol, act, has_extra):
    if has_extra:
        wx_ref, o_ref, ox_ref, xn_ref = rest
    else:
        o_ref, xn_ref = rest

    @pl.when(pl.program_id(1) == 0)
    def _():
        y = _rms(_load_rows(x_ref, ncol)) * ng_ref[...]
        xn = (y * (1.0 + sc_ref[...]) + sh_ref[...]).astype(BF16)
        xn_ref[...] = xn
        if has_extra:
            ox_ref[...] = _dot(xn, wx_ref[...])

    y = _dot(xn_ref[...], w_ref[...])
    if act == "gelu":
        y = 0.5 * y * (1.0 + lax.erf(y * (0.5 ** 0.5)))
    o_ref[...] = y.astype(o_ref.dtype)


def _prenorm_linear(x, mods, k0, bidx, norm_g, w, tm, seq, ncol, act=None, w_extra=None, tn=2048):
    rows, d = x.shape
    n = w.shape[1]
    assert n % tn == 0
    xv, xspec = _row_view(x, tm, seq, ncol)
    vec = pl.BlockSpec((1, d), lambda i, j: (0, 0))
    in_specs = [xspec, vec, _mod_spec(k0, d, bidx), _mod_spec(k0 + 1, d, bidx),
                pl.BlockSpec((d, tn), lambda i, j: (0, j))]
    args = [xv, norm_g.reshape(1, d), mods, mods, w]
    out_specs = [pl.BlockSpec((tm, tn), lambda i, j: (i, j))]
    out_shape = [jax.ShapeDtypeStruct((rows, n), F32)]
    if w_extra is not None:
        nx = w_extra.shape[1]
        in_specs.append(pl.BlockSpec((d, nx), lambda i, j: (0, 0)))
        args.append(w_extra)
        out_specs.append(pl.BlockSpec((tm, nx), lambda i, j: (i, 0)))
        out_shape.append(jax.ShapeDtypeStruct((rows, nx), F32))
    outs = pl.pallas_call(
        functools.partial(_prenorm_linear_kernel, ncol=ncol, act=act, has_extra=w_extra is not None),
        grid=(rows // tm, n // tn),
        in_specs=in_specs,
        out_specs=out_specs,
        out_shape=out_shape,
        scratch_shapes=[pltpu.VMEM((tm, d), BF16)],
        compiler_params=_cparams("parallel", "arbitrary"),
        name="mixer_in_proj",
    )(*args)
    return outs if w_extra is not None else outs[0]


_CONV_PAD = V7X_SUBLANES


def _dn_conv_kernel(p_ref, cw_ref, o_ref, xp_ref, *, seq, sub, n_q, n_qk):
    c = pl.program_id(1)
    cb = p_ref.shape[1]
    half = DN_CONV_K // 2
    xp_ref[0:_CONV_PAD, :] = jnp.zeros((_CONV_PAD, cb), F32)
    xp_ref[_CONV_PAD:_CONV_PAD + seq, :] = p_ref[...]
    xp_ref[_CONV_PAD + seq:, :] = jnp.zeros((_CONV_PAD, cb), F32)
    w = cw_ref[...]

    def conv_tile(i):
        r0 = pl.multiple_of(i * sub, sub)
        win = xp_ref[pl.ds(r0, sub + 2 * _CONV_PAD), :]
        acc = None
        for t in range(DN_CONV_K):
            lo = _CONV_PAD - half + t
            term = win[lo:lo + sub, :] * w[t:t + 1, :]
            acc = term if acc is None else acc + term
        return r0, jax.nn.silu(acc)

    @pl.when(c < n_qk)
    def _():
        scale = jnp.where(c < n_q, DN_HEAD_DIM ** -0.5, 1.0).astype(F32)

        def body(i, carry):
            r0, y = conv_tile(i)
            for hh in range(cb // DN_HEAD_DIM):
                seg = y[:, hh * DN_HEAD_DIM:(hh + 1) * DN_HEAD_DIM]
                seg = seg * lax.rsqrt(jnp.sum(seg * seg, axis=-1, keepdims=True) + EPS)
                o_ref[pl.ds(r0, sub), hh * DN_HEAD_DIM:(hh + 1) * DN_HEAD_DIM] = seg * scale
            return carry

        lax.fori_loop(0, seq // sub, body, 0)

    @pl.when(c >= n_qk)
    def _():
        def body(i, carry):
            r0, y = conv_tile(i)
            o_ref[pl.ds(r0, sub), :] = y
            return carry

        lax.fori_loop(0, seq // sub, body, 0)


def _dn_conv(p, conv_w, seq, qk_dim, conv_dim):
    rows = p.shape[0]
    cb = 256 if seq > 1024 else 1024
    sub = min(seq, 256)
    return pl.pallas_call(
        functools.partial(_dn_conv_kernel, seq=seq, sub=sub, n_q=qk_dim // cb, n_qk=2 * qk_dim // cb),
        grid=(rows // seq, conv_dim // cb),
        in_specs=[pl.BlockSpec((seq, cb), lambda s, c: (s, c)),
                  pl.BlockSpec((DN_CONV_K, cb), lambda s, c: (0, c))],
        out_specs=pl.BlockSpec((seq, cb), lambda s, c: (s, c)),
        out_shape=jax.ShapeDtypeStruct((rows, conv_dim), F32),
        scratch_shapes=[pltpu.VMEM((seq + 2 * _CONV_PAD, cb), F32)],
        compiler_params=_cparams("parallel", "parallel"),
        name="dn_conv",
    )(p, conv_w)


def _dn_gate_kernel(ab_ref, alog_ref, dtb_ref, pm_ref, gc_ref, gr_ref, *, hv):
    tm, width = ab_ref.shape
    tile = V7X_LANES
    x = ab_ref[...]
    lane_full = lax.broadcasted_iota(jnp.int32, (tm, width), 1)
    lane = lax.broadcasted_iota(jnp.int32, (tile, width), 1)
    is_a = (lane % (2 * hv)) < hv
    is_rev = lane >= 2 * hv
    g = -jnp.exp(alog_ref[...]) * jax.nn.softplus(x + dtb_ref[...])
    raw = jnp.where((lane_full % (2 * hv)) < hv, g, jax.nn.sigmoid(x))
    ri = lax.broadcasted_iota(jnp.int32, (tile, tile), 0)
    ci = lax.broadcasted_iota(jnp.int32, (tile, tile), 1)
    same = (ri // DN_CHUNK) == (ci // DN_CHUNK)
    low = jnp.where(same & (ci <= ri), 1.0, 0.0).astype(F32)
    upp = jnp.where(same & (ci >= ri), 1.0, 0.0).astype(F32)
    for t in range(tm // tile):
        blk = raw[t * tile:(t + 1) * tile, :]
        pre = jnp.dot(low, blk, precision=HIGHEST, preferred_element_type=F32)
        suf = jnp.dot(upp, blk, precision=HIGHEST, preferred_element_type=F32)
        out = jnp.where(is_a, jnp.where(is_rev, suf, pre), blk)
        gc_ref[t * tile:(t + 1) * tile, :] = out
        gr_ref[t] = lax.dot_general(pm_ref[...], out, (((1,), (1,)), ((), ())),
                                    precision=HIGHEST, preferred_element_type=F32)


def _gate_perm(hv):
    pm = np.zeros((4 * hv, 4 * hv), np.float32)
    for hq in range(hv // 2):
        for d in range(2):
            for isb in range(2):
                for j in range(2):
                    pm[hq * 8 + d * 4 + isb * 2 + j, d * 2 * hv + isb * hv + 2 * hq + j] = 1.0
    return jnp.asarray(pm)


def _dn_gates(ab, a_log, dt_bias, hv, tm):
    rows, width = ab.shape
    assert width == 4 * hv == V7X_LANES
    zeros = jnp.zeros_like(a_log)
    alog = jnp.concatenate([a_log, zeros], axis=1).reshape(1, width)
    dtb = jnp.concatenate([dt_bias, zeros], axis=1).reshape(1, width)
    vec = pl.BlockSpec((1, width), lambda i: (0, 0))
    return pl.pallas_call(
        functools.partial(_dn_gate_kernel, hv=hv),
        grid=(rows // tm,),
        in_specs=[pl.BlockSpec((tm, width), lambda i: (i, 0)), vec, vec,
                  pl.BlockSpec((width, width), lambda i: (0, 0))],
        out_specs=[pl.BlockSpec((tm, width), lambda i: (i, 0)),
                   pl.BlockSpec((tm // V7X_LANES, width, V7X_LANES), lambda i: (i, 0, 0))],
        out_shape=[jax.ShapeDtypeStruct((rows, width), F32),
                   jax.ShapeDtypeStruct((rows // V7X_LANES, width, V7X_LANES), F32)],
        compiler_params=_cparams("parallel"),
        name="dn_gates",
    )(ab, alog, dtb, _gate_perm(hv))


_INV_BLOCK = 4
_DN_GROUP_TILES = 2
_DN_TRIP_GROUPS = 4


def _mm_bf16(ps, qs):
    return [_dot(p.astype(BF16), q.astype(BF16)) for p, q in zip(ps, qs)]


def _pair_diag(x, lo_half):
    zero = jnp.zeros_like(x)
    return jnp.concatenate([jnp.where(lo_half, x, zero), jnp.where(lo_half, zero, x)], axis=0)


def _unit_tri_inverse(mats, ri, ci, lo_half):
    eye = (ri == ci).astype(F32)
    bi, bj = ri // _INV_BLOCK, ci // _INV_BLOCK
    n = mats[0].shape[0]

    def mm(ps, qs):
        return _mm_bf16(ps, [_pair_diag(q, lo_half) for q in qs])

    ds = [jnp.where(bi == bj, a, 0.0) for a in mats]
    ts = [eye - d for d in ds]
    ps = mm(ds, ds)
    yield
    power = 2
    while 2 * power < _INV_BLOCK:
        both = mm([jnp.concatenate([t, p], axis=0) for t, p in zip(ts, ps)], ps)
        ts = [t + b[:n, :] for t, b in zip(ts, both)]
        ps = [b[n:, :] for b in both]
        power *= 2
        yield
    ts = [t + tp for t, tp in zip(ts, mm(ts, ps))]
    yield
    w = 1
    while w * _INV_BLOCK < n:
        off = (bi // (2 * w) == bj // (2 * w)) & (bi // w != bj // w)
        tes = mm(ts, [jnp.where(off, a, 0.0) for a in mats])
        yield
        ts = [t - tet for t, tet in zip(ts, mm(tes, ts))]
        yield
        w *= 2
    return ts


def _issue_pipelined(n_groups, hop, make_head, make_tail):
    heads, values = {}, {}
    tail, tail_g, tick = None, 0, 0
    while tail_g < n_groups:
        if tick % hop == 0 and tick // hop < n_groups:
            heads[tick // hop] = make_head(tick // hop)
        for g in sorted(heads):
            try:
                next(heads[g])
            except StopIteration as stop:
                values[g] = stop.value
                del heads[g]
        while tail_g < n_groups:
            if tail is None:
                if tail_g not in values:
                    break
                tail = make_tail(values.pop(tail_g))
            try:
                next(tail)
                break
            except StopIteration:
                tail, tail_g = None, tail_g + 1
        tick += 1


def _dn_core_kernel(qc_ref, kc_ref, vc_ref, gcc_ref, grc_ref, ql_ref, kl_ref, vl_ref, gcl_ref, grl_ref,
                    oc_ref, ol_ref, s_ref, *, hv):
    hq = pl.program_id(1)
    dk = DN_HEAD_DIM
    cs = DN_CHUNK
    tile = V7X_LANES
    s_ref[...] = jnp.zeros(s_ref.shape, F32)
    oc_ref[...] = jnp.zeros(oc_ref.shape, F32)
    ol_ref[...] = jnp.zeros(ol_ref.shape, F32)
    ri = lax.broadcasted_iota(jnp.int32, (cs, 2 * cs), 0)
    lane = lax.broadcasted_iota(jnp.int32, (cs, 2 * cs), 1)
    shift = (tile - 2 * hq) % tile

    n_sub = tile // cs
    lo_half = lane < cs
    ci = jnp.where(lo_half, lane, lane - cs)
    incl = (ri >= ci, ri <= ci)
    strict = (ri > ci, ri < ci)
    last = (cs - 1, 0)

    def state_free(refs, t_fwd, t_bwd):
        q_ref, k_ref, v_ref, gc_ref, gr_ref, _ = refs
        chunks = ([(0, t, c) for t in t_fwd for c in range(n_sub)]
                  + [(1, t, c) for t in t_bwd for c in reversed(range(n_sub))])
        r0s = [pl.multiple_of(t * tile, tile) + c * cs for _, t, c in chunks]
        ks = [k_ref[pl.ds(r0, cs), :] for r0 in r0s]
        qs = [q_ref[pl.ds(r0, cs), :] for r0 in r0s]
        kbs = [k.astype(BF16) for k in ks]
        qkks = [_dot_nt(jnp.concatenate([q.astype(BF16), kb], axis=0), jnp.concatenate([kb, kb], axis=0))
                for q, kb in zip(qs, kbs)]
        gcts = [pltpu.roll(gc_ref[pl.ds(r0, cs), :], shift, 1) for r0 in r0s]
        grts = [gr_ref[t] for _, t, _ in chunks]
        yield

        def packed_cols(ic, off):
            base = chunks[ic][0] * 2 * hv + off
            return jnp.where(lo_half, gcts[ic][:, base:base + 1], gcts[ic][:, base + 1:base + 2])

        def packed_row(ic, off):
            d, _, c = chunks[ic]
            r0_, r1_ = grts[ic][d * 4 + off:d * 4 + off + 1, :], grts[ic][d * 4 + off + 1:d * 4 + off + 2, :]
            if c == 0:
                return jnp.where(lo_half[:1], r0_, pltpu.roll(r1_, cs, 1))
            return jnp.where(lo_half[:1], pltpu.roll(r0_, cs, 1), r1_)

        dec2s = [jnp.exp(jnp.where(incl[chunks[ic][0]], packed_cols(ic, 0) - packed_row(ic, 0), -jnp.inf))
                 for ic in range(len(chunks))]
        amats = [jnp.where(strict[chunks[ic][0]], qkks[ic][cs:, :] * dec2s[ic] * packed_cols(ic, hv), 0.0)
                 for ic in range(len(chunks))]
        qkms = [(qkks[ic][:cs, :] * dec2s[ic]).astype(BF16) for ic in range(len(chunks))]
        yield
        tinvs = yield from _unit_tri_inverse(amats, ri, ci, lo_half)
        tinvs = [t.astype(BF16) for t in tinvs]

        scs = [(ic, j) for ic in range(len(chunks)) for j in range(2)]
        dirs = [chunks[ic][0] for ic, _ in scs]
        gcols = [gcts[ic][:, chunks[ic][0] * 2 * hv + j:chunks[ic][0] * 2 * hv + j + 1] for ic, j in scs]
        bcols = [gcts[ic][:, chunks[ic][0] * 2 * hv + hv + j:chunks[ic][0] * 2 * hv + hv + j + 1] for ic, j in scs]

        def head_rows(x, j):
            zero = jnp.zeros_like(x)
            return jnp.concatenate([x, zero] if j == 0 else [zero, x], axis=0)

        rhs = [jnp.concatenate([ks[ic] * (bc * jnp.exp(gc)), v_ref[pl.ds(r0s[ic], cs), j * dk:(j + 1) * dk] * bc],
                               axis=1).astype(BF16) for (ic, j), bc, gc in zip(scs, bcols, gcols)]
        wus = [_dot(tinvs[ic], head_rows(r, j)).astype(BF16) for (ic, j), r in zip(scs, rhs)]
        yield
        qwus = [_dot(qkms[ic], head_rows(wu, j)) for (ic, j), wu in zip(scs, wus)]
        g_lasts = [gc[last[d]:last[d] + 1, :] for d, gc in zip(dirs, gcols)]
        kwus = [_dot_tn((ks[ic] * jnp.exp(gl - gc)).astype(BF16), wu)
                for (ic, _), gl, gc, wu in zip(scs, g_lasts, gcols, wus)]
        yield
        lhs = [jnp.concatenate([kwu[:, :dk].astype(BF16), (qs[ic] * jnp.exp(gc) - qwu[:, :dk]).astype(BF16)], axis=0)
               for (ic, _), gc, qwu, kwu in zip(scs, gcols, qwus, kwus)]
        yield
        return dict(rows=[(r0s[ic], j) for ic, j in scs], lhs=lhs, add=[kwu[:, dk:] for kwu in kwus],
                    intra=[qwu[:, dk:] for qwu in qwus], decay=[jnp.exp(gl) for gl in g_lasts])

    def recurrence(refs, pre, states):
        o_ref = refs[5]
        n_steps = len(pre["lhs"]) // 4
        for step in range(n_steps):
            idx = [(d * n_steps + step) * 2 + j for d in range(2) for j in range(2)]
            rs = [_dot(pre["lhs"][i], st.astype(BF16)) for i, st in zip(idx, states)]
            states[:] = [st * pre["decay"][i] + pre["add"][i] - r[:dk, :] for i, st, r in zip(idx, states, rs)]
            for i, r in zip(idx, rs):
                r0, j = pre["rows"][i]
                o_ref[pl.ds(r0, cs), j * dk:(j + 1) * dk] += r[dk:, :] + pre["intra"][i]
            yield

    def run(refs, n_tiles):
        group = max(p for p in range(1, _DN_GROUP_TILES + 1) if n_tiles % p == 0)
        n_groups = n_tiles // group
        per_trip = max(p for p in range(1, _DN_TRIP_GROUPS + 1) if n_groups % p == 0)
        hop = group * n_sub

        def tiles(g):
            return [g * group + m for m in range(group)], [n_tiles - 1 - g * group - m for m in range(group)]

        def body(i, carry):
            states = [s_ref[si] for si in range(4)]
            _issue_pipelined(per_trip, hop,
                             lambda g: state_free(refs, *tiles(i * per_trip + g)),
                             lambda pre: recurrence(refs, pre, states))
            for si in range(4):
                s_ref[si] = states[si]
            return carry

        lax.fori_loop(0, n_groups // per_trip, body, 0)

    run((qc_ref, kc_ref, vc_ref, gcc_ref, grc_ref, oc_ref), qc_ref.shape[0] // tile)
    run((ql_ref, kl_ref, vl_ref, gcl_ref, grl_ref, ol_ref), ql_ref.shape[0] // tile)


def _dn_core(qkv_c, gc_c, gr_c, qkv_l, gc_l, gr_l, batch, hv):
    dk = DN_HEAD_DIM
    hq_n = hv // 2
    ctx_len = qkv_c.shape[0] // batch
    seq = qkv_l.shape[0] // batch
    width = gc_l.shape[1]

    def specs(t):
        return [pl.BlockSpec((t, dk), lambda b, h: (b, h)),
                pl.BlockSpec((t, dk), lambda b, h: (b, hq_n + h)),
                pl.BlockSpec((t, 2 * dk), lambda b, h: (b, hq_n + h)),
                pl.BlockSpec((t, width), lambda b, h: (b, 0)),
                pl.BlockSpec((t // V7X_LANES, 8, V7X_LANES), lambda b, h: (b, h, 0))]

    return pl.pallas_call(
        functools.partial(_dn_core_kernel, hv=hv),
        grid=(batch, hq_n),
        in_specs=specs(ctx_len) + specs(seq),
        out_specs=[pl.BlockSpec((ctx_len, 2 * dk), lambda b, h: (b, h)),
                   pl.BlockSpec((seq, 2 * dk), lambda b, h: (b, h))],
        out_shape=[jax.ShapeDtypeStruct((batch * ctx_len, hv * dk), F32),
                   jax.ShapeDtypeStruct((batch * seq, hv * dk), F32)],
        scratch_shapes=[pltpu.VMEM((4, dk, dk), F32)],
        compiler_params=_cparams("parallel", "parallel"),
        name="dn_core",
    )(qkv_c, qkv_c, qkv_c, gc_c, gr_c, qkv_l, qkv_l, qkv_l, gc_l, gr_l)


def _dn_out_kernel(o_ref, z_ref, ng_ref, w_ref, x_ref, gt_ref, out_ref, acc_ref, *, ncol):
    k = pl.program_id(1)

    @pl.when(k == 0)
    def _():
        acc_ref[...] = jnp.zeros(acc_ref.shape, F32)

    o = o_ref[...]
    dk = DN_HEAD_DIM
    segs = [_rms(o[:, h * dk:(h + 1) * dk]) * ng_ref[...] for h in range(o.shape[1] // dk)]
    on = jnp.concatenate(segs, axis=1) * jax.nn.silu(z_ref[...])
    acc_ref[...] += _dot(on.astype(BF16), w_ref[...])

    @pl.when(k == pl.num_programs(1) - 1)
    def _():
        _store_rows(out_ref, _load_rows(x_ref, ncol) + gt_ref[...] * acc_ref[...], ncol)


def _dn_out(o, p, z_col0, norm_g, w_out, x, mods, bidx, tm, seq, ncol):
    rows, d = x.shape
    kdim = w_out.shape[0]
    tk = 1024
    xv, xspec = _row_view(x, tm, seq, ncol)
    zb = z_col0 // tk
    out = pl.pallas_call(
        functools.partial(_dn_out_kernel, ncol=ncol),
        grid=(rows // tm, kdim // tk),
        in_specs=[pl.BlockSpec((tm, tk), lambda i, k: (i, k)),
                  pl.BlockSpec((tm, tk), lambda i, k: (i, zb + k)),
                  pl.BlockSpec((1, DN_HEAD_DIM), lambda i, k: (0, 0)),
                  pl.BlockSpec((tk, d), lambda i, k: (k, 0)),
                  xspec, _mod_spec(5, d, bidx)],
        out_specs=xspec,
        out_shape=jax.ShapeDtypeStruct(xv.shape, F32),
        scratch_shapes=[pltpu.VMEM((tm, d), F32)],
        compiler_params=_cparams("parallel", "arbitrary"),
        name="dn_out",
    )(o, p, norm_g.reshape(1, DN_HEAD_DIM), w_out, xv, mods)
    return out.reshape(rows, d)


def _sg_out_kernel(u_ref, v_ref, lg_ref, lb_ref, ws_ref, bs_ref, w_ref, x_ref, gt_ref, out_ref, m_ref, *, ncol):
    v = v_ref[...]
    mu = jnp.mean(v, axis=-1, keepdims=True)
    var = jnp.mean(jnp.square(v - mu), axis=-1, keepdims=True)
    m_ref[...] = ((v - mu) * lax.rsqrt(var + EPS) * lg_ref[...] + lb_ref[...]).astype(BF16)
    tm, sg_dim = v.shape
    gd = sg_dim // SG_GROUPS
    for g in range(SG_GROUPS):
        wsg = ws_ref[g]
        bias = bs_ref[:, g:g + 1]
        for c in range(tm // SG_CHUNK):
            rs = slice(c * SG_CHUNK, (c + 1) * SG_CHUNK)
            cs = slice(g * gd, (g + 1) * gd)
            mixed = _dot(wsg, m_ref[rs, cs]) + bias
            m_ref[rs, cs] = (u_ref[rs, cs] * mixed).astype(BF16)
    y = _dot(m_ref[...], w_ref[...])
    _store_rows(out_ref, _load_rows(x_ref, ncol) + gt_ref[...] * y, ncol)


def _sg_out(uv, ln_g, ln_b, w_s, b_s, w_out, x, mods, bidx, tm, seq, ncol):
    rows, d = x.shape
    sg_dim = w_out.shape[0]
    xv, xspec = _row_view(x, tm, seq, ncol)
    vec = pl.BlockSpec((1, sg_dim), lambda i: (0, 0))
    out = pl.pallas_call(
        functools.partial(_sg_out_kernel, ncol=ncol),
        grid=(rows // tm,),
        in_specs=[pl.BlockSpec((tm, sg_dim), lambda i: (i, 0)),
                  pl.BlockSpec((tm, sg_dim), lambda i: (i, 1)),
                  vec, vec,
                  pl.BlockSpec((SG_GROUPS, SG_CHUNK, SG_CHUNK), lambda i: (0, 0, 0)),
                  pl.BlockSpec((SG_CHUNK, SG_GROUPS), lambda i: (0, 0)),
                  pl.BlockSpec((sg_dim, d), lambda i: (0, 0), pipeline_mode=pl.Buffered(1)),
                  xspec, _mod_spec(5, d, bidx)],
        out_specs=xspec,
        out_shape=jax.ShapeDtypeStruct(xv.shape, F32),
        scratch_shapes=[pltpu.VMEM((tm, sg_dim), BF16)],
        compiler_params=_cparams("parallel"),
        name="sg_out",
    )(uv, uv, ln_g.reshape(1, sg_dim), ln_b.reshape(1, sg_dim), w_s, b_s.T, w_out, xv, mods)
    return out.reshape(rows, d)


def kernel(x, c, ctx, c_ctx, mod_w, mod_b, norm_g, ffn_w_gu, ffn_w_down, dn_w_in, dn_conv_w, dn_a_log, dn_dt_bias,
           dn_norm_g, dn_w_out, sg_w_in, sg_ln_g, sg_ln_b, sg_w_s, sg_b_s, sg_w_out, final_norm_g):
    batch, seq, d = x.shape
    ctx_len = ctx.shape[1]
    depth = mod_w.shape[0]
    nm = mod_w.shape[2]
    hv = dn_a_log.shape[2]
    dk = DN_HEAD_DIM
    qk_dim = hv // 2 * dk
    v_dim = hv * dk
    conv_dim = 2 * qk_dim + v_dim
    assert batch + 1 <= MOD_ROWS

    xl = x.reshape(batch * seq, d)
    xc = ctx.reshape(batch * ctx_len, d)
    tm_l = min(512, seq)
    tm_c = min(512, batch * ctx_len)
    tiles_per_seq = seq // tm_l
    bidx_l = lambda i: i // tiles_per_seq
    bidx_c = lambda i: batch
    ncol_cm = tm_l // (seq // GRID_W)

    cs = jnp.concatenate([c, c_ctx[None, :], jnp.zeros((MOD_ROWS - batch - 1, d), F32)], axis=0)
    mods_all = _modulation(cs, mod_w, mod_b).reshape(depth, MOD_ROWS, 1, nm)

    w_gu = ffn_w_gu.astype(BF16)
    w_down = ffn_w_down.astype(BF16)
    n_mix = 2

    for i in range(depth):
        last = i == depth - 1
        kind = i % n_mix
        j = i // n_mix
        ncol = ncol_cm if (j % 2) == 1 else 0
        need_ctx = not (last and kind == 1)
        mods = mods_all[i]

        xl = _half_ffn(xl, mods, 0, bidx_l, norm_g[i, 0], w_gu, w_down, i, 0, tm_l)
        if need_ctx:
            xc = _half_ffn(xc, mods, 0, bidx_c, norm_g[i, 0], w_gu, w_down, i, 0, tm_c)

        if kind == 0:
            w_in = dn_w_in[j].astype(BF16)
            w_main, w_ab = w_in[:, :conv_dim + v_dim], w_in[:, conv_dim + v_dim:]
            w_out = dn_w_out[j].astype(BF16)
            pl_, ab_l = _prenorm_linear(xl, mods, 3, bidx_l, norm_g[i, 1], w_main, tm_l, seq, ncol, w_extra=w_ab)
            pc_, ab_c = _prenorm_linear(xc, mods, 3, bidx_c, norm_g[i, 1], w_main, tm_c, ctx_len, 0, w_extra=w_ab)
            qkv_l = _dn_conv(pl_, dn_conv_w[j], seq, qk_dim, conv_dim)
            qkv_c = _dn_conv(pc_, dn_conv_w[j], ctx_len, qk_dim, conv_dim)
            gc_l, gr_l = _dn_gates(ab_l, dn_a_log[j], dn_dt_bias[j], hv, tm_l)
            gc_c, gr_c = _dn_gates(ab_c, dn_a_log[j], dn_dt_bias[j], hv, tm_c)
            o_c, o_l = _dn_core(qkv_c, gc_c, gr_c, qkv_l, gc_l, gr_l, batch, hv)
            xl = _dn_out(o_l, pl_, conv_dim, dn_norm_g[j], w_out, xl, mods, bidx_l, tm_l, seq, ncol)
            if not last:
                xc = _dn_out(o_c, pc_, conv_dim, dn_norm_g[j], w_out, xc, mods, bidx_c, tm_c, ctx_len, 0)
        else:
            w_in = sg_w_in[j].astype(BF16)
            w_out = sg_w_out[j].astype(BF16)
            w_s = sg_w_s[j].astype(BF16)
            tm_sl = min(256, seq)
            tm_sc = min(256, batch * ctx_len)
            ncol_s = (tm_sl // (seq // GRID_W)) if ncol else 0
            bidx_sl = lambda i, n=seq // tm_sl: i // n
            uv_l = _prenorm_linear(xl, mods, 3, bidx_l, norm_g[i, 1], w_in, tm_l, seq, ncol, act="gelu")
            xl = _sg_out(uv_l, sg_ln_g[j], sg_ln_b[j], w_s, sg_b_s[j], w_out, xl, mods, bidx_sl, tm_sl, seq, ncol_s)
            if not last:
                uv_c = _prenorm_linear(xc, mods, 3, bidx_c, norm_g[i, 1], w_in, tm_c, ctx_len, 0, act="gelu")
                xc = _sg_out(uv_c, sg_ln_g[j], sg_ln_b[j], w_s, sg_b_s[j], w_out, xc, mods, bidx_c, tm_sc, ctx_len, 0)

        xl = _half_ffn(xl, mods, 6, bidx_l, norm_g[i, 2], w_gu, w_down, i, 1, tm_l,
                       final_g=final_norm_g if last else None)
        if not last:
            xc = _half_ffn(xc, mods, 6, bidx_c, norm_g[i, 2], w_gu, w_down, i, 1, tm_c)

    return xl.reshape(batch, seq, d)
```

```python
import functools

import numpy as np
import jax
import jax.numpy as jnp
from jax import lax
from jax.experimental import pallas as pl
from jax.experimental.pallas import tpu as pltpu

F32 = jnp.float32
BF16 = jnp.bfloat16
EPS = 1e-6
GRID_W = 64
N_MOD = 9
DN_HEAD_DIM = 128
DN_CHUNK = 64
DN_CONV_K = 5
SG_CHUNK = 128
SG_GROUPS = 16

V7X_LANES = 128
V7X_SUBLANES = 8
V7X_VMEM_BYTES = 64 * 1024 * 1024
VMEM_LIMIT = V7X_VMEM_BYTES - 8 * 1024 * 1024
MOD_ROWS = 16
HIGHEST = lax.Precision.HIGHEST


def _cparams(*sem):
    return pltpu.CompilerParams(dimension_semantics=sem, vmem_limit_bytes=VMEM_LIMIT)


def _dot(a, b):
    return jnp.dot(a, b, preferred_element_type=F32)


def _dot_nt(a, b):
    return lax.dot_general(a, b, (((1,), (1,)), ((), ())), preferred_element_type=F32)


def _dot_tn(a, b):
    return lax.dot_general(a, b, (((0,), (0,)), ((), ())), preferred_element_type=F32)


def _rms(x):
    return x * lax.rsqrt(jnp.mean(x * x, axis=-1, keepdims=True) + EPS)


def _load_rows(ref, ncol):
    if not ncol:
        return ref[...]
    width = ref.shape[1] // ncol
    return jnp.concatenate([ref[:, k * width:(k + 1) * width] for k in range(ncol)], axis=0)


def _store_rows(ref, val, ncol):
    if not ncol:
        ref[...] = val
        return
    width = ref.shape[1] // ncol
    rows = ref.shape[0]
    for k in range(ncol):
        ref[:, k * width:(k + 1) * width] = val[k * rows:(k + 1) * rows, :]


def _row_view(arr, tm, seq, ncol):
    width = arr.shape[1]
    if not ncol:
        return arr, pl.BlockSpec((tm, width), lambda i, *_: (i, 0))
    grid_rows = seq // GRID_W
    per_batch = GRID_W // ncol
    view = arr.reshape(arr.shape[0] // seq, grid_rows, GRID_W * width)
    return view, pl.BlockSpec((None, grid_rows, ncol * width), lambda i, *_: (i // per_batch, 0, i % per_batch))


def _mod_spec(k, d, bidx):
    return pl.BlockSpec((None, 1, d), lambda i, *_: (bidx(i), 0, k))


def _mod_kernel(c_ref, w_ref, b_ref, o_ref):
    s = jax.nn.silu(c_ref[...]).astype(BF16)
    o_ref[...] = _dot(s, w_ref[...].astype(BF16)) + b_ref[...]


def _modulation(cs, mod_w, mod_b):
    depth, d, nm = mod_w.shape
    tn = 1024
    return pl.pallas_call(
        _mod_kernel,
        grid=(depth, nm // tn),
        in_specs=[pl.BlockSpec((MOD_ROWS, d), lambda l, j: (0, 0)),
                  pl.BlockSpec((None, d, tn), lambda l, j: (l, 0, j)),
                  pl.BlockSpec((None, 1, tn), lambda l, j: (l, 0, j))],
        out_specs=pl.BlockSpec((None, MOD_ROWS, tn), lambda l, j: (l, 0, j)),
        out_shape=jax.ShapeDtypeStruct((depth, MOD_ROWS, nm), F32),
        compiler_params=_cparams("parallel", "parallel"),
        name="modulation",
    )(cs, mod_w, mod_b.reshape(depth, 1, nm))


def _ffn_kernel(x_ref, ng_ref, sh_ref, sc_ref, gt_ref, wg_ref, wu_ref, wd_ref, *rest, final_norm):
    if final_norm:
        fg_ref, o_ref, xn_ref = rest
    else:
        o_ref, xn_ref = rest
    j = pl.program_id(1)

    @pl.when(j == 0)
    def _():
        gain = ng_ref[...] * (1.0 + sc_ref[...])
        xn_ref[...] = (_rms(x_ref[...]) * gain + sh_ref[...]).astype(BF16)
        o_ref[...] = jnp.zeros(o_ref.shape, F32)

    xn = xn_ref[...]
    g = _dot(xn, wg_ref[...])
    u = _dot(xn, wu_ref[...])
    h = (jax.nn.silu(g) * u).astype(BF16)
    o_ref[...] += _dot(h, wd_ref[...])

    @pl.when(j == pl.num_programs(1) - 1)
    def _():
        r = x_ref[...] + 0.5 * gt_ref[...] * o_ref[...]
        if final_norm:
            r = _rms(r) * fg_ref[...]
        o_ref[...] = r


def _half_ffn(x, mods, k0, bidx, norm_g, w_gu, w_down, layer, half, tm, final_g=None):
    rows, d = x.shape
    dff = w_down.shape[2]
    fc = 512
    nf = dff // fc
    row = pl.BlockSpec((tm, d), lambda i, j: (i, 0))
    row_in = pl.BlockSpec((tm, d), lambda i, j: (i, 0), pipeline_mode=pl.Buffered(1)) if tm > 512 else row
    vec = pl.BlockSpec((1, d), lambda i, j: (0, 0))
    in_specs = [row_in, vec, _mod_spec(k0, d, bidx), _mod_spec(k0 + 1, d, bidx), _mod_spec(k0 + 2, d, bidx),
                pl.BlockSpec((None, None, d, fc), lambda i, j: (layer, half, 0, j)),
                pl.BlockSpec((None, None, d, fc), lambda i, j: (layer, half, 0, nf + j)),
                pl.BlockSpec((None, None, fc, d), lambda i, j: (layer, half, j, 0))]
    args = [x, norm_g.reshape(1, d), mods, mods, mods, w_gu, w_gu, w_down]
    if final_g is not None:
        in_specs.append(vec)
        args.append(final_g.reshape(1, d))
    return pl.pallas_call(
        functools.partial(_ffn_kernel, final_norm=final_g is not None),
        grid=(rows // tm, nf),
        in_specs=in_specs,
        out_specs=row,
        out_shape=jax.ShapeDtypeStruct((rows, d), F32),
        scratch_shapes=[pltpu.VMEM((tm, d), BF16)],
        compiler_params=_cparams("parallel", "arbitrary"),
        name="half_ffn",
    )(*args)


def _prenorm_linear_kernel(x_ref, ng_ref, sh_ref, sc_ref, w_ref, *rest, ncol, act, has_extra):
    if has_extra:
        wx_ref, o_ref, ox_ref, xn_ref = rest
    else:
        o_ref, xn_ref = rest

    @pl.when(pl.program_id(1) == 0)
    def _():
        gain = ng_ref[...] * (1.0 + sc_ref[...])
        xn = (_rms(_load_rows(x_ref, ncol)) * gain + sh_ref[...]).astype(BF16)
        xn_ref[...] = xn
        if has_extra:
            ox_ref[...] = _dot(xn, wx_ref[...])

    y = _dot(xn_ref[...], w_ref[...])
    if act == "gelu":
        y = 0.5 * y * (1.0 + lax.erf(y * (0.5 ** 0.5)))
    o_ref[...] = y.astype(o_ref.dtype)


def _prenorm_linear(x, mods, k0, bidx, norm_g, w, tm, seq, ncol, act=None, w_extra=None, tn=2048):
    rows, d = x.shape
    n = w.shape[1]
    assert n % tn == 0
    xv, xspec = _row_view(x, tm, seq, ncol)
    vec = pl.BlockSpec((1, d), lambda i, j: (0, 0))
    in_specs = [xspec, vec, _mod_spec(k0, d, bidx), _mod_spec(k0 + 1, d, bidx),
                pl.BlockSpec((d, tn), lambda i, j: (0, j))]
    args = [xv, norm_g.reshape(1, d), mods, mods, w]
    out_specs = [pl.BlockSpec((tm, tn), lambda i, j: (i, j))]
    out_shape = [jax.ShapeDtypeStruct((rows, n), F32)]
    if w_extra is not None:
        nx = w_extra.shape[1]
        in_specs.append(pl.BlockSpec((d, nx), lambda i, j: (0, 0)))
        args.append(w_extra)
        out_specs.append(pl.BlockSpec((tm, nx), lambda i, j: (i, 0)))
        out_shape.append(jax.ShapeDtypeStruct((rows, nx), F32))
    outs = pl.pallas_call(
        functools.partial(_prenorm_linear_kernel, ncol=ncol, act=act, has_extra=w_extra is not None),
        grid=(rows // tm, n // tn),
        in_specs=in_specs,
        out_specs=out_specs,
        out_shape=out_shape,
        scratch_shapes=[pltpu.VMEM((tm, d), BF16)],
        compiler_params=_cparams("parallel", "arbitrary"),
        name="mixer_in_proj",
    )(*args)
    return outs if w_extra is not None else outs[0]


_CONV_PAD = V7X_SUBLANES


def _dn_conv_kernel(p_ref, cw_ref, o_ref, xp_ref, *, seq, sub, n_q, n_qk):
    c = pl.program_id(1)
    cb = p_ref.shape[1]
    half = DN_CONV_K // 2
    xp_ref[0:_CONV_PAD, :] = jnp.zeros((_CONV_PAD, cb), F32)
    xp_ref[_CONV_PAD:_CONV_PAD + seq, :] = p_ref[...]
    xp_ref[_CONV_PAD + seq:, :] = jnp.zeros((_CONV_PAD, cb), F32)
    w = cw_ref[...]

    def conv_tile(i):
        r0 = pl.multiple_of(i * sub, sub)
        win = xp_ref[pl.ds(r0, sub + 2 * _CONV_PAD), :]
        acc = None
        for t in range(DN_CONV_K):
            lo = _CONV_PAD - half + t
            term = win[lo:lo + sub, :] * w[t:t + 1, :]
            acc = term if acc is None else acc + term
        return r0, jax.nn.silu(acc)

    @pl.when(c < n_qk)
    def _():
        scale = jnp.where(c < n_q, DN_HEAD_DIM ** -0.5, 1.0).astype(F32)

        def body(i, carry):
            r0, y = conv_tile(i)
            for hh in range(cb // DN_HEAD_DIM):
                seg = y[:, hh * DN_HEAD_DIM:(hh + 1) * DN_HEAD_DIM]
                seg = seg * lax.rsqrt(jnp.sum(seg * seg, axis=-1, keepdims=True) + EPS)
                o_ref[pl.ds(r0, sub), hh * DN_HEAD_DIM:(hh + 1) * DN_HEAD_DIM] = (seg * scale).astype(o_ref.dtype)
            return carry

        lax.fori_loop(0, seq // sub, body, 0)

    @pl.when(c >= n_qk)
    def _():
        def body(i, carry):
            r0, y = conv_tile(i)
            o_ref[pl.ds(r0, sub), :] = y.astype(o_ref.dtype)
            return carry

        lax.fori_loop(0, seq // sub, body, 0)


def _dn_conv(p, conv_w, seq, qk_dim, conv_dim):
    rows = p.shape[0]
    cb = 256 if seq > 1024 else 1024
    sub = min(seq, 256)
    return pl.pallas_call(
        functools.partial(_dn_conv_kernel, seq=seq, sub=sub, n_q=qk_dim // cb, n_qk=2 * qk_dim // cb),
        grid=(rows // seq, conv_dim // cb),
        in_specs=[pl.BlockSpec((seq, cb), lambda s, c: (s, c)),
                  pl.BlockSpec((DN_CONV_K, cb), lambda s, c: (0, c))],
        out_specs=pl.BlockSpec((seq, cb), lambda s, c: (s, c)),
        out_shape=jax.ShapeDtypeStruct((rows, conv_dim), BF16),
        scratch_shapes=[pltpu.VMEM((seq + 2 * _CONV_PAD, cb), F32)],
        compiler_params=_cparams("parallel", "parallel"),
        name="dn_conv",
    )(p, conv_w)


def _dn_gate_kernel(ab_ref, alog_ref, dtb_ref, pm_ref, gc_ref, gr_ref, *, hv):
    tm, width = ab_ref.shape
    tile = V7X_LANES
    x = ab_ref[...]
    lane_full = lax.broadcasted_iota(jnp.int32, (tm, width), 1)
    lane = lax.broadcasted_iota(jnp.int32, (tile, width), 1)
    is_a = (lane % (2 * hv)) < hv
    is_rev = lane >= 2 * hv
    g = -jnp.exp(alog_ref[...]) * jax.nn.softplus(x + dtb_ref[...])
    raw = jnp.where((lane_full % (2 * hv)) < hv, g, jax.nn.sigmoid(x))
    ri = lax.broadcasted_iota(jnp.int32, (tile, tile), 0)
    ci = lax.broadcasted_iota(jnp.int32, (tile, tile), 1)
    same = (ri // DN_CHUNK) == (ci // DN_CHUNK)
    low = jnp.where(same & (ci <= ri), 1.0, 0.0).astype(F32)
    upp = jnp.where(same & (ci >= ri), 1.0, 0.0).astype(F32)
    for t in range(tm // tile):
        blk = raw[t * tile:(t + 1) * tile, :]
        pre = jnp.dot(low, blk, precision=HIGHEST, preferred_element_type=F32)
        suf = jnp.dot(upp, blk, precision=HIGHEST, preferred_element_type=F32)
        out = jnp.where(is_a, jnp.where(is_rev, suf, pre), blk)
        gc_ref[t * tile:(t + 1) * tile, :] = out
        gr_ref[t] = lax.dot_general(pm_ref[...], out, (((1,), (1,)), ((), ())),
                                    precision=HIGHEST, preferred_element_type=F32)


def _gate_perm(hv):
    pm = np.zeros((4 * hv, 4 * hv), np.float32)
    for hq in range(hv // 2):
        for d in range(2):
            for isb in range(2):
                for j in range(2):
                    pm[hq * 8 + d * 4 + isb * 2 + j, d * 2 * hv + isb * hv + 2 * hq + j] = 1.0
    return jnp.asarray(pm)


def _dn_gates(ab, a_log, dt_bias, hv, tm):
    rows, width = ab.shape
    assert width == 4 * hv == V7X_LANES
    zeros = jnp.zeros_like(a_log)
    alog = jnp.concatenate([a_log, zeros], axis=1).reshape(1, width)
    dtb = jnp.concatenate([dt_bias, zeros], axis=1).reshape(1, width)
    vec = pl.BlockSpec((1, width), lambda i: (0, 0))
    return pl.pallas_call(
        functools.partial(_dn_gate_kernel, hv=hv),
        grid=(rows // tm,),
        in_specs=[pl.BlockSpec((tm, width), lambda i: (i, 0)), vec, vec,
                  pl.BlockSpec((width, width), lambda i: (0, 0))],
        out_specs=[pl.BlockSpec((tm, width), lambda i: (i, 0)),
                   pl.BlockSpec((tm // V7X_LANES, width, V7X_LANES), lambda i: (i, 0, 0))],
        out_shape=[jax.ShapeDtypeStruct((rows, width), F32),
                   jax.ShapeDtypeStruct((rows // V7X_LANES, width, V7X_LANES), F32)],
        compiler_params=_cparams("parallel"),
        name="dn_gates",
    )(ab, alog, dtb, _gate_perm(hv))


_INV_BLOCK = 4
_DN_GROUP_TILES = 2
_DN_TRIP_GROUPS = 4
_DN_HEADS_PER_STEP = 1


def _mm_bf16(ps, qs):
    return [_dot(p.astype(BF16), q.astype(BF16)) for p, q in zip(ps, qs)]


def _pair_diag(x, lo_half):
    zero = jnp.zeros_like(x)
    return jnp.concatenate([jnp.where(lo_half, x, zero), jnp.where(lo_half, zero, x)], axis=0)


def _unit_tri_inverse(mats, ri, ci, lo_half):
    eye = (ri == ci).astype(F32)
    bi, bj = ri // _INV_BLOCK, ci // _INV_BLOCK
    n = mats[0].shape[0]

    def mm(ps, qs):
        return _mm_bf16(ps, [_pair_diag(q, lo_half) for q in qs])

    ds = [jnp.where(bi == bj, a, 0.0) for a in mats]
    ts = [eye - d for d in ds]
    ps = mm(ds, ds)
    yield
    power = 2
    while 2 * power < _INV_BLOCK:
        both = mm([jnp.concatenate([t, p], axis=0) for t, p in zip(ts, ps)], ps)
        ts = [t + b[:n, :] for t, b in zip(ts, both)]
        ps = [b[n:, :] for b in both]
        power *= 2
        yield
    ts = [t + tp for t, tp in zip(ts, mm(ts, ps))]
    yield
    w = 1
    while w * _INV_BLOCK < n:
        off = (bi // (2 * w) == bj // (2 * w)) & (bi // w != bj // w)
        tes = mm(ts, [jnp.where(off, a, 0.0) for a in mats])
        yield
        ts = [t - tet for t, tet in zip(ts, mm(tes, ts))]
        yield
        w *= 2
    return ts


def _issue_pipelined(n_groups, hop, make_head, make_tail):
    heads, values = {}, {}
    tail, tail_g, tick = None, 0, 0
    while tail_g < n_groups:
        if tick % hop == 0 and tick // hop < n_groups:
            heads[tick // hop] = make_head(tick // hop)
        for g in sorted(heads):
            try:
                next(heads[g])
            except StopIteration as stop:
                values[g] = stop.value
                del heads[g]
        while tail_g < n_groups:
            if tail is None:
                if tail_g not in values:
                    break
                tail = make_tail(values.pop(tail_g))
            try:
                next(tail)
                break
            except StopIteration:
                tail, tail_g = None, tail_g + 1
        tick += 1


def _dn_core_kernel(qc_ref, kc_ref, vc_ref, gcc_ref, grc_ref, ql_ref, kl_ref, vl_ref, gcl_ref, grl_ref,
                    oc_ref, ol_ref, s_ref, *, hv):
    dk = DN_HEAD_DIM
    cs = DN_CHUNK
    tile = V7X_LANES
    n_h = _DN_HEADS_PER_STEP
    n_streams = 4 * n_h
    s_ref[...] = jnp.zeros(s_ref.shape, F32)
    oc_ref[...] = jnp.zeros(oc_ref.shape, F32)
    ol_ref[...] = jnp.zeros(ol_ref.shape, F32)
    ri = lax.broadcasted_iota(jnp.int32, (cs, 2 * cs), 0)
    lane = lax.broadcasted_iota(jnp.int32, (cs, 2 * cs), 1)
    shifts = [(tile - 2 * (n_h * pl.program_id(1) + hh)) % tile for hh in range(n_h)]

    n_sub = tile // cs
    lo_half = lane < cs
    ci = jnp.where(lo_half, lane, lane - cs)
    incl = (ri >= ci, ri <= ci)
    strict = (ri > ci, ri < ci)
    last = (cs - 1, 0)

    def state_free(refs, t_fwd, t_bwd):
        q_ref, k_ref, v_ref, gc_ref, gr_ref, _ = refs
        chunks = ([(0, t, c) for t in t_fwd for c in range(n_sub)]
                  + [(1, t, c) for t in t_bwd for c in reversed(range(n_sub))])
        r0s = [pl.multiple_of(t * tile, tile) + c * cs for _, t, c in chunks]
        cells = [(ic, hh) for ic in range(len(chunks)) for hh in range(n_h)]
        cdir = [chunks[ic][0] for ic, _ in cells]
        kbs = [k_ref[pl.ds(r0s[ic], cs), hh * dk:(hh + 1) * dk] for ic, hh in cells]
        qbs = [q_ref[pl.ds(r0s[ic], cs), hh * dk:(hh + 1) * dk] for ic, hh in cells]
        qkks = [_dot_nt(jnp.concatenate([qb, kb], axis=0), jnp.concatenate([kb, kb], axis=0))
                for qb, kb in zip(qbs, kbs)]
        gcts = [pltpu.roll(gc_ref[pl.ds(r0s[ic], cs), :], shifts[hh], 1) for ic, hh in cells]
        grts = [gr_ref[chunks[ic][1], hh * 8:(hh + 1) * 8, :] for ic, hh in cells]
        yield

        def packed_cols(m, off):
            base = cdir[m] * 2 * hv + off
            return jnp.where(lo_half, gcts[m][:, base:base + 1], gcts[m][:, base + 1:base + 2])

        def packed_row(m, off):
            d, _, c = chunks[cells[m][0]]
            r0_, r1_ = grts[m][d * 4 + off:d * 4 + off + 1, :], grts[m][d * 4 + off + 1:d * 4 + off + 2, :]
            if c == 0:
                return jnp.where(lo_half[:1], r0_, pltpu.roll(r1_, cs, 1))
            return jnp.where(lo_half[:1], pltpu.roll(r0_, cs, 1), r1_)

        n_cells = len(cells)
        dec2s = [jnp.exp(jnp.where(incl[cdir[m]], packed_cols(m, 0) - packed_row(m, 0), -jnp.inf))
                 for m in range(n_cells)]
        amats = [jnp.where(strict[cdir[m]], qkks[m][cs:, :] * dec2s[m] * packed_cols(m, hv), 0.0)
                 for m in range(n_cells)]
        qkms = [(qkks[m][:cs, :] * dec2s[m]).astype(BF16) for m in range(n_cells)]
        yield
        tinvs = yield from _unit_tri_inverse(amats, ri, ci, lo_half)
        tinvs = [t.astype(BF16) for t in tinvs]

        scs = [(m, j) for m in range(n_cells) for j in range(2)]
        dirs = [cdir[m] for m, _ in scs]
        gcols = [gcts[m][:, cdir[m] * 2 * hv + j:cdir[m] * 2 * hv + j + 1] for m, j in scs]
        bcols = [gcts[m][:, cdir[m] * 2 * hv + hv + j:cdir[m] * 2 * hv + hv + j + 1] for m, j in scs]
        vcols = [(cells[m][1] * 2 + j) * dk for m, j in scs]
        rows = [r0s[cells[m][0]] for m, _ in scs]
        ks = [kb.astype(F32) for kb in kbs]

        def head_rows(x, j):
            zero = jnp.zeros_like(x)
            return jnp.concatenate([x, zero] if j == 0 else [zero, x], axis=0)

        rhs = [jnp.concatenate([ks[m] * (bc * jnp.exp(gc)), v_ref[pl.ds(r0, cs), vc:vc + dk] * bc],
                               axis=1).astype(BF16)
               for (m, _), bc, gc, r0, vc in zip(scs, bcols, gcols, rows, vcols)]
        wus = [_dot(tinvs[m], head_rows(r, j)).astype(BF16) for (m, j), r in zip(scs, rhs)]
        yield
        qwus = [_dot(qkms[m], head_rows(wu, j)) for (m, j), wu in zip(scs, wus)]
        g_lasts = [gc[last[d]:last[d] + 1, :] for d, gc in zip(dirs, gcols)]
        kwus = [_dot_tn((ks[m] * jnp.exp(gl - gc)).astype(BF16), wu)
                for (m, _), gl, gc, wu in zip(scs, g_lasts, gcols, wus)]
        yield
        lhs = [jnp.concatenate([kwu[:, :dk].astype(BF16), (qbs[m] * jnp.exp(gc) - qwu[:, :dk]).astype(BF16)], axis=0)
               for (m, _), gc, qwu, kwu in zip(scs, gcols, qwus, kwus)]
        yield
        return dict(rows=list(zip(rows, vcols)), lhs=lhs, add=[kwu[:, dk:] for kwu in kwus],
                    intra=[qwu[:, dk:] for qwu in qwus], decay=[jnp.exp(gl) for gl in g_lasts])

    def recurrence(refs, pre, states):
        o_ref = refs[5]
        n_steps = len(pre["lhs"]) // n_streams
        for step in range(n_steps):
            idx = [((d * n_steps + step) * n_h + hh) * 2 + j for d in range(2) for hh in range(n_h) for j in range(2)]
            rs = [_dot(pre["lhs"][i], st.astype(BF16)) for i, st in zip(idx, states)]
            states[:] = [st * pre["decay"][i] + pre["add"][i] - r[:dk, :] for i, st, r in zip(idx, states, rs)]
            for i, r in zip(idx, rs):
                r0, vc = pre["rows"][i]
                o_ref[pl.ds(r0, cs), vc:vc + dk] += r[dk:, :] + pre["intra"][i]
            yield

    def run(refs, n_tiles):
        group = max(p for p in range(1, _DN_GROUP_TILES + 1) if n_tiles % p == 0)
        n_groups = n_tiles // group
        per_trip = max(p for p in range(1, _DN_TRIP_GROUPS + 1) if n_groups % p == 0)
        hop = group * n_sub

        def tiles(g):
            return [g * group + m for m in range(group)], [n_tiles - 1 - g * group - m for m in range(group)]

        def body(i, carry):
            states = [s_ref[si] for si in range(n_streams)]
            _issue_pipelined(per_trip, hop,
                             lambda g: state_free(refs, *tiles(i * per_trip + g)),
                             lambda pre: recurrence(refs, pre, states))
            for si in range(n_streams):
                s_ref[si] = states[si]
            return carry

        lax.fori_loop(0, n_groups // per_trip, body, 0)

    run((qc_ref, kc_ref, vc_ref, gcc_ref, grc_ref, oc_ref), qc_ref.shape[0] // tile)
    run((ql_ref, kl_ref, vl_ref, gcl_ref, grl_ref, ol_ref), ql_ref.shape[0] // tile)


def _dn_core(qkv_c, gc_c, gr_c, qkv_l, gc_l, gr_l, batch, hv):
    dk = DN_HEAD_DIM
    n_h = _DN_HEADS_PER_STEP
    steps = hv // 2 // n_h
    ctx_len = qkv_c.shape[0] // batch
    seq = qkv_l.shape[0] // batch
    width = gc_l.shape[1]

    def specs(t):
        return [pl.BlockSpec((t, n_h * dk), lambda b, h: (b, h)),
                pl.BlockSpec((t, n_h * dk), lambda b, h: (b, steps + h)),
                pl.BlockSpec((t, 2 * n_h * dk), lambda b, h: (b, steps + h)),
                pl.BlockSpec((t, width), lambda b, h: (b, 0)),
                pl.BlockSpec((t // V7X_LANES, 8 * n_h, V7X_LANES), lambda b, h: (b, h, 0))]

    return pl.pallas_call(
        functools.partial(_dn_core_kernel, hv=hv),
        grid=(batch, steps),
        in_specs=specs(ctx_len) + specs(seq),
        out_specs=[pl.BlockSpec((ctx_len, 2 * n_h * dk), lambda b, h: (b, h)),
                   pl.BlockSpec((seq, 2 * n_h * dk), lambda b, h: (b, h))],
        out_shape=[jax.ShapeDtypeStruct((batch * ctx_len, hv * dk), F32),
                   jax.ShapeDtypeStruct((batch * seq, hv * dk), F32)],
        scratch_shapes=[pltpu.VMEM((4 * n_h, dk, dk), F32)],
        compiler_params=_cparams("parallel", "parallel"),
        name="dn_core",
    )(qkv_c, qkv_c, qkv_c, gc_c, gr_c, qkv_l, qkv_l, qkv_l, gc_l, gr_l)


def _dn_out_kernel(o_ref, z_ref, ng_ref, w_ref, x_ref, gt_ref, out_ref, acc_ref, *, ncol):
    k = pl.program_id(1)

    @pl.when(k == 0)
    def _():
        acc_ref[...] = jnp.zeros(acc_ref.shape, F32)

    o = o_ref[...]
    dk = DN_HEAD_DIM
    segs = [_rms(o[:, h * dk:(h + 1) * dk]) * ng_ref[...] for h in range(o.shape[1] // dk)]
    on = jnp.concatenate(segs, axis=1) * jax.nn.silu(z_ref[...])
    acc_ref[...] += _dot(on.astype(BF16), w_ref[...])

    @pl.when(k == pl.num_programs(1) - 1)
    def _():
        _store_rows(out_ref, _load_rows(x_ref, ncol) + gt_ref[...] * acc_ref[...], ncol)


def _dn_out(o, p, z_col0, norm_g, w_out, x, mods, bidx, tm, seq, ncol):
    rows, d = x.shape
    kdim = w_out.shape[0]
    tk = 1024
    xv, xspec = _row_view(x, tm, seq, ncol)
    zb = z_col0 // tk
    out = pl.pallas_call(
        functools.partial(_dn_out_kernel, ncol=ncol),
        grid=(rows // tm, kdim // tk),
        in_specs=[pl.BlockSpec((tm, tk), lambda i, k: (i, k)),
                  pl.BlockSpec((tm, tk), lambda i, k: (i, zb + k)),
                  pl.BlockSpec((1, DN_HEAD_DIM), lambda i, k: (0, 0)),
                  pl.BlockSpec((tk, d), lambda i, k: (k, 0)),
                  xspec, _mod_spec(5, d, bidx)],
        out_specs=xspec,
        out_shape=jax.ShapeDtypeStruct(xv.shape, F32),
        scratch_shapes=[pltpu.VMEM((tm, d), F32)],
        compiler_params=_cparams("parallel", "arbitrary"),
        name="dn_out",
    )(o, p, norm_g.reshape(1, DN_HEAD_DIM), w_out, xv, mods)
    return out.reshape(rows, d)


def _sg_out_kernel(u_ref, v_ref, lg_ref, lb_ref, ws_ref, bs_ref, w_ref, x_ref, gt_ref, out_ref, m_ref, *, ncol):
    v = v_ref[...]
    mu = jnp.mean(v, axis=-1, keepdims=True)
    var = jnp.mean(jnp.square(v - mu), axis=-1, keepdims=True)
    m_ref[...] = ((v - mu) * lax.rsqrt(var + EPS) * lg_ref[...] + lb_ref[...]).astype(BF16)
    tm, sg_dim = v.shape
    gd = sg_dim // SG_GROUPS
    for g in range(SG_GROUPS):
        wsg = ws_ref[g]
        bias = bs_ref[:, g:g + 1]
        for c in range(tm // SG_CHUNK):
            rs = slice(c * SG_CHUNK, (c + 1) * SG_CHUNK)
            cs = slice(g * gd, (g + 1) * gd)
            mixed = _dot(wsg, m_ref[rs, cs]) + bias
            m_ref[rs, cs] = (u_ref[rs, cs] * mixed).astype(BF16)
    y = _dot(m_ref[...], w_ref[...])
    _store_rows(out_ref, _load_rows(x_ref, ncol) + gt_ref[...] * y, ncol)


def _sg_out(uv, ln_g, ln_b, w_s, b_s, w_out, x, mods, bidx, tm, seq, ncol):
    rows, d = x.shape
    sg_dim = w_out.shape[0]
    xv, xspec = _row_view(x, tm, seq, ncol)
    vec = pl.BlockSpec((1, sg_dim), lambda i: (0, 0))
    out = pl.pallas_call(
        functools.partial(_sg_out_kernel, ncol=ncol),
        grid=(rows // tm,),
        in_specs=[pl.BlockSpec((tm, sg_dim), lambda i: (i, 0)),
                  pl.BlockSpec((tm, sg_dim), lambda i: (i, 1)),
                  vec, vec,
                  pl.BlockSpec((SG_GROUPS, SG_CHUNK, SG_CHUNK), lambda i: (0, 0, 0)),
                  pl.BlockSpec((SG_CHUNK, SG_GROUPS), lambda i: (0, 0)),
                  pl.BlockSpec((sg_dim, d), lambda i: (0, 0), pipeline_mode=pl.Buffered(1)),
                  xspec, _mod_spec(5, d, bidx)],
        out_specs=xspec,
        out_shape=jax.ShapeDtypeStruct(xv.shape, F32),
        scratch_shapes=[pltpu.VMEM((tm, sg_dim), BF16)],
        compiler_params=_cparams("parallel"),
        name="sg_out",
    )(uv, uv, ln_g.reshape(1, sg_dim), ln_b.reshape(1, sg_dim), w_s, b_s.T, w_out, xv, mods)
    return out.reshape(rows, d)


def kernel(x, c, ctx, c_ctx, mod_w, mod_b, norm_g, ffn_w_gu, ffn_w_down, dn_w_in, dn_conv_w, dn_a_log, dn_dt_bias,
           dn_norm_g, dn_w_out, sg_w_in, sg_ln_g, sg_ln_b, sg_w_s, sg_b_s, sg_w_out, final_norm_g):
    batch, seq, d = x.shape
    ctx_len = ctx.shape[1]
    depth = mod_w.shape[0]
    nm = mod_w.shape[2]
    hv = dn_a_log.shape[2]
    dk = DN_HEAD_DIM
    qk_dim = hv // 2 * dk
    v_dim = hv * dk
    conv_dim = 2 * qk_dim + v_dim
    assert batch + 1 <= MOD_ROWS

    xl = x.reshape(batch * seq, d)
    xc = ctx.reshape(batch * ctx_len, d)
    tm_l = min(512, seq)
    tm_c = min(512, batch * ctx_len)
    tiles_per_seq = seq // tm_l
    bidx_l = lambda i: i // tiles_per_seq
    bidx_c = lambda i: batch
    ncol_cm = tm_l // (seq // GRID_W)
    tm_f = min(1024, seq)
    bidx_f = lambda i, n=seq // tm_f: i // n

    cs = jnp.concatenate([c, c_ctx[None, :], jnp.zeros((MOD_ROWS - batch - 1, d), F32)], axis=0)
    mods_all = _modulation(cs, mod_w, mod_b).reshape(depth, MOD_ROWS, 1, nm)

    w_gu = ffn_w_gu.astype(BF16)
    w_down = ffn_w_down.astype(BF16)
    n_mix = 2

    for i in range(depth):
        last = i == depth - 1
        kind = i % n_mix
        j = i // n_mix
        ncol = ncol_cm if (j % 2) == 1 else 0
        need_ctx = not (last and kind == 1)
        mods = mods_all[i]

        xl = _half_ffn(xl, mods, 0, bidx_f, norm_g[i, 0], w_gu, w_down, i, 0, tm_f)
        if need_ctx:
            xc = _half_ffn(xc, mods, 0, bidx_c, norm_g[i, 0], w_gu, w_down, i, 0, tm_c)

        if kind == 0:
            w_in = dn_w_in[j].astype(BF16)
            w_main, w_ab = w_in[:, :conv_dim + v_dim], w_in[:, conv_dim + v_dim:]
            w_out = dn_w_out[j].astype(BF16)
            pl_, ab_l = _prenorm_linear(xl, mods, 3, bidx_l, norm_g[i, 1], w_main, tm_l, seq, ncol, w_extra=w_ab)
            pc_, ab_c = _prenorm_linear(xc, mods, 3, bidx_c, norm_g[i, 1], w_main, tm_c, ctx_len, 0, w_extra=w_ab)
            qkv_l = _dn_conv(pl_, dn_conv_w[j], seq, qk_dim, conv_dim)
            qkv_c = _dn_conv(pc_, dn_conv_w[j], ctx_len, qk_dim, conv_dim)
            gc_l, gr_l = _dn_gates(ab_l, dn_a_log[j], dn_dt_bias[j], hv, tm_l)
            gc_c, gr_c = _dn_gates(ab_c, dn_a_log[j], dn_dt_bias[j], hv, tm_c)
            o_c, o_l = _dn_core(qkv_c, gc_c, gr_c, qkv_l, gc_l, gr_l, batch, hv)
            xl = _dn_out(o_l, pl_, conv_dim, dn_norm_g[j], w_out, xl, mods, bidx_l, tm_l, seq, ncol)
            if not last:
                xc = _dn_out(o_c, pc_, conv_dim, dn_norm_g[j], w_out, xc, mods, bidx_c, tm_c, ctx_len, 0)
        else:
            w_in = sg_w_in[j].astype(BF16)
            w_out = sg_w_out[j].astype(BF16)
            w_s = sg_w_s[j].astype(BF16)
            tm_sl = min(256, seq)
            tm_sc = min(256, batch * ctx_len)
            ncol_s = (tm_sl // (seq // GRID_W)) if ncol else 0
            bidx_sl = lambda i, n=seq // tm_sl: i // n
            uv_l = _prenorm_linear(xl, mods, 3, bidx_l, norm_g[i, 1], w_in, tm_l, seq, ncol, act="gelu")
            xl = _sg_out(uv_l, sg_ln_g[j], sg_ln_b[j], w_s, sg_b_s[j], w_out, xl, mods, bidx_sl, tm_sl, seq, ncol_s)
            if not last:
                uv_c = _prenorm_linear(xc, mods, 3, bidx_c, norm_g[i, 1], w_in, tm_c, ctx_len, 0, act="gelu")
                xc = _sg_out(uv_c, sg_ln_g[j], sg_ln_b[j], w_s, sg_b_s[j], w_out, xc, mods, bidx_c, tm_sc, ctx_len, 0)

        xl = _half_ffn(xl, mods, 6, bidx_f, norm_g[i, 2], w_gu, w_down, i, 1, tm_f,
                       final_g=final_norm_g if last else None)
        if not last:
            xc = _half_ffn(xc, mods, 6, bidx_c, norm_g[i, 2], w_gu, w_down, i, 1, tm_c)

    return xl.reshape(batch, seq, d)
```

```python
import functools

import numpy as np
import jax
import jax.numpy as jnp
from jax import lax
from jax.experimental import pallas as pl
from jax.experimental.pallas import tpu as pltpu

F32 = jnp.float32
BF16 = jnp.bfloat16
EPS = 1e-6
GRID_W = 64
N_MOD = 9
DN_HEAD_DIM = 128
DN_CHUNK = 64
DN_CONV_K = 5
SG_CHUNK = 128
SG_GROUPS = 16

V7X_LANES = 128
V7X_SUBLANES = 8
V7X_VMEM_BYTES = 64 * 1024 * 1024
VMEM_LIMIT = V7X_VMEM_BYTES - 8 * 1024 * 1024
MOD_ROWS = 16
HIGHEST = lax.Precision.HIGHEST


def _cparams(*sem):
    return pltpu.CompilerParams(dimension_semantics=sem, vmem_limit_bytes=VMEM_LIMIT)


def _dot(a, b):
    return jnp.dot(a, b, preferred_element_type=F32)


def _dot_nt(a, b):
    return lax.dot_general(a, b, (((1,), (1,)), ((), ())), preferred_element_type=F32)


def _dot_tn(a, b):
    return lax.dot_general(a, b, (((0,), (0,)), ((), ())), preferred_element_type=F32)


def _rms(x):
    return x * lax.rsqrt(jnp.mean(x * x, axis=-1, keepdims=True) + EPS)


def _load_rows(ref, ncol):
    if not ncol:
        return ref[...]
    width = ref.shape[1] // ncol
    return jnp.concatenate([ref[:, k * width:(k + 1) * width] for k in range(ncol)], axis=0)


def _store_rows(ref, val, ncol):
    if not ncol:
        ref[...] = val
        return
    width = ref.shape[1] // ncol
    rows = ref.shape[0]
    for k in range(ncol):
        ref[:, k * width:(k + 1) * width] = val[k * rows:(k + 1) * rows, :]


def _row_view(arr, tm, seq, ncol):
    width = arr.shape[1]
    if not ncol:
        return arr, pl.BlockSpec((tm, width), lambda i, *_: (i, 0))
    grid_rows = seq // GRID_W
    per_batch = GRID_W // ncol
    view = arr.reshape(arr.shape[0] // seq, grid_rows, GRID_W * width)
    return view, pl.BlockSpec((None, grid_rows, ncol * width), lambda i, *_: (i // per_batch, 0, i % per_batch))


def _mod_spec(k, d, bidx):
    return pl.BlockSpec((None, 1, d), lambda i, *_: (bidx(i), 0, k))


def _mod_kernel(c_ref, w_ref, b_ref, o_ref):
    s = jax.nn.silu(c_ref[...]).astype(BF16)
    o_ref[...] = _dot(s, w_ref[...].astype(BF16)) + b_ref[...]


def _modulation(cs, mod_w, mod_b):
    depth, d, nm = mod_w.shape
    tn = 1024
    return pl.pallas_call(
        _mod_kernel,
        grid=(depth, nm // tn),
        in_specs=[pl.BlockSpec((MOD_ROWS, d), lambda l, j: (0, 0)),
                  pl.BlockSpec((None, d, tn), lambda l, j: (l, 0, j)),
                  pl.BlockSpec((None, 1, tn), lambda l, j: (l, 0, j))],
        out_specs=pl.BlockSpec((None, MOD_ROWS, tn), lambda l, j: (l, 0, j)),
        out_shape=jax.ShapeDtypeStruct((depth, MOD_ROWS, nm), F32),
        compiler_params=_cparams("parallel", "parallel"),
        name="modulation",
    )(cs, mod_w, mod_b.reshape(depth, 1, nm))


_FFN_CHUNK = 512


def _ffn_kernel(x_ref, ng_ref, sh_ref, sc_ref, gt_ref, wg_ref, wu_ref, wd_ref, *rest, final_norm):
    if final_norm:
        fg_ref, o_ref, xn_ref = rest
    else:
        o_ref, xn_ref = rest
    j = pl.program_id(1)

    @pl.when(j == 0)
    def _():
        gain = ng_ref[...] * (1.0 + sc_ref[...])
        xn_ref[...] = (_rms(x_ref[...]) * gain + sh_ref[...]).astype(BF16)
        o_ref[...] = jnp.zeros(o_ref.shape, F32)

    xn = xn_ref[...]
    g = _dot(xn, wg_ref[...])
    u = _dot(xn, wu_ref[...])
    h = (jax.nn.silu(g) * u).astype(BF16)
    o_ref[...] += _dot(h, wd_ref[...])

    @pl.when(j == pl.num_programs(1) - 1)
    def _():
        r = x_ref[...] + 0.5 * gt_ref[...] * o_ref[...]
        if final_norm:
            r = _rms(r) * fg_ref[...]
        o_ref[...] = r


def _half_ffn(x, mods, k0, bidx, norm_g, w_gu, w_down, layer, half, tm, final_g=None):
    rows, d = x.shape
    fc = w_gu.shape[4]
    nf = w_gu.shape[2] // 2
    row = pl.BlockSpec((tm, d), lambda i, j: (i, 0))
    vec = pl.BlockSpec((1, d), lambda i, j: (0, 0))
    in_specs = [row, vec, _mod_spec(k0, d, bidx), _mod_spec(k0 + 1, d, bidx), _mod_spec(k0 + 2, d, bidx),
                pl.BlockSpec((None, None, None, d, fc), lambda i, j: (layer, half, j, 0, 0)),
                pl.BlockSpec((None, None, None, d, fc), lambda i, j: (layer, half, nf + j, 0, 0)),
                pl.BlockSpec((None, None, fc, d), lambda i, j: (layer, half, j, 0))]
    args = [x, norm_g.reshape(1, d), mods, mods, mods, w_gu, w_gu, w_down]
    if final_g is not None:
        in_specs.append(vec)
        args.append(final_g.reshape(1, d))
    return pl.pallas_call(
        functools.partial(_ffn_kernel, final_norm=final_g is not None),
        grid=(rows // tm, nf),
        in_specs=in_specs,
        out_specs=row,
        out_shape=jax.ShapeDtypeStruct((rows, d), F32),
        scratch_shapes=[pltpu.VMEM((tm, d), BF16)],
        compiler_params=_cparams("parallel", "arbitrary"),
        name="half_ffn",
    )(*args)


def _prenorm_linear_kernel(x_ref, ng_ref, sh_ref, sc_ref, w_ref, *rest, ncol, act, has_extra):
    if has_extra:
        wx_ref, o_ref, ox_ref, xn_ref = rest
    else:
        o_ref, xn_ref = rest

    @pl.when(pl.program_id(1) == 0)
    def _():
        gain = ng_ref[...] * (1.0 + sc_ref[...])
        xn = (_rms(_load_rows(x_ref, ncol)) * gain + sh_ref[...]).astype(BF16)
        xn_ref[...] = xn
        if has_extra:
            ox_ref[...] = _dot(xn, wx_ref[...])

    y = _dot(xn_ref[...], w_ref[...])
    if act == "gelu":
        y = 0.5 * y * (1.0 + lax.erf(y * (0.5 ** 0.5)))
    o_ref[...] = y.astype(o_ref.dtype)


def _prenorm_linear(x, mods, k0, bidx, norm_g, w, tm, seq, ncol, act=None, w_extra=None, tn=2048):
    rows, d = x.shape
    n = w.shape[1]
    assert n % tn == 0
    xv, xspec = _row_view(x, tm, seq, ncol)
    vec = pl.BlockSpec((1, d), lambda i, j: (0, 0))
    in_specs = [xspec, vec, _mod_spec(k0, d, bidx), _mod_spec(k0 + 1, d, bidx),
                pl.BlockSpec((d, tn), lambda i, j: (0, j))]
    args = [xv, norm_g.reshape(1, d), mods, mods, w]
    out_specs = [pl.BlockSpec((tm, tn), lambda i, j: (i, j))]
    out_shape = [jax.ShapeDtypeStruct((rows, n), F32)]
    if w_extra is not None:
        nx = w_extra.shape[1]
        in_specs.append(pl.BlockSpec((d, nx), lambda i, j: (0, 0)))
        args.append(w_extra)
        out_specs.append(pl.BlockSpec((tm, nx), lambda i, j: (i, 0)))
        out_shape.append(jax.ShapeDtypeStruct((rows, nx), F32))
    outs = pl.pallas_call(
        functools.partial(_prenorm_linear_kernel, ncol=ncol, act=act, has_extra=w_extra is not None),
        grid=(rows // tm, n // tn),
        in_specs=in_specs,
        out_specs=out_specs,
        out_shape=out_shape,
        scratch_shapes=[pltpu.VMEM((tm, d), BF16)],
        compiler_params=_cparams("parallel", "arbitrary"),
        name="mixer_in_proj",
    )(*args)
    return outs if w_extra is not None else outs[0]


_CONV_PAD = V7X_SUBLANES


def _dn_conv_kernel(p_ref, cw_ref, o_ref, xp_ref, *, seq, sub, n_q, n_qk):
    c = pl.program_id(1)
    cb = p_ref.shape[1]
    half = DN_CONV_K // 2
    xp_ref[0:_CONV_PAD, :] = jnp.zeros((_CONV_PAD, cb), F32)
    xp_ref[_CONV_PAD:_CONV_PAD + seq, :] = p_ref[...]
    xp_ref[_CONV_PAD + seq:, :] = jnp.zeros((_CONV_PAD, cb), F32)
    w = cw_ref[...]

    def conv_tile(i):
        r0 = pl.multiple_of(i * sub, sub)
        win = xp_ref[pl.ds(r0, sub + 2 * _CONV_PAD), :]
        acc = None
        for t in range(DN_CONV_K):
            lo = _CONV_PAD - half + t
            term = win[lo:lo + sub, :] * w[t:t + 1, :]
            acc = term if acc is None else acc + term
        return r0, jax.nn.silu(acc)

    @pl.when(c < n_qk)
    def _():
        scale = jnp.where(c < n_q, DN_HEAD_DIM ** -0.5, 1.0).astype(F32)

        def body(i, carry):
            r0, y = conv_tile(i)
            for hh in range(cb // DN_HEAD_DIM):
                seg = y[:, hh * DN_HEAD_DIM:(hh + 1) * DN_HEAD_DIM]
                seg = seg * lax.rsqrt(jnp.sum(seg * seg, axis=-1, keepdims=True) + EPS)
                o_ref[pl.ds(r0, sub), hh * DN_HEAD_DIM:(hh + 1) * DN_HEAD_DIM] = (seg * scale).astype(o_ref.dtype)
            return carry

        lax.fori_loop(0, seq // sub, body, 0)

    @pl.when(c >= n_qk)
    def _():
        def body(i, carry):
            r0, y = conv_tile(i)
            o_ref[pl.ds(r0, sub), :] = y.astype(o_ref.dtype)
            return carry

        lax.fori_loop(0, seq // sub, body, 0)


def _dn_conv(p, conv_w, seq, qk_dim, conv_dim):
    rows = p.shape[0]
    cb = 256 if seq > 1024 else 1024
    sub = min(seq, 256)
    return pl.pallas_call(
        functools.partial(_dn_conv_kernel, seq=seq, sub=sub, n_q=qk_dim // cb, n_qk=2 * qk_dim // cb),
        grid=(rows // seq, conv_dim // cb),
        in_specs=[pl.BlockSpec((seq, cb), lambda s, c: (s, c)),
                  pl.BlockSpec((DN_CONV_K, cb), lambda s, c: (0, c))],
        out_specs=pl.BlockSpec((seq, cb), lambda s, c: (s, c)),
        out_shape=jax.ShapeDtypeStruct((rows, conv_dim), BF16),
        scratch_shapes=[pltpu.VMEM((seq + 2 * _CONV_PAD, cb), F32)],
        compiler_params=_cparams("parallel", "parallel"),
        name="dn_conv",
    )(p, conv_w)


def _dn_gate_kernel(ab_ref, alog_ref, dtb_ref, pm_ref, gc_ref, gr_ref, *, hv):
    tm, width = ab_ref.shape
    tile = V7X_LANES
    x = ab_ref[...]
    lane_full = lax.broadcasted_iota(jnp.int32, (tm, width), 1)
    lane = lax.broadcasted_iota(jnp.int32, (tile, width), 1)
    is_a = (lane % (2 * hv)) < hv
    is_rev = lane >= 2 * hv
    g = -jnp.exp(alog_ref[...]) * jax.nn.softplus(x + dtb_ref[...])
    raw = jnp.where((lane_full % (2 * hv)) < hv, g, jax.nn.sigmoid(x))
    ri = lax.broadcasted_iota(jnp.int32, (tile, tile), 0)
    ci = lax.broadcasted_iota(jnp.int32, (tile, tile), 1)
    same = (ri // DN_CHUNK) == (ci // DN_CHUNK)
    low = jnp.where(same & (ci <= ri), 1.0, 0.0).astype(F32)
    upp = jnp.where(same & (ci >= ri), 1.0, 0.0).astype(F32)
    for t in range(tm // tile):
        blk = raw[t * tile:(t + 1) * tile, :]
        pre = jnp.dot(low, blk, precision=HIGHEST, preferred_element_type=F32)
        suf = jnp.dot(upp, blk, precision=HIGHEST, preferred_element_type=F32)
        out = jnp.where(is_a, jnp.where(is_rev, suf, pre), blk)
        gc_ref[t * tile:(t + 1) * tile, :] = out
        gr_ref[t] = lax.dot_general(pm_ref[...], out, (((1,), (1,)), ((), ())),
                                    precision=HIGHEST, preferred_element_type=F32)


def _gate_perm(hv):
    pm = np.zeros((4 * hv, 4 * hv), np.float32)
    for hq in range(hv // 2):
        for d in range(2):
            for isb in range(2):
                for j in range(2):
                    pm[hq * 8 + d * 4 + isb * 2 + j, d * 2 * hv + isb * hv + 2 * hq + j] = 1.0
    return jnp.asarray(pm)


def _dn_gates(ab, a_log, dt_bias, hv, tm):
    rows, width = ab.shape
    assert width == 4 * hv == V7X_LANES
    zeros = jnp.zeros_like(a_log)
    alog = jnp.concatenate([a_log, zeros], axis=1).reshape(1, width)
    dtb = jnp.concatenate([dt_bias, zeros], axis=1).reshape(1, width)
    vec = pl.BlockSpec((1, width), lambda i: (0, 0))
    return pl.pallas_call(
        functools.partial(_dn_gate_kernel, hv=hv),
        grid=(rows // tm,),
        in_specs=[pl.BlockSpec((tm, width), lambda i: (i, 0)), vec, vec,
                  pl.BlockSpec((width, width), lambda i: (0, 0))],
        out_specs=[pl.BlockSpec((tm, width), lambda i: (i, 0)),
                   pl.BlockSpec((tm // V7X_LANES, width, V7X_LANES), lambda i: (i, 0, 0))],
        out_shape=[jax.ShapeDtypeStruct((rows, width), F32),
                   jax.ShapeDtypeStruct((rows // V7X_LANES, width, V7X_LANES), F32)],
        compiler_params=_cparams("parallel"),
        name="dn_gates",
    )(ab, alog, dtb, _gate_perm(hv))


_INV_BLOCK = 4
_DN_GROUP_TILES = 2
_DN_TRIP_GROUPS = 4
_DN_HEADS_PER_STEP = 1


def _mm_bf16(ps, qs):
    return [_dot(p.astype(BF16), q.astype(BF16)) for p, q in zip(ps, qs)]


def _pair_diag(x, lo_half):
    zero = jnp.zeros_like(x)
    return jnp.concatenate([jnp.where(lo_half, x, zero), jnp.where(lo_half, zero, x)], axis=0)


def _unit_tri_inverse(mats, ri, ci, lo_half):
    eye = (ri == ci).astype(F32)
    bi, bj = ri // _INV_BLOCK, ci // _INV_BLOCK
    n = mats[0].shape[0]

    def mm(ps, qs):
        return _mm_bf16(ps, [_pair_diag(q, lo_half) for q in qs])

    ds = [jnp.where(bi == bj, a, 0.0) for a in mats]
    ts = [eye - d for d in ds]
    ps = mm(ds, ds)
    yield
    power = 2
    while 2 * power < _INV_BLOCK:
        both = mm([jnp.concatenate([t, p], axis=0) for t, p in zip(ts, ps)], ps)
        ts = [t + b[:n, :] for t, b in zip(ts, both)]
        ps = [b[n:, :] for b in both]
        power *= 2
        yield
    ts = [t + tp for t, tp in zip(ts, mm(ts, ps))]
    yield
    w = 1
    while w * _INV_BLOCK < n:
        off = (bi // (2 * w) == bj // (2 * w)) & (bi // w != bj // w)
        tes = mm(ts, [jnp.where(off, a, 0.0) for a in mats])
        yield
        ts = [t - tet for t, tet in zip(ts, mm(tes, ts))]
        yield
        w *= 2
    return ts


def _issue_pipelined(n_groups, hop, make_head, make_tail):
    heads, values = {}, {}
    tail, tail_g, tick = None, 0, 0
    while tail_g < n_groups:
        if tick % hop == 0 and tick // hop < n_groups:
            heads[tick // hop] = make_head(tick // hop)
        for g in sorted(heads):
            try:
                next(heads[g])
            except StopIteration as stop:
                values[g] = stop.value
                del heads[g]
        while tail_g < n_groups:
            if tail is None:
                if tail_g not in values:
                    break
                tail = make_tail(values.pop(tail_g))
            try:
                next(tail)
                break
            except StopIteration:
                tail, tail_g = None, tail_g + 1
        tick += 1


def _dn_core_kernel(qc_ref, kc_ref, vc_ref, gcc_ref, grc_ref, ql_ref, kl_ref, vl_ref, gcl_ref, grl_ref,
                    oc_ref, ol_ref, s_ref, *, hv):
    dk = DN_HEAD_DIM
    cs = DN_CHUNK
    tile = V7X_LANES
    n_h = _DN_HEADS_PER_STEP
    n_streams = 4 * n_h
    s_ref[...] = jnp.zeros(s_ref.shape, F32)
    oc_ref[...] = jnp.zeros(oc_ref.shape, F32)
    ol_ref[...] = jnp.zeros(ol_ref.shape, F32)
    ri = lax.broadcasted_iota(jnp.int32, (cs, 2 * cs), 0)
    lane = lax.broadcasted_iota(jnp.int32, (cs, 2 * cs), 1)
    shifts = [(tile - 2 * (n_h * pl.program_id(1) + hh)) % tile for hh in range(n_h)]

    n_sub = tile // cs
    lo_half = lane < cs
    ci = jnp.where(lo_half, lane, lane - cs)
    incl = (ri >= ci, ri <= ci)
    strict = (ri > ci, ri < ci)
    last = (cs - 1, 0)

    def state_free(refs, t_fwd, t_bwd):
        q_ref, k_ref, v_ref, gc_ref, gr_ref, _ = refs
        chunks = ([(0, t, c) for t in t_fwd for c in range(n_sub)]
                  + [(1, t, c) for t in t_bwd for c in reversed(range(n_sub))])
        r0s = [pl.multiple_of(t * tile, tile) + c * cs for _, t, c in chunks]
        cells = [(ic, hh) for ic in range(len(chunks)) for hh in range(n_h)]
        cdir = [chunks[ic][0] for ic, _ in cells]
        kbs = [k_ref[pl.ds(r0s[ic], cs), hh * dk:(hh + 1) * dk] for ic, hh in cells]
        qbs = [q_ref[pl.ds(r0s[ic], cs), hh * dk:(hh + 1) * dk] for ic, hh in cells]
        qkks = [_dot_nt(jnp.concatenate([qb, kb], axis=0), jnp.concatenate([kb, kb], axis=0))
                for qb, kb in zip(qbs, kbs)]
        gcts = [pltpu.roll(gc_ref[pl.ds(r0s[ic], cs), :], shifts[hh], 1) for ic, hh in cells]
        grts = [gr_ref[chunks[ic][1], hh * 8:(hh + 1) * 8, :] for ic, hh in cells]
        yield

        def packed_cols(m, off):
            base = cdir[m] * 2 * hv + off
            return jnp.where(lo_half, gcts[m][:, base:base + 1], gcts[m][:, base + 1:base + 2])

        def packed_row(m, off):
            d, _, c = chunks[cells[m][0]]
            r0_, r1_ = grts[m][d * 4 + off:d * 4 + off + 1, :], grts[m][d * 4 + off + 1:d * 4 + off + 2, :]
            if c == 0:
                return jnp.where(lo_half[:1], r0_, pltpu.roll(r1_, cs, 1))
            return jnp.where(lo_half[:1], pltpu.roll(r0_, cs, 1), r1_)

        n_cells = len(cells)
        dec2s = [jnp.exp(jnp.where(incl[cdir[m]], packed_cols(m, 0) - packed_row(m, 0), -jnp.inf))
                 for m in range(n_cells)]
        amats = [jnp.where(strict[cdir[m]], qkks[m][cs:, :] * dec2s[m] * packed_cols(m, hv), 0.0)
                 for m in range(n_cells)]
        qkms = [(qkks[m][:cs, :] * dec2s[m]).astype(BF16) for m in range(n_cells)]
        yield
        tinvs = yield from _unit_tri_inverse(amats, ri, ci, lo_half)
        tinvs = [t.astype(BF16) for t in tinvs]

        scs = [(m, j) for m in range(n_cells) for j in range(2)]
        dirs = [cdir[m] for m, _ in scs]
        gcols = [gcts[m][:, cdir[m] * 2 * hv + j:cdir[m] * 2 * hv + j + 1] for m, j in scs]
        bcols = [gcts[m][:, cdir[m] * 2 * hv + hv + j:cdir[m] * 2 * hv + hv + j + 1] for m, j in scs]
        vcols = [(cells[m][1] * 2 + j) * dk for m, j in scs]
        rows = [r0s[cells[m][0]] for m, _ in scs]
        ks = [kb.astype(F32) for kb in kbs]

        def head_rows(x, j):
            zero = jnp.zeros_like(x)
            return jnp.concatenate([x, zero] if j == 0 else [zero, x], axis=0)

        rhs = [jnp.concatenate([ks[m] * (bc * jnp.exp(gc)), v_ref[pl.ds(r0, cs), vc:vc + dk] * bc],
                               axis=1).astype(BF16)
               for (m, _), bc, gc, r0, vc in zip(scs, bcols, gcols, rows, vcols)]
        wus = [_dot(tinvs[m], head_rows(r, j)).astype(BF16) for (m, j), r in zip(scs, rhs)]
        yield
        qwus = [_dot(qkms[m], head_rows(wu, j)) for (m, j), wu in zip(scs, wus)]
        g_lasts = [gc[last[d]:last[d] + 1, :] for d, gc in zip(dirs, gcols)]
        kwus = [_dot_tn((ks[m] * jnp.exp(gl - gc)).astype(BF16), wu)
                for (m, _), gl, gc, wu in zip(scs, g_lasts, gcols, wus)]
        yield
        lhs = [jnp.concatenate([kwu[:, :dk].astype(BF16), (qbs[m] * jnp.exp(gc) - qwu[:, :dk]).astype(BF16)], axis=0)
               for (m, _), gc, qwu, kwu in zip(scs, gcols, qwus, kwus)]
        yield
        return dict(rows=list(zip(rows, vcols)), lhs=lhs, add=[kwu[:, dk:] for kwu in kwus],
                    intra=[qwu[:, dk:] for qwu in qwus], decay=[jnp.exp(gl) for gl in g_lasts])

    def recurrence(refs, pre, states):
        o_ref = refs[5]
        n_steps = len(pre["lhs"]) // n_streams
        for step in range(n_steps):
            idx = [((d * n_steps + step) * n_h + hh) * 2 + j for d in range(2) for hh in range(n_h) for j in range(2)]
            rs = [_dot(pre["lhs"][i], st.astype(BF16)) for i, st in zip(idx, states)]
            states[:] = [st * pre["decay"][i] + pre["add"][i] - r[:dk, :] for i, st, r in zip(idx, states, rs)]
            for i, r in zip(idx, rs):
                r0, vc = pre["rows"][i]
                o_ref[pl.ds(r0, cs), vc:vc + dk] += r[dk:, :] + pre["intra"][i]
            yield

    def run(refs, n_tiles):
        group = max(p for p in range(1, _DN_GROUP_TILES + 1) if n_tiles % p == 0)
        n_groups = n_tiles // group
        per_trip = max(p for p in range(1, _DN_TRIP_GROUPS + 1) if n_groups % p == 0)
        hop = group * n_sub

        def tiles(g):
            return [g * group + m for m in range(group)], [n_tiles - 1 - g * group - m for m in range(group)]

        def body(i, carry):
            states = [s_ref[si] for si in range(n_streams)]
            _issue_pipelined(per_trip, hop,
                             lambda g: state_free(refs, *tiles(i * per_trip + g)),
                             lambda pre: recurrence(refs, pre, states))
            for si in range(n_streams):
                s_ref[si] = states[si]
            return carry

        lax.fori_loop(0, n_groups // per_trip, body, 0)

    run((qc_ref, kc_ref, vc_ref, gcc_ref, grc_ref, oc_ref), qc_ref.shape[0] // tile)
    run((ql_ref, kl_ref, vl_ref, gcl_ref, grl_ref, ol_ref), ql_ref.shape[0] // tile)


def _dn_core(qkv_c, gc_c, gr_c, qkv_l, gc_l, gr_l, batch, hv):
    dk = DN_HEAD_DIM
    n_h = _DN_HEADS_PER_STEP
    steps = hv // 2 // n_h
    ctx_len = qkv_c.shape[0] // batch
    seq = qkv_l.shape[0] // batch
    width = gc_l.shape[1]

    def specs(t):
        return [pl.BlockSpec((t, n_h * dk), lambda b, h: (b, h)),
                pl.BlockSpec((t, n_h * dk), lambda b, h: (b, steps + h)),
                pl.BlockSpec((t, 2 * n_h * dk), lambda b, h: (b, steps + h)),
                pl.BlockSpec((t, width), lambda b, h: (b, 0)),
                pl.BlockSpec((t // V7X_LANES, 8 * n_h, V7X_LANES), lambda b, h: (b, h, 0))]

    return pl.pallas_call(
        functools.partial(_dn_core_kernel, hv=hv),
        grid=(batch, steps),
        in_specs=specs(ctx_len) + specs(seq),
        out_specs=[pl.BlockSpec((ctx_len, 2 * n_h * dk), lambda b, h: (b, h)),
                   pl.BlockSpec((seq, 2 * n_h * dk), lambda b, h: (b, h))],
        out_shape=[jax.ShapeDtypeStruct((batch * ctx_len, hv * dk), F32),
                   jax.ShapeDtypeStruct((batch * seq, hv * dk), F32)],
        scratch_shapes=[pltpu.VMEM((4 * n_h, dk, dk), F32)],
        compiler_params=_cparams("parallel", "parallel"),
        name="dn_core",
    )(qkv_c, qkv_c, qkv_c, gc_c, gr_c, qkv_l, qkv_l, qkv_l, gc_l, gr_l)


def _dn_out_kernel(o_ref, z_ref, ng_ref, w_ref, x_ref, gt_ref, out_ref, acc_ref, *, ncol):
    k = pl.program_id(1)

    @pl.when(k == 0)
    def _():
        acc_ref[...] = jnp.zeros(acc_ref.shape, F32)

    o = o_ref[...]
    dk = DN_HEAD_DIM
    segs = [_rms(o[:, h * dk:(h + 1) * dk]) * ng_ref[...] for h in range(o.shape[1] // dk)]
    on = jnp.concatenate(segs, axis=1) * jax.nn.silu(z_ref[...])
    acc_ref[...] += _dot(on.astype(BF16), w_ref[...])

    @pl.when(k == pl.num_programs(1) - 1)
    def _():
        _store_rows(out_ref, _load_rows(x_ref, ncol) + gt_ref[...] * acc_ref[...], ncol)


def _dn_out(o, p, z_col0, norm_g, w_out, x, mods, bidx, tm, seq, ncol):
    rows, d = x.shape
    kdim = w_out.shape[0]
    tk = 1024
    xv, xspec = _row_view(x, tm, seq, ncol)
    zb = z_col0 // tk
    out = pl.pallas_call(
        functools.partial(_dn_out_kernel, ncol=ncol),
        grid=(rows // tm, kdim // tk),
        in_specs=[pl.BlockSpec((tm, tk), lambda i, k: (i, k)),
                  pl.BlockSpec((tm, tk), lambda i, k: (i, zb + k)),
                  pl.BlockSpec((1, DN_HEAD_DIM), lambda i, k: (0, 0)),
                  pl.BlockSpec((tk, d), lambda i, k: (k, 0)),
                  xspec, _mod_spec(5, d, bidx)],
        out_specs=xspec,
        out_shape=jax.ShapeDtypeStruct(xv.shape, F32),
        scratch_shapes=[pltpu.VMEM((tm, d), F32)],
        compiler_params=_cparams("parallel", "arbitrary"),
        name="dn_out",
    )(o, p, norm_g.reshape(1, DN_HEAD_DIM), w_out, xv, mods)
    return out.reshape(rows, d)


def _sg_out_kernel(u_ref, v_ref, lg_ref, lb_ref, ws_ref, bs_ref, w_ref, x_ref, gt_ref, out_ref, m_ref, *, ncol):
    v = v_ref[...]
    mu = jnp.mean(v, axis=-1, keepdims=True)
    var = jnp.mean(jnp.square(v - mu), axis=-1, keepdims=True)
    m_ref[...] = ((v - mu) * lax.rsqrt(var + EPS) * lg_ref[...] + lb_ref[...]).astype(BF16)
    tm, sg_dim = v.shape
    gd = sg_dim // SG_GROUPS
    for g in range(SG_GROUPS):
        wsg = ws_ref[g]
        bias = bs_ref[:, g:g + 1]
        for c in range(tm // SG_CHUNK):
            rs = slice(c * SG_CHUNK, (c + 1) * SG_CHUNK)
            cs = slice(g * gd, (g + 1) * gd)
            mixed = _dot(wsg, m_ref[rs, cs]) + bias
            m_ref[rs, cs] = (u_ref[rs, cs] * mixed).astype(BF16)
    y = _dot(m_ref[...], w_ref[...])
    _store_rows(out_ref, _load_rows(x_ref, ncol) + gt_ref[...] * y, ncol)


def _sg_out(uv, ln_g, ln_b, w_s, b_s, w_out, x, mods, bidx, tm, seq, ncol):
    rows, d = x.shape
    sg_dim = w_out.shape[0]
    xv, xspec = _row_view(x, tm, seq, ncol)
    vec = pl.BlockSpec((1, sg_dim), lambda i: (0, 0))
    out = pl.pallas_call(
        functools.partial(_sg_out_kernel, ncol=ncol),
        grid=(rows // tm,),
        in_specs=[pl.BlockSpec((tm, sg_dim), lambda i: (i, 0)),
                  pl.BlockSpec((tm, sg_dim), lambda i: (i, 1)),
                  vec, vec,
                  pl.BlockSpec((SG_GROUPS, SG_CHUNK, SG_CHUNK), lambda i: (0, 0, 0)),
                  pl.BlockSpec((SG_CHUNK, SG_GROUPS), lambda i: (0, 0)),
                  pl.BlockSpec((sg_dim, d), lambda i: (0, 0), pipeline_mode=pl.Buffered(1)),
                  xspec, _mod_spec(5, d, bidx)],
        out_specs=xspec,
        out_shape=jax.ShapeDtypeStruct(xv.shape, F32),
        scratch_shapes=[pltpu.VMEM((tm, sg_dim), BF16)],
        compiler_params=_cparams("parallel"),
        name="sg_out",
    )(uv, uv, ln_g.reshape(1, sg_dim), ln_b.reshape(1, sg_dim), w_s, b_s.T, w_out, xv, mods)
    return out.reshape(rows, d)


def kernel(x, c, ctx, c_ctx, mod_w, mod_b, norm_g, ffn_w_gu, ffn_w_down, dn_w_in, dn_conv_w, dn_a_log, dn_dt_bias,
           dn_norm_g, dn_w_out, sg_w_in, sg_ln_g, sg_ln_b, sg_w_s, sg_b_s, sg_w_out, final_norm_g):
    batch, seq, d = x.shape
    ctx_len = ctx.shape[1]
    depth = mod_w.shape[0]
    nm = mod_w.shape[2]
    hv = dn_a_log.shape[2]
    dk = DN_HEAD_DIM
    qk_dim = hv // 2 * dk
    v_dim = hv * dk
    conv_dim = 2 * qk_dim + v_dim
    assert batch + 1 <= MOD_ROWS

    xl = x.reshape(batch * seq, d)
    xc = ctx.reshape(batch * ctx_len, d)
    tm_l = min(512, seq)
    tm_c = min(512, batch * ctx_len)
    tiles_per_seq = seq // tm_l
    bidx_l = lambda i: i // tiles_per_seq
    bidx_c = lambda i: batch
    ncol_cm = tm_l // (seq // GRID_W)
    tm_f = tm_l
    bidx_f = lambda i, n=seq // tm_f: i // n

    cs = jnp.concatenate([c, c_ctx[None, :], jnp.zeros((MOD_ROWS - batch - 1, d), F32)], axis=0)
    mods_all = _modulation(cs, mod_w, mod_b).reshape(depth, MOD_ROWS, 1, nm)

    w_gu = ffn_w_gu.astype(BF16).reshape(depth, 2, d, -1, _FFN_CHUNK).transpose(0, 1, 3, 2, 4)
    w_down = ffn_w_down.astype(BF16)
    n_mix = 2

    for i in range(depth):
        last = i == depth - 1
        kind = i % n_mix
        j = i // n_mix
        ncol = ncol_cm if (j % 2) == 1 else 0
        need_ctx = not (last and kind == 1)
        mods = mods_all[i]

        xl = _half_ffn(xl, mods, 0, bidx_f, norm_g[i, 0], w_gu, w_down, i, 0, tm_f)
        if need_ctx:
            xc = _half_ffn(xc, mods, 0, bidx_c, norm_g[i, 0], w_gu, w_down, i, 0, tm_c)

        if kind == 0:
            w_in = dn_w_in[j].astype(BF16)
            w_main, w_ab = w_in[:, :conv_dim + v_dim], w_in[:, conv_dim + v_dim:]
            w_out = dn_w_out[j].astype(BF16)
            pl_, ab_l = _prenorm_linear(xl, mods, 3, bidx_l, norm_g[i, 1], w_main, tm_l, seq, ncol, w_extra=w_ab)
            pc_, ab_c = _prenorm_linear(xc, mods, 3, bidx_c, norm_g[i, 1], w_main, tm_c, ctx_len, 0, w_extra=w_ab)
            qkv_l = _dn_conv(pl_, dn_conv_w[j], seq, qk_dim, conv_dim)
            qkv_c = _dn_conv(pc_, dn_conv_w[j], ctx_len, qk_dim, conv_dim)
            gc_l, gr_l = _dn_gates(ab_l, dn_a_log[j], dn_dt_bias[j], hv, tm_l)
            gc_c, gr_c = _dn_gates(ab_c, dn_a_log[j], dn_dt_bias[j], hv, tm_c)
            o_c, o_l = _dn_core(qkv_c, gc_c, gr_c, qkv_l, gc_l, gr_l, batch, hv)
            xl = _dn_out(o_l, pl_, conv_dim, dn_norm_g[j], w_out, xl, mods, bidx_l, tm_l, seq, ncol)
            if not last:
                xc = _dn_out(o_c, pc_, conv_dim, dn_norm_g[j], w_out, xc, mods, bidx_c, tm_c, ctx_len, 0)
        else:
            w_in = sg_w_in[j].astype(BF16)
            w_out = sg_w_out[j].astype(BF16)
            w_s = sg_w_s[j].astype(BF16)
            tm_sl = min(256, seq)
            tm_sc = min(256, batch * ctx_len)
            ncol_s = (tm_sl // (seq // GRID_W)) if ncol else 0
            bidx_sl = lambda i, n=seq // tm_sl: i // n
            uv_l = _prenorm_linear(xl, mods, 3, bidx_l, norm_g[i, 1], w_in, tm_l, seq, ncol, act="gelu")
            xl = _sg_out(uv_l, sg_ln_g[j], sg_ln_b[j], w_s, sg_b_s[j], w_out, xl, mods, bidx_sl, tm_sl, seq, ncol_s)
            if not last:
                uv_c = _prenorm_linear(xc, mods, 3, bidx_c, norm_g[i, 1], w_in, tm_c, ctx_len, 0, act="gelu")
                xc = _sg_out(uv_c, sg_ln_g[j], sg_ln_b[j], w_s, sg_b_s[j], w_out, xc, mods, bidx_c, tm_sc, ctx_len, 0)

        xl = _half_ffn(xl, mods, 6, bidx_f, norm_g[i, 2], w_gu, w_down, i, 1, tm_f,
                       final_g=final_norm_g if last else None)
        if not last:
            xc = _half_ffn(xc, mods, 6, bidx_c, norm_g[i, 2], w_gu, w_down, i, 1, tm_c)

    return xl.reshape(batch, seq, d)
```

```python
import functools

import numpy as np
import jax
import jax.numpy as jnp
from jax import lax
from jax.experimental import pallas as pl
from jax.experimental.pallas import tpu as pltpu

F32 = jnp.float32
BF16 = jnp.bfloat16
EPS = 1e-6
GRID_W = 64
N_MOD = 9
DN_HEAD_DIM = 128
DN_CHUNK = 64
DN_CONV_K = 5
SG_CHUNK = 128
SG_GROUPS = 16

V7X_LANES = 128
V7X_SUBLANES = 8
V7X_VMEM_BYTES = 64 * 1024 * 1024
VMEM_LIMIT = V7X_VMEM_BYTES - 8 * 1024 * 1024
MOD_ROWS = 16
HIGHEST = lax.Precision.HIGHEST


def _cparams(*sem):
    return pltpu.CompilerParams(dimension_semantics=sem, vmem_limit_bytes=VMEM_LIMIT)


def _dot(a, b):
    return jnp.dot(a, b, preferred_element_type=F32)


def _dot_nt(a, b):
    return lax.dot_general(a, b, (((1,), (1,)), ((), ())), preferred_element_type=F32)


def _dot_tn(a, b):
    return lax.dot_general(a, b, (((0,), (0,)), ((), ())), preferred_element_type=F32)


def _rms(x):
    return x * lax.rsqrt(jnp.mean(x * x, axis=-1, keepdims=True) + EPS)


def _load_rows(ref, ncol):
    if not ncol:
        return ref[...]
    width = ref.shape[1] // ncol
    return jnp.concatenate([ref[:, k * width:(k + 1) * width] for k in range(ncol)], axis=0)


def _store_rows(ref, val, ncol):
    if not ncol:
        ref[...] = val
        return
    width = ref.shape[1] // ncol
    rows = ref.shape[0]
    for k in range(ncol):
        ref[:, k * width:(k + 1) * width] = val[k * rows:(k + 1) * rows, :]


def _row_view(arr, tm, seq, ncol):
    width = arr.shape[1]
    if not ncol:
        return arr, pl.BlockSpec((tm, width), lambda i, *_: (i, 0))
    grid_rows = seq // GRID_W
    per_batch = GRID_W // ncol
    view = arr.reshape(arr.shape[0] // seq, grid_rows, GRID_W * width)
    return view, pl.BlockSpec((None, grid_rows, ncol * width), lambda i, *_: (i // per_batch, 0, i % per_batch))


def _mod_spec(k, d, bidx):
    return pl.BlockSpec((None, 1, d), lambda i, *_: (bidx(i), 0, k))


def _mod_kernel(c_ref, w_ref, b_ref, o_ref):
    s = jax.nn.silu(c_ref[...]).astype(BF16)
    o_ref[...] = _dot(s, w_ref[...].astype(BF16)) + b_ref[...]


def _modulation(cs, mod_w, mod_b):
    depth, d, nm = mod_w.shape
    tn = 1024
    return pl.pallas_call(
        _mod_kernel,
        grid=(depth, nm // tn),
        in_specs=[pl.BlockSpec((MOD_ROWS, d), lambda l, j: (0, 0)),
                  pl.BlockSpec((None, d, tn), lambda l, j: (l, 0, j)),
                  pl.BlockSpec((None, 1, tn), lambda l, j: (l, 0, j))],
        out_specs=pl.BlockSpec((None, MOD_ROWS, tn), lambda l, j: (l, 0, j)),
        out_shape=jax.ShapeDtypeStruct((depth, MOD_ROWS, nm), F32),
        compiler_params=_cparams("parallel", "parallel"),
        name="modulation",
    )(cs, mod_w, mod_b.reshape(depth, 1, nm))


_FFN_CHUNK = 512


def _ffn_kernel(x_ref, ng_ref, sh_ref, sc_ref, gt_ref, wg_ref, wu_ref, wd_ref, *rest, final_norm):
    if final_norm:
        fg_ref, o_ref, xn_ref = rest
    else:
        o_ref, xn_ref = rest
    j = pl.program_id(1)

    @pl.when(j == 0)
    def _():
        gain = ng_ref[...] * (1.0 + sc_ref[...])
        xn_ref[...] = (_rms(x_ref[...]) * gain + sh_ref[...]).astype(BF16)
        o_ref[...] = jnp.zeros(o_ref.shape, F32)

    xn = xn_ref[...]
    g = _dot(xn, wg_ref[...])
    u = _dot(xn, wu_ref[...])
    h = (jax.nn.silu(g) * u).astype(BF16)
    o_ref[...] += _dot(h, wd_ref[...])

    @pl.when(j == pl.num_programs(1) - 1)
    def _():
        r = x_ref[...] + 0.5 * gt_ref[...] * o_ref[...]
        if final_norm:
            r = _rms(r) * fg_ref[...]
        o_ref[...] = r


def _half_ffn(x, mods, k0, bidx, norm_g, w_gu, w_down, layer, half, tm, final_g=None, fc=_FFN_CHUNK):
    rows, d = x.shape
    nf = w_down.shape[2] // fc
    row = pl.BlockSpec((tm, d), lambda i, j: (i, 0))
    vec = pl.BlockSpec((1, d), lambda i, j: (0, 0))
    in_specs = [row, vec, _mod_spec(k0, d, bidx), _mod_spec(k0 + 1, d, bidx), _mod_spec(k0 + 2, d, bidx),
                pl.BlockSpec((None, None, d, fc), lambda i, j: (layer, half, 0, j)),
                pl.BlockSpec((None, None, d, fc), lambda i, j: (layer, half, 0, nf + j)),
                pl.BlockSpec((None, None, fc, d), lambda i, j: (layer, half, j, 0))]
    args = [x, norm_g.reshape(1, d), mods, mods, mods, w_gu, w_gu, w_down]
    if final_g is not None:
        in_specs.append(vec)
        args.append(final_g.reshape(1, d))
    return pl.pallas_call(
        functools.partial(_ffn_kernel, final_norm=final_g is not None),
        grid=(rows // tm, nf),
        in_specs=in_specs,
        out_specs=row,
        out_shape=jax.ShapeDtypeStruct((rows, d), F32),
        scratch_shapes=[pltpu.VMEM((tm, d), BF16)],
        compiler_params=_cparams("parallel", "arbitrary"),
        name="half_ffn",
    )(*args)


def _prenorm_linear_kernel(x_ref, ng_ref, sh_ref, sc_ref, w_ref, *rest, ncol, act, has_extra):
    if has_extra:
        wx_ref, o_ref, ox_ref, xn_ref = rest
    else:
        o_ref, xn_ref = rest

    @pl.when(pl.program_id(1) == 0)
    def _():
        gain = ng_ref[...] * (1.0 + sc_ref[...])
        xn = (_rms(_load_rows(x_ref, ncol)) * gain + sh_ref[...]).astype(BF16)
        xn_ref[...] = xn
        if has_extra:
            ox_ref[...] = _dot(xn, wx_ref[...])

    y = _dot(xn_ref[...], w_ref[...])
    if act == "gelu":
        y = 0.5 * y * (1.0 + lax.erf(y * (0.5 ** 0.5)))
    o_ref[...] = y.astype(o_ref.dtype)


def _prenorm_linear(x, mods, k0, bidx, norm_g, w, tm, seq, ncol, act=None, w_extra=None, tn=2048):
    rows, d = x.shape
    n = w.shape[1]
    assert n % tn == 0
    xv, xspec = _row_view(x, tm, seq, ncol)
    vec = pl.BlockSpec((1, d), lambda i, j: (0, 0))
    in_specs = [xspec, vec, _mod_spec(k0, d, bidx), _mod_spec(k0 + 1, d, bidx),
                pl.BlockSpec((d, tn), lambda i, j: (0, j))]
    args = [xv, norm_g.reshape(1, d), mods, mods, w]
    out_specs = [pl.BlockSpec((tm, tn), lambda i, j: (i, j))]
    out_shape = [jax.ShapeDtypeStruct((rows, n), F32)]
    if w_extra is not None:
        nx = w_extra.shape[1]
        in_specs.append(pl.BlockSpec((d, nx), lambda i, j: (0, 0)))
        args.append(w_extra)
        out_specs.append(pl.BlockSpec((tm, nx), lambda i, j: (i, 0)))
        out_shape.append(jax.ShapeDtypeStruct((rows, nx), F32))
    outs = pl.pallas_call(
        functools.partial(_prenorm_linear_kernel, ncol=ncol, act=act, has_extra=w_extra is not None),
        grid=(rows // tm, n // tn),
        in_specs=in_specs,
        out_specs=out_specs,
        out_shape=out_shape,
        scratch_shapes=[pltpu.VMEM((tm, d), BF16)],
        compiler_params=_cparams("parallel", "arbitrary"),
        name="mixer_in_proj",
    )(*args)
    return outs if w_extra is not None else outs[0]


_CONV_PAD = V7X_SUBLANES


def _dn_conv_kernel(p_ref, cw_ref, o_ref, xp_ref, *, seq, sub, n_q, n_qk):
    c = pl.program_id(1)
    cb = p_ref.shape[1]
    half = DN_CONV_K // 2
    xp_ref[0:_CONV_PAD, :] = jnp.zeros((_CONV_PAD, cb), F32)
    xp_ref[_CONV_PAD:_CONV_PAD + seq, :] = p_ref[...]
    xp_ref[_CONV_PAD + seq:, :] = jnp.zeros((_CONV_PAD, cb), F32)
    w = cw_ref[...]

    def conv_tile(i):
        r0 = pl.multiple_of(i * sub, sub)
        win = xp_ref[pl.ds(r0, sub + 2 * _CONV_PAD), :]
        acc = None
        for t in range(DN_CONV_K):
            lo = _CONV_PAD - half + t
            term = win[lo:lo + sub, :] * w[t:t + 1, :]
            acc = term if acc is None else acc + term
        return r0, jax.nn.silu(acc)

    @pl.when(c < n_qk)
    def _():
        scale = jnp.where(c < n_q, DN_HEAD_DIM ** -0.5, 1.0).astype(F32)

        def body(i, carry):
            r0, y = conv_tile(i)
            for hh in range(cb // DN_HEAD_DIM):
                seg = y[:, hh * DN_HEAD_DIM:(hh + 1) * DN_HEAD_DIM]
                seg = seg * lax.rsqrt(jnp.sum(seg * seg, axis=-1, keepdims=True) + EPS)
                o_ref[pl.ds(r0, sub), hh * DN_HEAD_DIM:(hh + 1) * DN_HEAD_DIM] = (seg * scale).astype(o_ref.dtype)
            return carry

        lax.fori_loop(0, seq // sub, body, 0)

    @pl.when(c >= n_qk)
    def _():
        def body(i, carry):
            r0, y = conv_tile(i)
            o_ref[pl.ds(r0, sub), :] = y.astype(o_ref.dtype)
            return carry

        lax.fori_loop(0, seq // sub, body, 0)


def _dn_conv(p, conv_w, seq, qk_dim, conv_dim, cb_long=256):
    rows = p.shape[0]
    cb = cb_long if seq > 1024 else 1024
    sub = min(seq, 256)
    return pl.pallas_call(
        functools.partial(_dn_conv_kernel, seq=seq, sub=sub, n_q=qk_dim // cb, n_qk=2 * qk_dim // cb),
        grid=(rows // seq, conv_dim // cb),
        in_specs=[pl.BlockSpec((seq, cb), lambda s, c: (s, c)),
                  pl.BlockSpec((DN_CONV_K, cb), lambda s, c: (0, c))],
        out_specs=pl.BlockSpec((seq, cb), lambda s, c: (s, c)),
        out_shape=jax.ShapeDtypeStruct((rows, conv_dim), BF16),
        scratch_shapes=[pltpu.VMEM((seq + 2 * _CONV_PAD, cb), F32)],
        compiler_params=_cparams("parallel", "parallel"),
        name="dn_conv",
    )(p, conv_w)


def _dn_gate_kernel(ab_ref, alog_ref, dtb_ref, pm_ref, gc_ref, gr_ref, *, hv):
    tm, width = ab_ref.shape
    tile = V7X_LANES
    x = ab_ref[...]
    lane_full = lax.broadcasted_iota(jnp.int32, (tm, width), 1)
    lane = lax.broadcasted_iota(jnp.int32, (tile, width), 1)
    is_a = (lane % (2 * hv)) < hv
    is_rev = lane >= 2 * hv
    g = -jnp.exp(alog_ref[...]) * jax.nn.softplus(x + dtb_ref[...])
    raw = jnp.where((lane_full % (2 * hv)) < hv, g, jax.nn.sigmoid(x))
    ri = lax.broadcasted_iota(jnp.int32, (tile, tile), 0)
    ci = lax.broadcasted_iota(jnp.int32, (tile, tile), 1)
    same = (ri // DN_CHUNK) == (ci // DN_CHUNK)
    low = jnp.where(same & (ci <= ri), 1.0, 0.0).astype(F32)
    upp = jnp.where(same & (ci >= ri), 1.0, 0.0).astype(F32)
    for t in range(tm // tile):
        blk = raw[t * tile:(t + 1) * tile, :]
        pre = jnp.dot(low, blk, precision=HIGHEST, preferred_element_type=F32)
        suf = jnp.dot(upp, blk, precision=HIGHEST, preferred_element_type=F32)
        out = jnp.where(is_a, jnp.where(is_rev, suf, pre), blk)
        gc_ref[t * tile:(t + 1) * tile, :] = out
        gr_ref[t] = lax.dot_general(pm_ref[...], out, (((1,), (1,)), ((), ())),
                                    precision=HIGHEST, preferred_element_type=F32)


def _gate_perm(hv):
    pm = np.zeros((4 * hv, 4 * hv), np.float32)
    for hq in range(hv // 2):
        for d in range(2):
            for isb in range(2):
                for j in range(2):
                    pm[hq * 8 + d * 4 + isb * 2 + j, d * 2 * hv + isb * hv + 2 * hq + j] = 1.0
    return jnp.asarray(pm)


def _dn_gates(ab, a_log, dt_bias, hv, tm):
    rows, width = ab.shape
    assert width == 4 * hv == V7X_LANES
    zeros = jnp.zeros_like(a_log)
    alog = jnp.concatenate([a_log, zeros], axis=1).reshape(1, width)
    dtb = jnp.concatenate([dt_bias, zeros], axis=1).reshape(1, width)
    vec = pl.BlockSpec((1, width), lambda i: (0, 0))
    return pl.pallas_call(
        functools.partial(_dn_gate_kernel, hv=hv),
        grid=(rows // tm,),
        in_specs=[pl.BlockSpec((tm, width), lambda i: (i, 0)), vec, vec,
                  pl.BlockSpec((width, width), lambda i: (0, 0))],
        out_specs=[pl.BlockSpec((tm, width), lambda i: (i, 0)),
                   pl.BlockSpec((tm // V7X_LANES, width, V7X_LANES), lambda i: (i, 0, 0))],
        out_shape=[jax.ShapeDtypeStruct((rows, width), F32),
                   jax.ShapeDtypeStruct((rows // V7X_LANES, width, V7X_LANES), F32)],
        compiler_params=_cparams("parallel"),
        name="dn_gates",
    )(ab, alog, dtb, _gate_perm(hv))


_INV_BLOCK = 4
_DN_GROUP_TILES = 2
_DN_TRIP_GROUPS = 8
_DN_HEADS_PER_STEP = 1


def _mm_bf16(ps, qs):
    return [_dot(p.astype(BF16), q.astype(BF16)) for p, q in zip(ps, qs)]


def _pair_diag(x, lo_half):
    zero = jnp.zeros_like(x)
    return jnp.concatenate([jnp.where(lo_half, x, zero), jnp.where(lo_half, zero, x)], axis=0)


def _unit_tri_inverse(mats, ri, ci, lo_half):
    eye = (ri == ci).astype(F32)
    bi, bj = ri // _INV_BLOCK, ci // _INV_BLOCK
    n = mats[0].shape[0]

    def mm(ps, qs):
        return _mm_bf16(ps, [_pair_diag(q, lo_half) for q in qs])

    ds = [jnp.where(bi == bj, a, 0.0) for a in mats]
    ts = [eye - d for d in ds]
    ps = mm(ds, ds)
    yield
    power = 2
    while 2 * power < _INV_BLOCK:
        both = mm([jnp.concatenate([t, p], axis=0) for t, p in zip(ts, ps)], ps)
        ts = [t + b[:n, :] for t, b in zip(ts, both)]
        ps = [b[n:, :] for b in both]
        power *= 2
        yield
    ts = [t + tp for t, tp in zip(ts, mm(ts, ps))]
    yield
    w = 1
    while w * _INV_BLOCK < n:
        off = (bi // (2 * w) == bj // (2 * w)) & (bi // w != bj // w)
        tes = mm(ts, [jnp.where(off, a, 0.0) for a in mats])
        yield
        ts = [t - tet for t, tet in zip(ts, mm(tes, ts))]
        yield
        w *= 2
    return ts


def _issue_pipelined(n_groups, hop, make_head, make_tail):
    heads, values = {}, {}
    tail, tail_g, tick = None, 0, 0
    while tail_g < n_groups:
        if tick % hop == 0 and tick // hop < n_groups:
            heads[tick // hop] = make_head(tick // hop)
        for g in sorted(heads):
            try:
                next(heads[g])
            except StopIteration as stop:
                values[g] = stop.value
                del heads[g]
        while tail_g < n_groups:
            if tail is None:
                if tail_g not in values:
                    break
                tail = make_tail(values.pop(tail_g))
            try:
                next(tail)
                break
            except StopIteration:
                tail, tail_g = None, tail_g + 1
        tick += 1


def _dn_core_kernel(qc_ref, kc_ref, vc_ref, gcc_ref, grc_ref, ql_ref, kl_ref, vl_ref, gcl_ref, grl_ref,
                    oc_ref, ol_ref, s_ref, *, hv, trip_groups):
    dk = DN_HEAD_DIM
    cs = DN_CHUNK
    tile = V7X_LANES
    n_h = _DN_HEADS_PER_STEP
    n_streams = 4 * n_h
    s_ref[...] = jnp.zeros(s_ref.shape, F32)
    oc_ref[...] = jnp.zeros(oc_ref.shape, F32)
    ol_ref[...] = jnp.zeros(ol_ref.shape, F32)
    ri = lax.broadcasted_iota(jnp.int32, (cs, 2 * cs), 0)
    lane = lax.broadcasted_iota(jnp.int32, (cs, 2 * cs), 1)
    shifts = [(tile - 2 * (n_h * pl.program_id(1) + hh)) % tile for hh in range(n_h)]

    n_sub = tile // cs
    lo_half = lane < cs
    ci = jnp.where(lo_half, lane, lane - cs)
    incl = (ri >= ci, ri <= ci)
    strict = (ri > ci, ri < ci)
    last = (cs - 1, 0)

    def state_free(refs, t_fwd, t_bwd):
        q_ref, k_ref, v_ref, gc_ref, gr_ref, _ = refs
        chunks = ([(0, t, c) for t in t_fwd for c in range(n_sub)]
                  + [(1, t, c) for t in t_bwd for c in reversed(range(n_sub))])
        r0s = [pl.multiple_of(t * tile, tile) + c * cs for _, t, c in chunks]
        cells = [(ic, hh) for ic in range(len(chunks)) for hh in range(n_h)]
        cdir = [chunks[ic][0] for ic, _ in cells]
        kbs = [k_ref[pl.ds(r0s[ic], cs), hh * dk:(hh + 1) * dk] for ic, hh in cells]
        qbs = [q_ref[pl.ds(r0s[ic], cs), hh * dk:(hh + 1) * dk] for ic, hh in cells]
        qkks = [_dot_nt(jnp.concatenate([qb, kb], axis=0), jnp.concatenate([kb, kb], axis=0))
                for qb, kb in zip(qbs, kbs)]
        gcts = [pltpu.roll(gc_ref[pl.ds(r0s[ic], cs), :], shifts[hh], 1) for ic, hh in cells]
        grts = [gr_ref[chunks[ic][1], hh * 8:(hh + 1) * 8, :] for ic, hh in cells]
        yield

        def packed_cols(m, off):
            base = cdir[m] * 2 * hv + off
            return jnp.where(lo_half, gcts[m][:, base:base + 1], gcts[m][:, base + 1:base + 2])

        def packed_row(m, off):
            d, _, c = chunks[cells[m][0]]
            r0_, r1_ = grts[m][d * 4 + off:d * 4 + off + 1, :], grts[m][d * 4 + off + 1:d * 4 + off + 2, :]
            if c == 0:
                return jnp.where(lo_half[:1], r0_, pltpu.roll(r1_, cs, 1))
            return jnp.where(lo_half[:1], pltpu.roll(r0_, cs, 1), r1_)

        n_cells = len(cells)
        dec2s = [jnp.exp(jnp.where(incl[cdir[m]], packed_cols(m, 0) - packed_row(m, 0), -jnp.inf))
                 for m in range(n_cells)]
        amats = [jnp.where(strict[cdir[m]], qkks[m][cs:, :] * dec2s[m] * packed_cols(m, hv), 0.0)
                 for m in range(n_cells)]
        qkms = [(qkks[m][:cs, :] * dec2s[m]).astype(BF16) for m in range(n_cells)]
        yield
        tinvs = yield from _unit_tri_inverse(amats, ri, ci, lo_half)
        tinvs = [t.astype(BF16) for t in tinvs]

        scs = [(m, j) for m in range(n_cells) for j in range(2)]
        dirs = [cdir[m] for m, _ in scs]
        gcols = [gcts[m][:, cdir[m] * 2 * hv + j:cdir[m] * 2 * hv + j + 1] for m, j in scs]
        bcols = [gcts[m][:, cdir[m] * 2 * hv + hv + j:cdir[m] * 2 * hv + hv + j + 1] for m, j in scs]
        vcols = [(cells[m][1] * 2 + j) * dk for m, j in scs]
        rows = [r0s[cells[m][0]] for m, _ in scs]
        ks = [kb.astype(F32) for kb in kbs]

        def head_rows(x, j):
            zero = jnp.zeros_like(x)
            return jnp.concatenate([x, zero] if j == 0 else [zero, x], axis=0)

        rhs = [jnp.concatenate([ks[m] * (bc * jnp.exp(gc)), v_ref[pl.ds(r0, cs), vc:vc + dk] * bc],
                               axis=1).astype(BF16)
               for (m, _), bc, gc, r0, vc in zip(scs, bcols, gcols, rows, vcols)]
        wus = [_dot(tinvs[m], head_rows(r, j)).astype(BF16) for (m, j), r in zip(scs, rhs)]
        yield
        qwus = [_dot(qkms[m], head_rows(wu, j)) for (m, j), wu in zip(scs, wus)]
        g_lasts = [gc[last[d]:last[d] + 1, :] for d, gc in zip(dirs, gcols)]
        kwus = [_dot_tn((ks[m] * jnp.exp(gl - gc)).astype(BF16), wu)
                for (m, _), gl, gc, wu in zip(scs, g_lasts, gcols, wus)]
        yield
        lhs = [jnp.concatenate([kwu[:, :dk].astype(BF16), (qbs[m] * jnp.exp(gc) - qwu[:, :dk]).astype(BF16)], axis=0)
               for (m, _), gc, qwu, kwu in zip(scs, gcols, qwus, kwus)]
        yield
        return dict(rows=list(zip(rows, vcols)), lhs=lhs, add=[kwu[:, dk:] for kwu in kwus],
                    intra=[qwu[:, dk:] for qwu in qwus], decay=[jnp.exp(gl) for gl in g_lasts])

    def recurrence(refs, pre, states):
        o_ref = refs[5]
        n_steps = len(pre["lhs"]) // n_streams
        for step in range(n_steps):
            idx = [((d * n_steps + step) * n_h + hh) * 2 + j for d in range(2) for hh in range(n_h) for j in range(2)]
            rs = [_dot(pre["lhs"][i], st.astype(BF16)) for i, st in zip(idx, states)]
            states[:] = [st * pre["decay"][i] + pre["add"][i] - r[:dk, :] for i, st, r in zip(idx, states, rs)]
            for i, r in zip(idx, rs):
                r0, vc = pre["rows"][i]
                o_ref[pl.ds(r0, cs), vc:vc + dk] += r[dk:, :] + pre["intra"][i]
            yield

    def run(refs, n_tiles):
        group = max(p for p in range(1, _DN_GROUP_TILES + 1) if n_tiles % p == 0)
        n_groups = n_tiles // group
        per_trip = max(p for p in range(1, trip_groups + 1) if n_groups % p == 0)
        hop = group * n_sub

        def tiles(g):
            return [g * group + m for m in range(group)], [n_tiles - 1 - g * group - m for m in range(group)]

        def body(i, carry):
            states = [s_ref[si] for si in range(n_streams)]
            _issue_pipelined(per_trip, hop,
                             lambda g: state_free(refs, *tiles(i * per_trip + g)),
                             lambda pre: recurrence(refs, pre, states))
            for si in range(n_streams):
                s_ref[si] = states[si]
            return carry

        lax.fori_loop(0, n_groups // per_trip, body, 0)

    run((qc_ref, kc_ref, vc_ref, gcc_ref, grc_ref, oc_ref), qc_ref.shape[0] // tile)
    run((ql_ref, kl_ref, vl_ref, gcl_ref, grl_ref, ol_ref), ql_ref.shape[0] // tile)


def _dn_core(qkv_c, gc_c, gr_c, qkv_l, gc_l, gr_l, batch, hv, trip_groups=_DN_TRIP_GROUPS):
    dk = DN_HEAD_DIM
    n_h = _DN_HEADS_PER_STEP
    steps = hv // 2 // n_h
    ctx_len = qkv_c.shape[0] // batch
    seq = qkv_l.shape[0] // batch
    width = gc_l.shape[1]

    def specs(t):
        return [pl.BlockSpec((t, n_h * dk), lambda b, h: (b, h)),
                pl.BlockSpec((t, n_h * dk), lambda b, h: (b, steps + h)),
                pl.BlockSpec((t, 2 * n_h * dk), lambda b, h: (b, steps + h)),
                pl.BlockSpec((t, width), lambda b, h: (b, 0)),
                pl.BlockSpec((t // V7X_LANES, 8 * n_h, V7X_LANES), lambda b, h: (b, h, 0))]

    return pl.pallas_call(
        functools.partial(_dn_core_kernel, hv=hv, trip_groups=trip_groups),
        grid=(batch, steps),
        in_specs=specs(ctx_len) + specs(seq),
        out_specs=[pl.BlockSpec((ctx_len, 2 * n_h * dk), lambda b, h: (b, h)),
                   pl.BlockSpec((seq, 2 * n_h * dk), lambda b, h: (b, h))],
        out_shape=[jax.ShapeDtypeStruct((batch * ctx_len, hv * dk), F32),
                   jax.ShapeDtypeStruct((batch * seq, hv * dk), F32)],
        scratch_shapes=[pltpu.VMEM((4 * n_h, dk, dk), F32)],
        compiler_params=_cparams("parallel", "parallel"),
        name="dn_core",
    )(qkv_c, qkv_c, qkv_c, gc_c, gr_c, qkv_l, qkv_l, qkv_l, gc_l, gr_l)


def _dn_out_kernel(o_ref, z_ref, ng_ref, w_ref, x_ref, gt_ref, out_ref, acc_ref, *, ncol):
    k = pl.program_id(1)

    @pl.when(k == 0)
    def _():
        acc_ref[...] = jnp.zeros(acc_ref.shape, F32)

    o = o_ref[...]
    dk = DN_HEAD_DIM
    segs = [_rms(o[:, h * dk:(h + 1) * dk]) * ng_ref[...] for h in range(o.shape[1] // dk)]
    on = jnp.concatenate(segs, axis=1) * jax.nn.silu(z_ref[...])
    acc_ref[...] += _dot(on.astype(BF16), w_ref[...])

    @pl.when(k == pl.num_programs(1) - 1)
    def _():
        _store_rows(out_ref, _load_rows(x_ref, ncol) + gt_ref[...] * acc_ref[...], ncol)


def _dn_out(o, p, z_col0, norm_g, w_out, x, mods, bidx, tm, seq, ncol):
    rows, d = x.shape
    kdim = w_out.shape[0]
    tk = 1024
    xv, xspec = _row_view(x, tm, seq, ncol)
    zb = z_col0 // tk
    out = pl.pallas_call(
        functools.partial(_dn_out_kernel, ncol=ncol),
        grid=(rows // tm, kdim // tk),
        in_specs=[pl.BlockSpec((tm, tk), lambda i, k: (i, k)),
                  pl.BlockSpec((tm, tk), lambda i, k: (i, zb + k)),
                  pl.BlockSpec((1, DN_HEAD_DIM), lambda i, k: (0, 0)),
                  pl.BlockSpec((tk, d), lambda i, k: (k, 0)),
                  xspec, _mod_spec(5, d, bidx)],
        out_specs=xspec,
        out_shape=jax.ShapeDtypeStruct(xv.shape, F32),
        scratch_shapes=[pltpu.VMEM((tm, d), F32)],
        compiler_params=_cparams("parallel", "arbitrary"),
        name="dn_out",
    )(o, p, norm_g.reshape(1, DN_HEAD_DIM), w_out, xv, mods)
    return out.reshape(rows, d)


def _sg_out_kernel(u_ref, v_ref, lg_ref, lb_ref, ws_ref, bs_ref, w_ref, x_ref, gt_ref, out_ref, m_ref, *, ncol):
    v = v_ref[...]
    mu = jnp.mean(v, axis=-1, keepdims=True)
    var = jnp.mean(jnp.square(v - mu), axis=-1, keepdims=True)
    m_ref[...] = ((v - mu) * lax.rsqrt(var + EPS) * lg_ref[...] + lb_ref[...]).astype(BF16)
    tm, sg_dim = v.shape
    gd = sg_dim // SG_GROUPS
    for g in range(SG_GROUPS):
        wsg = ws_ref[g]
        bias = bs_ref[:, g:g + 1]
        for c in range(tm // SG_CHUNK):
            rs = slice(c * SG_CHUNK, (c + 1) * SG_CHUNK)
            cs = slice(g * gd, (g + 1) * gd)
            mixed = _dot(wsg, m_ref[rs, cs]) + bias
            m_ref[rs, cs] = (u_ref[rs, cs] * mixed).astype(BF16)
    y = _dot(m_ref[...], w_ref[...])
    _store_rows(out_ref, _load_rows(x_ref, ncol) + gt_ref[...] * y, ncol)


def _sg_out(uv, ln_g, ln_b, w_s, b_s, w_out, x, mods, bidx, tm, seq, ncol):
    rows, d = x.shape
    sg_dim = w_out.shape[0]
    xv, xspec = _row_view(x, tm, seq, ncol)
    vec = pl.BlockSpec((1, sg_dim), lambda i: (0, 0))
    out = pl.pallas_call(
        functools.partial(_sg_out_kernel, ncol=ncol),
        grid=(rows // tm,),
        in_specs=[pl.BlockSpec((tm, sg_dim), lambda i: (i, 0)),
                  pl.BlockSpec((tm, sg_dim), lambda i: (i, 1)),
                  vec, vec,
                  pl.BlockSpec((SG_GROUPS, SG_CHUNK, SG_CHUNK), lambda i: (0, 0, 0)),
                  pl.BlockSpec((SG_CHUNK, SG_GROUPS), lambda i: (0, 0)),
                  pl.BlockSpec((sg_dim, d), lambda i: (0, 0), pipeline_mode=pl.Buffered(1)),
                  xspec, _mod_spec(5, d, bidx)],
        out_specs=xspec,
        out_shape=jax.ShapeDtypeStruct(xv.shape, F32),
        scratch_shapes=[pltpu.VMEM((tm, sg_dim), BF16)],
        compiler_params=_cparams("parallel"),
        name="sg_out",
    )(uv, uv, ln_g.reshape(1, sg_dim), ln_b.reshape(1, sg_dim), w_s, b_s.T, w_out, xv, mods)
    return out.reshape(rows, d)


def kernel(x, c, ctx, c_ctx, mod_w, mod_b, norm_g, ffn_w_gu, ffn_w_down, dn_w_in, dn_conv_w, dn_a_log, dn_dt_bias,
           dn_norm_g, dn_w_out, sg_w_in, sg_ln_g, sg_ln_b, sg_w_s, sg_b_s, sg_w_out, final_norm_g):
    batch, seq, d = x.shape
    ctx_len = ctx.shape[1]
    depth = mod_w.shape[0]
    nm = mod_w.shape[2]
    hv = dn_a_log.shape[2]
    dk = DN_HEAD_DIM
    qk_dim = hv // 2 * dk
    v_dim = hv * dk
    conv_dim = 2 * qk_dim + v_dim
    assert batch + 1 <= MOD_ROWS

    xl = x.reshape(batch * seq, d)
    xc = ctx.reshape(batch * ctx_len, d)
    tm_l = min(512, seq)
    tm_c = min(512, batch * ctx_len)
    tiles_per_seq = seq // tm_l
    bidx_l = lambda i: i // tiles_per_seq
    bidx_c = lambda i: batch
    ncol_cm = tm_l // (seq // GRID_W)
    tm_f = tm_l
    bidx_f = lambda i, n=seq // tm_f: i // n

    cs = jnp.concatenate([c, c_ctx[None, :], jnp.zeros((MOD_ROWS - batch - 1, d), F32)], axis=0)
    mods_all = _modulation(cs, mod_w, mod_b).reshape(depth, MOD_ROWS, 1, nm)

    w_gu = ffn_w_gu.astype(BF16)
    w_down = ffn_w_down.astype(BF16)
    n_mix = 2

    for i in range(depth):
        last = i == depth - 1
        kind = i % n_mix
        j = i // n_mix
        ncol = ncol_cm if (j % 2) == 1 else 0
        need_ctx = not (last and kind == 1)
        mods = mods_all[i]

        fc_l = _FFN_CHUNK if i < 2 else _FFN_CHUNK // 2
        xl = _half_ffn(xl, mods, 0, bidx_f, norm_g[i, 0], w_gu, w_down, i, 0, tm_f, fc=fc_l)
        if need_ctx:
            xc = _half_ffn(xc, mods, 0, bidx_c, norm_g[i, 0], w_gu, w_down, i, 0, tm_c)

        if kind == 0:
            w_in = dn_w_in[j].astype(BF16)
            w_main, w_ab = w_in[:, :conv_dim + v_dim], w_in[:, conv_dim + v_dim:]
            w_out = dn_w_out[j].astype(BF16)
            pl_, ab_l = _prenorm_linear(xl, mods, 3, bidx_l, norm_g[i, 1], w_main, tm_l, seq, ncol, w_extra=w_ab)
            pc_, ab_c = _prenorm_linear(xc, mods, 3, bidx_c, norm_g[i, 1], w_main, tm_c, ctx_len, 0, w_extra=w_ab)
            qkv_l = _dn_conv(pl_, dn_conv_w[j], seq, qk_dim, conv_dim, cb_long=256 if i == 0 else 512)
            qkv_c = _dn_conv(pc_, dn_conv_w[j], ctx_len, qk_dim, conv_dim)
            gc_l, gr_l = _dn_gates(ab_l, dn_a_log[j], dn_dt_bias[j], hv, tm_l)
            gc_c, gr_c = _dn_gates(ab_c, dn_a_log[j], dn_dt_bias[j], hv, tm_c)
            o_c, o_l = _dn_core(qkv_c, gc_c, gr_c, qkv_l, gc_l, gr_l, batch, hv,
                                trip_groups=_DN_TRIP_GROUPS if i == 0 else _DN_TRIP_GROUPS // 2)
            xl = _dn_out(o_l, pl_, conv_dim, dn_norm_g[j], w_out, xl, mods, bidx_l, tm_l, seq, ncol)
            if not last:
                xc = _dn_out(o_c, pc_, conv_dim, dn_norm_g[j], w_out, xc, mods, bidx_c, tm_c, ctx_len, 0)
        else:
            w_in = sg_w_in[j].astype(BF16)
            w_out = sg_w_out[j].astype(BF16)
            w_s = sg_w_s[j].astype(BF16)
            tm_sl = min(256, seq)
            tm_sc = min(256, batch * ctx_len)
            ncol_s = (tm_sl // (seq // GRID_W)) if ncol else 0
            bidx_sl = lambda i, n=seq // tm_sl: i // n
            uv_l = _prenorm_linear(xl, mods, 3, bidx_l, norm_g[i, 1], w_in, tm_l, seq, ncol, act="gelu",
                                   tn=2048 if i == 1 else 1024)
            xl = _sg_out(uv_l, sg_ln_g[j], sg_ln_b[j], w_s, sg_b_s[j], w_out, xl, mods, bidx_sl, tm_sl, seq, ncol_s)
            if not last:
                uv_c = _prenorm_linear(xc, mods, 3, bidx_c, norm_g[i, 1], w_in, tm_c, ctx_len, 0, act="gelu")
                xc = _sg_out(uv_c, sg_ln_g[j], sg_ln_b[j], w_s, sg_b_s[j], w_out, xc, mods, bidx_c, tm_sc, ctx_len, 0)

        xl = _half_ffn(xl, mods, 6, bidx_f, norm_g[i, 2], w_gu, w_down, i, 1, tm_f,
                       final_g=final_norm_g if last else None, fc=fc_l)
        if not last:
            xc = _half_ffn(xc, mods, 6, bidx_c, norm_g[i, 2], w_gu, w_down, i, 1, tm_c)

    return xl.reshape(batch, seq, d)
```

```python
import functools

import numpy as np
import jax
import jax.numpy as jnp
from jax import lax
from jax.experimental import pallas as pl
from jax.experimental.pallas import tpu as pltpu

F32 = jnp.float32
BF16 = jnp.bfloat16
EPS = 1e-6
GRID_W = 64
N_MOD = 9
DN_HEAD_DIM = 128
DN_CHUNK = 64
DN_CONV_K = 5
SG_CHUNK = 128
SG_GROUPS = 16

V7X_LANES = 128
V7X_SUBLANES = 8
V7X_VMEM_BYTES = 64 * 1024 * 1024
VMEM_LIMIT = V7X_VMEM_BYTES - 8 * 1024 * 1024
MOD_ROWS = 16
HIGHEST = lax.Precision.HIGHEST


def _cparams(*sem):
    return pltpu.CompilerParams(dimension_semantics=sem, vmem_limit_bytes=VMEM_LIMIT)


def _dot(a, b):
    return jnp.dot(a, b, preferred_element_type=F32)


def _dot_nt(a, b):
    return lax.dot_general(a, b, (((1,), (1,)), ((), ())), preferred_element_type=F32)


def _dot_tn(a, b):
    return lax.dot_general(a, b, (((0,), (0,)), ((), ())), preferred_element_type=F32)


def _rms(x):
    return x * lax.rsqrt(jnp.mean(x * x, axis=-1, keepdims=True) + EPS)


def _load_rows(ref, ncol):
    if not ncol:
        return ref[...]
    width = ref.shape[1] // ncol
    return jnp.concatenate([ref[:, k * width:(k + 1) * width] for k in range(ncol)], axis=0)


def _store_rows(ref, val, ncol):
    if not ncol:
        ref[...] = val
        return
    width = ref.shape[1] // ncol
    rows = ref.shape[0]
    for k in range(ncol):
        ref[:, k * width:(k + 1) * width] = val[k * rows:(k + 1) * rows, :]


def _row_view(arr, tm, seq, ncol):
    width = arr.shape[1]
    if not ncol:
        return arr, pl.BlockSpec((tm, width), lambda i, *_: (i, 0))
    grid_rows = seq // GRID_W
    per_batch = GRID_W // ncol
    view = arr.reshape(arr.shape[0] // seq, grid_rows, GRID_W * width)
    return view, pl.BlockSpec((None, grid_rows, ncol * width), lambda i, *_: (i // per_batch, 0, i % per_batch))


def _mod_spec(k, d, bidx):
    return pl.BlockSpec((None, 1, d), lambda i, *_: (bidx(i), 0, k))


def _mod_kernel(c_ref, w_ref, b_ref, o_ref):
    s = jax.nn.silu(c_ref[...]).astype(BF16)
    o_ref[...] = _dot(s, w_ref[...].astype(BF16)) + b_ref[...]


def _modulation(cs, mod_w, mod_b):
    depth, d, nm = mod_w.shape
    tn = 1024
    return pl.pallas_call(
        _mod_kernel,
        grid=(depth, nm // tn),
        in_specs=[pl.BlockSpec((MOD_ROWS, d), lambda l, j: (0, 0)),
                  pl.BlockSpec((None, d, tn), lambda l, j: (l, 0, j)),
                  pl.BlockSpec((None, 1, tn), lambda l, j: (l, 0, j))],
        out_specs=pl.BlockSpec((None, MOD_ROWS, tn), lambda l, j: (l, 0, j)),
        out_shape=jax.ShapeDtypeStruct((depth, MOD_ROWS, nm), F32),
        compiler_params=_cparams("parallel", "parallel"),
        name="modulation",
    )(cs, mod_w, mod_b.reshape(depth, 1, nm))


_FFN_CHUNK = 512


def _ffn_kernel(x_ref, ng_ref, *rest, final_norm, merged):
    d = x_ref.shape[1]
    if merged:
        m_ref, wg_ref, wu_ref, wd_ref, *rest = rest
        shift, scale, gate = (lambda: m_ref[:, 0:d]), (lambda: m_ref[:, d:2 * d]), (lambda: m_ref[:, 2 * d:3 * d])
    else:
        sh_ref, sc_ref, gt_ref, wg_ref, wu_ref, wd_ref, *rest = rest
        shift, scale, gate = (lambda: sh_ref[...]), (lambda: sc_ref[...]), (lambda: gt_ref[...])
    if final_norm:
        fg_ref, o_ref, xn_ref = rest
    else:
        o_ref, xn_ref = rest
    j = pl.program_id(1)

    @pl.when(j == 0)
    def _():
        gain = ng_ref[...] * (1.0 + scale())
        xn_ref[...] = (_rms(x_ref[...]) * gain + shift()).astype(BF16)
        o_ref[...] = jnp.zeros(o_ref.shape, F32)

    xn = xn_ref[...]
    g = _dot(xn, wg_ref[...])
    u = _dot(xn, wu_ref[...])
    h = (jax.nn.silu(g) * u).astype(BF16)
    o_ref[...] += _dot(h, wd_ref[...])

    @pl.when(j == pl.num_programs(1) - 1)
    def _():
        r = x_ref[...] + 0.5 * gate() * o_ref[...]
        if final_norm:
            r = _rms(r) * fg_ref[...]
        o_ref[...] = r


def _half_ffn(x, mods, k0, bidx, norm_g, w_gu, w_down, layer, half, tm, final_g=None, fc=_FFN_CHUNK,
              merged=False, sem=("parallel", "arbitrary")):
    rows, d = x.shape
    nf = w_down.shape[2] // fc
    row = pl.BlockSpec((tm, d), lambda i, j: (i, 0))
    vec = pl.BlockSpec((1, d), lambda i, j: (0, 0))
    if merged:
        mod_specs = [pl.BlockSpec((None, 1, 3 * d), lambda i, j: (bidx(i), 0, k0 // 3))]
    else:
        mod_specs = [_mod_spec(k0, d, bidx), _mod_spec(k0 + 1, d, bidx), _mod_spec(k0 + 2, d, bidx)]
    in_specs = [row, vec] + mod_specs + [
        pl.BlockSpec((None, None, d, fc), lambda i, j: (layer, half, 0, j)),
        pl.BlockSpec((None, None, d, fc), lambda i, j: (layer, half, 0, nf + j)),
        pl.BlockSpec((None, None, fc, d), lambda i, j: (layer, half, j, 0))]
    args = [x, norm_g.reshape(1, d)] + [mods] * len(mod_specs) + [w_gu, w_gu, w_down]
    if final_g is not None:
        in_specs.append(vec)
        args.append(final_g.reshape(1, d))
    return pl.pallas_call(
        functools.partial(_ffn_kernel, final_norm=final_g is not None, merged=merged),
        grid=(rows // tm, nf),
        in_specs=in_specs,
        out_specs=row,
        out_shape=jax.ShapeDtypeStruct((rows, d), F32),
        scratch_shapes=[pltpu.VMEM((tm, d), BF16)],
        compiler_params=_cparams(*sem),
        name="half_ffn",
    )(*args)


def _prenorm_linear_kernel(x_ref, ng_ref, sh_ref, sc_ref, w_ref, *rest, ncol, act, has_extra):
    if has_extra:
        wx_ref, o_ref, ox_ref, xn_ref = rest
    else:
        o_ref, xn_ref = rest

    @pl.when(pl.program_id(1) == 0)
    def _():
        gain = ng_ref[...] * (1.0 + sc_ref[...])
        xn = (_rms(_load_rows(x_ref, ncol)) * gain + sh_ref[...]).astype(BF16)
        xn_ref[...] = xn
        if has_extra:
            ox_ref[...] = _dot(xn, wx_ref[...])

    y = _dot(xn_ref[...], w_ref[...])
    if act == "gelu":
        y = 0.5 * y * (1.0 + lax.erf(y * (0.5 ** 0.5)))
    o_ref[...] = y.astype(o_ref.dtype)


def _prenorm_linear(x, mods, k0, bidx, norm_g, w, tm, seq, ncol, act=None, w_extra=None, tn=2048):
    rows, d = x.shape
    n = w.shape[1]
    assert n % tn == 0
    xv, xspec = _row_view(x, tm, seq, ncol)
    vec = pl.BlockSpec((1, d), lambda i, j: (0, 0))
    in_specs = [xspec, vec, _mod_spec(k0, d, bidx), _mod_spec(k0 + 1, d, bidx),
                pl.BlockSpec((d, tn), lambda i, j: (0, j))]
    args = [xv, norm_g.reshape(1, d), mods, mods, w]
    out_specs = [pl.BlockSpec((tm, tn), lambda i, j: (i, j))]
    out_shape = [jax.ShapeDtypeStruct((rows, n), F32)]
    if w_extra is not None:
        nx = w_extra.shape[1]
        in_specs.append(pl.BlockSpec((d, nx), lambda i, j: (0, 0)))
        args.append(w_extra)
        out_specs.append(pl.BlockSpec((tm, nx), lambda i, j: (i, 0)))
        out_shape.append(jax.ShapeDtypeStruct((rows, nx), F32))
    outs = pl.pallas_call(
        functools.partial(_prenorm_linear_kernel, ncol=ncol, act=act, has_extra=w_extra is not None),
        grid=(rows // tm, n // tn),
        in_specs=in_specs,
        out_specs=out_specs,
        out_shape=out_shape,
        scratch_shapes=[pltpu.VMEM((tm, d), BF16)],
        compiler_params=_cparams("parallel", "arbitrary"),
        name="mixer_in_proj",
    )(*args)
    return outs if w_extra is not None else outs[0]


_CONV_PAD = V7X_SUBLANES


def _dn_conv_kernel(p_ref, cw_ref, o_ref, xp_ref, *, seq, sub, n_q, n_qk):
    c = pl.program_id(1)
    cb = p_ref.shape[1]
    half = DN_CONV_K // 2
    xp_ref[0:_CONV_PAD, :] = jnp.zeros((_CONV_PAD, cb), F32)
    xp_ref[_CONV_PAD:_CONV_PAD + seq, :] = p_ref[...]
    xp_ref[_CONV_PAD + seq:, :] = jnp.zeros((_CONV_PAD, cb), F32)
    w = cw_ref[...]

    def conv_tile(i):
        r0 = pl.multiple_of(i * sub, sub)
        win = xp_ref[pl.ds(r0, sub + 2 * _CONV_PAD), :]
        acc = None
        for t in range(DN_CONV_K):
            lo = _CONV_PAD - half + t
            term = win[lo:lo + sub, :] * w[t:t + 1, :]
            acc = term if acc is None else acc + term
        return r0, jax.nn.silu(acc)

    @pl.when(c < n_qk)
    def _():
        scale = jnp.where(c < n_q, DN_HEAD_DIM ** -0.5, 1.0).astype(F32)

        def body(i, carry):
            r0, y = conv_tile(i)
            for hh in range(cb // DN_HEAD_DIM):
                seg = y[:, hh * DN_HEAD_DIM:(hh + 1) * DN_HEAD_DIM]
                seg = seg * lax.rsqrt(jnp.sum(seg * seg, axis=-1, keepdims=True) + EPS)
                o_ref[pl.ds(r0, sub), hh * DN_HEAD_DIM:(hh + 1) * DN_HEAD_DIM] = (seg * scale).astype(o_ref.dtype)
            return carry

        lax.fori_loop(0, seq // sub, body, 0)

    @pl.when(c >= n_qk)
    def _():
        def body(i, carry):
            r0, y = conv_tile(i)
            o_ref[pl.ds(r0, sub), :] = y.astype(o_ref.dtype)
            return carry

        lax.fori_loop(0, seq // sub, body, 0)


def _dn_conv(p, conv_w, seq, qk_dim, conv_dim, cb_long=256):
    rows = p.shape[0]
    cb = cb_long if seq > 1024 else 1024
    sub = min(seq, 256)
    return pl.pallas_call(
        functools.partial(_dn_conv_kernel, seq=seq, sub=sub, n_q=qk_dim // cb, n_qk=2 * qk_dim // cb),
        grid=(rows // seq, conv_dim // cb),
        in_specs=[pl.BlockSpec((seq, cb), lambda s, c: (s, c)),
                  pl.BlockSpec((DN_CONV_K, cb), lambda s, c: (0, c))],
        out_specs=pl.BlockSpec((seq, cb), lambda s, c: (s, c)),
        out_shape=jax.ShapeDtypeStruct((rows, conv_dim), BF16),
        scratch_shapes=[pltpu.VMEM((seq + 2 * _CONV_PAD, cb), F32)],
        compiler_params=_cparams("parallel", "parallel"),
        name="dn_conv",
    )(p, conv_w)


def _dn_gate_kernel(ab_ref, alog_ref, dtb_ref, pm_ref, gc_ref, gr_ref, *, hv):
    tm, width = ab_ref.shape
    tile = V7X_LANES
    x = ab_ref[...]
    lane_full = lax.broadcasted_iota(jnp.int32, (tm, width), 1)
    lane = lax.broadcasted_iota(jnp.int32, (tile, width), 1)
    is_a = (lane % (2 * hv)) < hv
    is_rev = lane >= 2 * hv
    g = -jnp.exp(alog_ref[...]) * jax.nn.softplus(x + dtb_ref[...])
    raw = jnp.where((lane_full % (2 * hv)) < hv, g, jax.nn.sigmoid(x))
    ri = lax.broadcasted_iota(jnp.int32, (tile, tile), 0)
    ci = lax.broadcasted_iota(jnp.int32, (tile, tile), 1)
    same = (ri // DN_CHUNK) == (ci // DN_CHUNK)
    low = jnp.where(same & (ci <= ri), 1.0, 0.0).astype(F32)
    upp = jnp.where(same & (ci >= ri), 1.0, 0.0).astype(F32)
    for t in range(tm // tile):
        blk = raw[t * tile:(t + 1) * tile, :]
        pre = jnp.dot(low, blk, precision=HIGHEST, preferred_element_type=F32)
        suf = jnp.dot(upp, blk, precision=HIGHEST, preferred_element_type=F32)
        out = jnp.where(is_a, jnp.where(is_rev, suf, pre), blk)
        gc_ref[t * tile:(t + 1) * tile, :] = out
        gr_ref[t] = lax.dot_general(pm_ref[...], out, (((1,), (1,)), ((), ())),
                                    precision=HIGHEST, preferred_element_type=F32)


def _gate_perm(hv):
    pm = np.zeros((4 * hv, 4 * hv), np.float32)
    for hq in range(hv // 2):
        for d in range(2):
            for isb in range(2):
                for j in range(2):
                    pm[hq * 8 + d * 4 + isb * 2 + j, d * 2 * hv + isb * hv + 2 * hq + j] = 1.0
    return jnp.asarray(pm)


def _dn_gates(ab, a_log, dt_bias, hv, tm):
    rows, width = ab.shape
    assert width == 4 * hv == V7X_LANES
    zeros = jnp.zeros_like(a_log)
    alog = jnp.concatenate([a_log, zeros], axis=1).reshape(1, width)
    dtb = jnp.concatenate([dt_bias, zeros], axis=1).reshape(1, width)
    vec = pl.BlockSpec((1, width), lambda i: (0, 0))
    return pl.pallas_call(
        functools.partial(_dn_gate_kernel, hv=hv),
        grid=(rows // tm,),
        in_specs=[pl.BlockSpec((tm, width), lambda i: (i, 0)), vec, vec,
                  pl.BlockSpec((width, width), lambda i: (0, 0))],
        out_specs=[pl.BlockSpec((tm, width), lambda i: (i, 0)),
                   pl.BlockSpec((tm // V7X_LANES, width, V7X_LANES), lambda i: (i, 0, 0))],
        out_shape=[jax.ShapeDtypeStruct((rows, width), F32),
                   jax.ShapeDtypeStruct((rows // V7X_LANES, width, V7X_LANES), F32)],
        compiler_params=_cparams("parallel"),
        name="dn_gates",
    )(ab, alog, dtb, _gate_perm(hv))


_INV_BLOCK = 4
_DN_GROUP_TILES = 2
_DN_TRIP_GROUPS = 8
_DN_HEADS_PER_STEP = 1


def _mm_bf16(ps, qs):
    return [_dot(p.astype(BF16), q.astype(BF16)) for p, q in zip(ps, qs)]


def _pair_diag(x, lo_half):
    zero = jnp.zeros_like(x)
    return jnp.concatenate([jnp.where(lo_half, x, zero), jnp.where(lo_half, zero, x)], axis=0)


def _unit_tri_inverse(mats, ri, ci, lo_half):
    eye = (ri == ci).astype(F32)
    bi, bj = ri // _INV_BLOCK, ci // _INV_BLOCK
    n = mats[0].shape[0]

    def mm(ps, qs):
        return _mm_bf16(ps, [_pair_diag(q, lo_half) for q in qs])

    ds = [jnp.where(bi == bj, a, 0.0) for a in mats]
    ts = [eye - d for d in ds]
    ps = mm(ds, ds)
    yield
    power = 2
    while 2 * power < _INV_BLOCK:
        both = mm([jnp.concatenate([t, p], axis=0) for t, p in zip(ts, ps)], ps)
        ts = [t + b[:n, :] for t, b in zip(ts, both)]
        ps = [b[n:, :] for b in both]
        power *= 2
        yield
    ts = [t + tp for t, tp in zip(ts, mm(ts, ps))]
    yield
    w = 1
    while w * _INV_BLOCK < n:
        off = (bi // (2 * w) == bj // (2 * w)) & (bi // w != bj // w)
        tes = mm(ts, [jnp.where(off, a, 0.0) for a in mats])
        yield
        ts = [t - tet for t, tet in zip(ts, mm(tes, ts))]
        yield
        w *= 2
    return ts


def _issue_pipelined(n_groups, hop, make_head, make_tail):
    heads, values = {}, {}
    tail, tail_g, tick = None, 0, 0
    while tail_g < n_groups:
        if tick % hop == 0 and tick // hop < n_groups:
            heads[tick // hop] = make_head(tick // hop)
        for g in sorted(heads):
            try:
                next(heads[g])
            except StopIteration as stop:
                values[g] = stop.value
                del heads[g]
        while tail_g < n_groups:
            if tail is None:
                if tail_g not in values:
                    break
                tail = make_tail(values.pop(tail_g))
            try:
                next(tail)
                break
            except StopIteration:
                tail, tail_g = None, tail_g + 1
        tick += 1


def _dn_core_kernel(qc_ref, kc_ref, vc_ref, gcc_ref, grc_ref, ql_ref, kl_ref, vl_ref, gcl_ref, grl_ref,
                    oc_ref, ol_ref, s_ref, *, hv, trip_groups):
    dk = DN_HEAD_DIM
    cs = DN_CHUNK
    tile = V7X_LANES
    n_h = _DN_HEADS_PER_STEP
    n_streams = 4 * n_h
    s_ref[...] = jnp.zeros(s_ref.shape, F32)
    oc_ref[...] = jnp.zeros(oc_ref.shape, F32)
    ol_ref[...] = jnp.zeros(ol_ref.shape, F32)
    ri = lax.broadcasted_iota(jnp.int32, (cs, 2 * cs), 0)
    lane = lax.broadcasted_iota(jnp.int32, (cs, 2 * cs), 1)
    shifts = [(tile - 2 * (n_h * pl.program_id(1) + hh)) % tile for hh in range(n_h)]

    n_sub = tile // cs
    lo_half = lane < cs
    ci = jnp.where(lo_half, lane, lane - cs)
    incl = (ri >= ci, ri <= ci)
    strict = (ri > ci, ri < ci)
    last = (cs - 1, 0)

    def state_free(refs, t_fwd, t_bwd):
        q_ref, k_ref, v_ref, gc_ref, gr_ref, _ = refs
        chunks = ([(0, t, c) for t in t_fwd for c in range(n_sub)]
                  + [(1, t, c) for t in t_bwd for c in reversed(range(n_sub))])
        r0s = [pl.multiple_of(t * tile, tile) + c * cs for _, t, c in chunks]
        cells = [(ic, hh) for ic in range(len(chunks)) for hh in range(n_h)]
        cdir = [chunks[ic][0] for ic, _ in cells]
        kbs = [k_ref[pl.ds(r0s[ic], cs), hh * dk:(hh + 1) * dk] for ic, hh in cells]
        qbs = [q_ref[pl.ds(r0s[ic], cs), hh * dk:(hh + 1) * dk] for ic, hh in cells]
        qkks = [_dot_nt(jnp.concatenate([qb, kb], axis=0), jnp.concatenate([kb, kb], axis=0))
                for qb, kb in zip(qbs, kbs)]
        gcts = [pltpu.roll(gc_ref[pl.ds(r0s[ic], cs), :], shifts[hh], 1) for ic, hh in cells]
        grts = [gr_ref[chunks[ic][1], hh * 8:(hh + 1) * 8, :] for ic, hh in cells]
        yield

        def packed_cols(m, off):
            base = cdir[m] * 2 * hv + off
            return jnp.where(lo_half, gcts[m][:, base:base + 1], gcts[m][:, base + 1:base + 2])

        def packed_row(m, off):
            d, _, c = chunks[cells[m][0]]
            r0_, r1_ = grts[m][d * 4 + off:d * 4 + off + 1, :], grts[m][d * 4 + off + 1:d * 4 + off + 2, :]
            if c == 0:
                return jnp.where(lo_half[:1], r0_, pltpu.roll(r1_, cs, 1))
            return jnp.where(lo_half[:1], pltpu.roll(r0_, cs, 1), r1_)

        n_cells = len(cells)
        dec2s = [jnp.exp(jnp.where(incl[cdir[m]], packed_cols(m, 0) - packed_row(m, 0), -jnp.inf))
                 for m in range(n_cells)]
        amats = [jnp.where(strict[cdir[m]], qkks[m][cs:, :] * dec2s[m] * packed_cols(m, hv), 0.0)
                 for m in range(n_cells)]
        qkms = [(qkks[m][:cs, :] * dec2s[m]).astype(BF16) for m in range(n_cells)]
        yield
        tinvs = yield from _unit_tri_inverse(amats, ri, ci, lo_half)
        tinvs = [t.astype(BF16) for t in tinvs]

        scs = [(m, j) for m in range(n_cells) for j in range(2)]
        dirs = [cdir[m] for m, _ in scs]
        gcols = [gcts[m][:, cdir[m] * 2 * hv + j:cdir[m] * 2 * hv + j + 1] for m, j in scs]
        bcols = [gcts[m][:, cdir[m] * 2 * hv + hv + j:cdir[m] * 2 * hv + hv + j + 1] for m, j in scs]
        vcols = [(cells[m][1] * 2 + j) * dk for m, j in scs]
        rows = [r0s[cells[m][0]] for m, _ in scs]
        ks = [kb.astype(F32) for kb in kbs]

        def head_rows(x, j):
            zero = jnp.zeros_like(x)
            return jnp.concatenate([x, zero] if j == 0 else [zero, x], axis=0)

        rhs = [jnp.concatenate([ks[m] * (bc * jnp.exp(gc)), v_ref[pl.ds(r0, cs), vc:vc + dk] * bc],
                               axis=1).astype(BF16)
               for (m, _), bc, gc, r0, vc in zip(scs, bcols, gcols, rows, vcols)]
        wus = [_dot(tinvs[m], head_rows(r, j)).astype(BF16) for (m, j), r in zip(scs, rhs)]
        yield
        qwus = [_dot(qkms[m], head_rows(wu, j)) for (m, j), wu in zip(scs, wus)]
        g_lasts = [gc[last[d]:last[d] + 1, :] for d, gc in zip(dirs, gcols)]
        kwus = [_dot_tn((ks[m] * jnp.exp(gl - gc)).astype(BF16), wu)
                for (m, _), gl, gc, wu in zip(scs, g_lasts, gcols, wus)]
        yield
        lhs = [jnp.concatenate([kwu[:, :dk].astype(BF16), (qbs[m] * jnp.exp(gc) - qwu[:, :dk]).astype(BF16)], axis=0)
               for (m, _), gc, qwu, kwu in zip(scs, gcols, qwus, kwus)]
        yield
        return dict(rows=list(zip(rows, vcols)), lhs=lhs, add=[kwu[:, dk:] for kwu in kwus],
                    intra=[qwu[:, dk:] for qwu in qwus], decay=[jnp.exp(gl) for gl in g_lasts])

    def recurrence(refs, pre, states):
        o_ref = refs[5]
        n_steps = len(pre["lhs"]) // n_streams
        for step in range(n_steps):
            idx = [((d * n_steps + step) * n_h + hh) * 2 + j for d in range(2) for hh in range(n_h) for j in range(2)]
            rs = [_dot(pre["lhs"][i], st.astype(BF16)) for i, st in zip(idx, states)]
            states[:] = [st * pre["decay"][i] + pre["add"][i] - r[:dk, :] for i, st, r in zip(idx, states, rs)]
            for i, r in zip(idx, rs):
                r0, vc = pre["rows"][i]
                o_ref[pl.ds(r0, cs), vc:vc + dk] += r[dk:, :] + pre["intra"][i]
            yield

    def run(refs, n_tiles):
        group = max(p for p in range(1, _DN_GROUP_TILES + 1) if n_tiles % p == 0)
        n_groups = n_tiles // group
        per_trip = max(p for p in range(1, trip_groups + 1) if n_groups % p == 0)
        hop = group * n_sub

        def tiles(g):
            return [g * group + m for m in range(group)], [n_tiles - 1 - g * group - m for m in range(group)]

        def body(i, carry):
            states = [s_ref[si] for si in range(n_streams)]
            _issue_pipelined(per_trip, hop,
                             lambda g: state_free(refs, *tiles(i * per_trip + g)),
                             lambda pre: recurrence(refs, pre, states))
            for si in range(n_streams):
                s_ref[si] = states[si]
            return carry

        lax.fori_loop(0, n_groups // per_trip, body, 0)

    run((qc_ref, kc_ref, vc_ref, gcc_ref, grc_ref, oc_ref), qc_ref.shape[0] // tile)
    run((ql_ref, kl_ref, vl_ref, gcl_ref, grl_ref, ol_ref), ql_ref.shape[0] // tile)


def _dn_core(qkv_c, gc_c, gr_c, qkv_l, gc_l, gr_l, batch, hv, trip_groups=_DN_TRIP_GROUPS):
    dk = DN_HEAD_DIM
    n_h = _DN_HEADS_PER_STEP
    steps = hv // 2 // n_h
    ctx_len = qkv_c.shape[0] // batch
    seq = qkv_l.shape[0] // batch
    width = gc_l.shape[1]

    def specs(t):
        return [pl.BlockSpec((t, n_h * dk), lambda b, h: (b, h)),
                pl.BlockSpec((t, n_h * dk), lambda b, h: (b, steps + h)),
                pl.BlockSpec((t, 2 * n_h * dk), lambda b, h: (b, steps + h)),
                pl.BlockSpec((t, width), lambda b, h: (b, 0)),
                pl.BlockSpec((t // V7X_LANES, 8 * n_h, V7X_LANES), lambda b, h: (b, h, 0))]

    return pl.pallas_call(
        functools.partial(_dn_core_kernel, hv=hv, trip_groups=trip_groups),
        grid=(batch, steps),
        in_specs=specs(ctx_len) + specs(seq),
        out_specs=[pl.BlockSpec((ctx_len, 2 * n_h * dk), lambda b, h: (b, h)),
                   pl.BlockSpec((seq, 2 * n_h * dk), lambda b, h: (b, h))],
        out_shape=[jax.ShapeDtypeStruct((batch * ctx_len, hv * dk), F32),
                   jax.ShapeDtypeStruct((batch * seq, hv * dk), F32)],
        scratch_shapes=[pltpu.VMEM((4 * n_h, dk, dk), F32)],
        compiler_params=_cparams("parallel", "parallel"),
        name="dn_core",
    )(qkv_c, qkv_c, qkv_c, gc_c, gr_c, qkv_l, qkv_l, qkv_l, gc_l, gr_l)


def _dn_out_kernel(o_ref, z_ref, ng_ref, w_ref, x_ref, gt_ref, out_ref, acc_ref, *, ncol):
    k = pl.program_id(1)

    @pl.when(k == 0)
    def _():
        acc_ref[...] = jnp.zeros(acc_ref.shape, F32)

    o = o_ref[...]
    dk = DN_HEAD_DIM
    segs = [_rms(o[:, h * dk:(h + 1) * dk]) * ng_ref[...] for h in range(o.shape[1] // dk)]
    on = jnp.concatenate(segs, axis=1) * jax.nn.silu(z_ref[...])
    acc_ref[...] += _dot(on.astype(BF16), w_ref[...])

    @pl.when(k == pl.num_programs(1) - 1)
    def _():
        _store_rows(out_ref, _load_rows(x_ref, ncol) + gt_ref[...] * acc_ref[...], ncol)


def _dn_out(o, p, z_col0, norm_g, w_out, x, mods, bidx, tm, seq, ncol):
    rows, d = x.shape
    kdim = w_out.shape[0]
    tk = 1024
    xv, xspec = _row_view(x, tm, seq, ncol)
    zb = z_col0 // tk
    out = pl.pallas_call(
        functools.partial(_dn_out_kernel, ncol=ncol),
        grid=(rows // tm, kdim // tk),
        in_specs=[pl.BlockSpec((tm, tk), lambda i, k: (i, k)),
                  pl.BlockSpec((tm, tk), lambda i, k: (i, zb + k)),
                  pl.BlockSpec((1, DN_HEAD_DIM), lambda i, k: (0, 0)),
                  pl.BlockSpec((tk, d), lambda i, k: (k, 0)),
                  xspec, _mod_spec(5, d, bidx)],
        out_specs=xspec,
        out_shape=jax.ShapeDtypeStruct(xv.shape, F32),
        scratch_shapes=[pltpu.VMEM((tm, d), F32)],
        compiler_params=_cparams("parallel", "arbitrary"),
        name="dn_out",
    )(o, p, norm_g.reshape(1, DN_HEAD_DIM), w_out, xv, mods)
    return out.reshape(rows, d)


def _sg_out_kernel(u_ref, v_ref, lg_ref, lb_ref, ws_ref, bs_ref, w_ref, x_ref, gt_ref, out_ref, m_ref, *, ncol):
    v = v_ref[...]
    mu = jnp.mean(v, axis=-1, keepdims=True)
    var = jnp.mean(jnp.square(v - mu), axis=-1, keepdims=True)
    m_ref[...] = ((v - mu) * lax.rsqrt(var + EPS) * lg_ref[...] + lb_ref[...]).astype(BF16)
    tm, sg_dim = v.shape
    gd = sg_dim // SG_GROUPS
    for g in range(SG_GROUPS):
        wsg = ws_ref[g]
        bias = bs_ref[:, g:g + 1]
        for c in range(tm // SG_CHUNK):
            rs = slice(c * SG_CHUNK, (c + 1) * SG_CHUNK)
            cs = slice(g * gd, (g + 1) * gd)
            mixed = _dot(wsg, m_ref[rs, cs]) + bias
            m_ref[rs, cs] = (u_ref[rs, cs] * mixed).astype(BF16)
    y = _dot(m_ref[...], w_ref[...])
    _store_rows(out_ref, _load_rows(x_ref, ncol) + gt_ref[...] * y, ncol)


def _sg_out(uv, ln_g, ln_b, w_s, b_s, w_out, x, mods, bidx, tm, seq, ncol):
    rows, d = x.shape
    sg_dim = w_out.shape[0]
    xv, xspec = _row_view(x, tm, seq, ncol)
    vec = pl.BlockSpec((1, sg_dim), lambda i: (0, 0))
    out = pl.pallas_call(
        functools.partial(_sg_out_kernel, ncol=ncol),
        grid=(rows // tm,),
        in_specs=[pl.BlockSpec((tm, sg_dim), lambda i: (i, 0)),
                  pl.BlockSpec((tm, sg_dim), lambda i: (i, 1)),
                  vec, vec,
                  pl.BlockSpec((SG_GROUPS, SG_CHUNK, SG_CHUNK), lambda i: (0, 0, 0)),
                  pl.BlockSpec((SG_CHUNK, SG_GROUPS), lambda i: (0, 0)),
                  pl.BlockSpec((sg_dim, d), lambda i: (0, 0), pipeline_mode=pl.Buffered(1)),
                  xspec, _mod_spec(5, d, bidx)],
        out_specs=xspec,
        out_shape=jax.ShapeDtypeStruct(xv.shape, F32),
        scratch_shapes=[pltpu.VMEM((tm, sg_dim), BF16)],
        compiler_params=_cparams("parallel"),
        name="sg_out",
    )(uv, uv, ln_g.reshape(1, sg_dim), ln_b.reshape(1, sg_dim), w_s, b_s.T, w_out, xv, mods)
    return out.reshape(rows, d)


def kernel(x, c, ctx, c_ctx, mod_w, mod_b, norm_g, ffn_w_gu, ffn_w_down, dn_w_in, dn_conv_w, dn_a_log, dn_dt_bias,
           dn_norm_g, dn_w_out, sg_w_in, sg_ln_g, sg_ln_b, sg_w_s, sg_b_s, sg_w_out, final_norm_g):
    batch, seq, d = x.shape
    ctx_len = ctx.shape[1]
    depth = mod_w.shape[0]
    nm = mod_w.shape[2]
    hv = dn_a_log.shape[2]
    dk = DN_HEAD_DIM
    qk_dim = hv // 2 * dk
    v_dim = hv * dk
    conv_dim = 2 * qk_dim + v_dim
    assert batch + 1 <= MOD_ROWS

    xl = x.reshape(batch * seq, d)
    xc = ctx.reshape(batch * ctx_len, d)
    tm_l = min(512, seq)
    tm_c = min(512, batch * ctx_len)
    tiles_per_seq = seq // tm_l
    bidx_l = lambda i: i // tiles_per_seq
    bidx_c = lambda i: batch
    ncol_cm = tm_l // (seq // GRID_W)
    tm_f = tm_l
    bidx_f = lambda i, n=seq // tm_f: i // n

    cs = jnp.concatenate([c, c_ctx[None, :], jnp.zeros((MOD_ROWS - batch - 1, d), F32)], axis=0)
    mods_all = _modulation(cs, mod_w, mod_b).reshape(depth, MOD_ROWS, 1, nm)

    w_gu = ffn_w_gu.astype(BF16)
    w_down = ffn_w_down.astype(BF16)
    n_mix = 2

    for i in range(depth):
        last = i == depth - 1
        kind = i % n_mix
        j = i // n_mix
        ncol = ncol_cm if (j % 2) == 1 else 0
        need_ctx = not (last and kind == 1)
        mods = mods_all[i]

        ab = dict(merged=(i == 0), sem=("arbitrary", "arbitrary") if i == 1 else ("parallel", "arbitrary"))
        xl = _half_ffn(xl, mods, 0, bidx_f, norm_g[i, 0], w_gu, w_down, i, 0, tm_f, **ab)
        if need_ctx:
            xc = _half_ffn(xc, mods, 0, bidx_c, norm_g[i, 0], w_gu, w_down, i, 0, tm_c)

        if kind == 0:
            w_in = dn_w_in[j].astype(BF16)
            w_main, w_ab = w_in[:, :conv_dim + v_dim], w_in[:, conv_dim + v_dim:]
            w_out = dn_w_out[j].astype(BF16)
            pl_, ab_l = _prenorm_linear(xl, mods, 3, bidx_l, norm_g[i, 1], w_main, tm_l, seq, ncol, w_extra=w_ab,
                                        tn=2048 if i == 0 else 3072)
            pc_, ab_c = _prenorm_linear(xc, mods, 3, bidx_c, norm_g[i, 1], w_main, tm_c, ctx_len, 0, w_extra=w_ab)
            qkv_l = _dn_conv(pl_, dn_conv_w[j], seq, qk_dim, conv_dim, cb_long=256)
            qkv_c = _dn_conv(pc_, dn_conv_w[j], ctx_len, qk_dim, conv_dim)
            gc_l, gr_l = _dn_gates(ab_l, dn_a_log[j], dn_dt_bias[j], hv, tm_l)
            gc_c, gr_c = _dn_gates(ab_c, dn_a_log[j], dn_dt_bias[j], hv, tm_c)
            o_c, o_l = _dn_core(qkv_c, gc_c, gr_c, qkv_l, gc_l, gr_l, batch, hv,
                                trip_groups=_DN_TRIP_GROUPS)
            xl = _dn_out(o_l, pl_, conv_dim, dn_norm_g[j], w_out, xl, mods, bidx_l, tm_l, seq, ncol)
            if not last:
                xc = _dn_out(o_c, pc_, conv_dim, dn_norm_g[j], w_out, xc, mods, bidx_c, tm_c, ctx_len, 0)
        else:
            w_in = sg_w_in[j].astype(BF16)
            w_out = sg_w_out[j].astype(BF16)
            w_s = sg_w_s[j].astype(BF16)
            tm_sl = min(256, seq)
            tm_sc = min(256, batch * ctx_len)
            ncol_s = (tm_sl // (seq // GRID_W)) if ncol else 0
            bidx_sl = lambda i, n=seq // tm_sl: i // n
            uv_l = _prenorm_linear(xl, mods, 3, bidx_l, norm_g[i, 1], w_in, tm_l, seq, ncol, act="gelu")
            xl = _sg_out(uv_l, sg_ln_g[j], sg_ln_b[j], w_s, sg_b_s[j], w_out, xl, mods, bidx_sl, tm_sl, seq, ncol_s)
            if not last:
                uv_c = _prenorm_linear(xc, mods, 3, bidx_c, norm_g[i, 1], w_in, tm_c, ctx_len, 0, act="gelu")
                xc = _sg_out(uv_c, sg_ln_g[j], sg_ln_b[j], w_s, sg_b_s[j], w_out, xc, mods, bidx_c, tm_sc, ctx_len, 0)

        xl = _half_ffn(xl, mods, 6, bidx_f, norm_g[i, 2], w_gu, w_down, i, 1, tm_f,
                       final_g=final_norm_g if last else None, **ab)
        if not last:
            xc = _half_ffn(xc, mods, 6, bidx_c, norm_g[i, 2], w_gu, w_down, i, 1, tm_c)

    return xl.reshape(batch, seq, d)
```

```python
import functools

import numpy as np
import jax
import jax.numpy as jnp
from jax import lax
from jax.experimental import pallas as pl
from jax.experimental.pallas import tpu as pltpu

F32 = jnp.float32
BF16 = jnp.bfloat16
EPS = 1e-6
GRID_W = 64
DN_HEAD_DIM = 128
DN_CHUNK = 64
DN_CONV_K = 5
SG_CHUNK = 128
SG_GROUPS = 16

V7X_LANES = 128
V7X_SUBLANES = 8
V7X_VMEM_BYTES = 64 * 1024 * 1024
VMEM_LIMIT = V7X_VMEM_BYTES - 8 * 1024 * 1024
MOD_ROWS = 16
HIGHEST = lax.Precision.HIGHEST


def _cparams(*sem):
    return pltpu.CompilerParams(dimension_semantics=sem, vmem_limit_bytes=VMEM_LIMIT)


def _dot(a, b):
    return jnp.dot(a, b, preferred_element_type=F32)


def _dot_nt(a, b):
    return lax.dot_general(a, b, (((1,), (1,)), ((), ())), preferred_element_type=F32)


def _dot_tn(a, b):
    return lax.dot_general(a, b, (((0,), (0,)), ((), ())), preferred_element_type=F32)


def _rms(x):
    return x * lax.rsqrt(jnp.mean(x * x, axis=-1, keepdims=True) + EPS)


def _load_rows(ref, ncol):
    if not ncol:
        return ref[...]
    width = ref.shape[1] // ncol
    return jnp.concatenate([ref[:, k * width:(k + 1) * width] for k in range(ncol)], axis=0)


def _store_rows(ref, val, ncol):
    if not ncol:
        ref[...] = val
        return
    width = ref.shape[1] // ncol
    rows = ref.shape[0]
    for k in range(ncol):
        ref[:, k * width:(k + 1) * width] = val[k * rows:(k + 1) * rows, :]


def _row_view(arr, tm, seq, ncol):
    width = arr.shape[1]
    if not ncol:
        return arr, pl.BlockSpec((tm, width), lambda i, *_: (i, 0))
    grid_rows = seq // GRID_W
    per_batch = GRID_W // ncol
    view = arr.reshape(arr.shape[0] // seq, grid_rows, GRID_W * width)
    return view, pl.BlockSpec((None, grid_rows, ncol * width), lambda i, *_: (i // per_batch, 0, i % per_batch))


def _mod_spec(k, d, bidx):
    return pl.BlockSpec((None, 1, d), lambda i, *_: (bidx(i), 0, k))


def _mod_kernel(c_ref, w_ref, b_ref, o_ref):
    s = jax.nn.silu(c_ref[...]).astype(BF16)
    o_ref[...] = _dot(s, w_ref[...].astype(BF16)) + b_ref[...]


def _modulation(cs, mod_w, mod_b):
    depth, d, nm = mod_w.shape
    tn = 1024
    return pl.pallas_call(
        _mod_kernel,
        grid=(depth, nm // tn),
        in_specs=[pl.BlockSpec((MOD_ROWS, d), lambda l, j: (0, 0)),
                  pl.BlockSpec((None, d, tn), lambda l, j: (l, 0, j)),
                  pl.BlockSpec((None, 1, tn), lambda l, j: (l, 0, j))],
        out_specs=pl.BlockSpec((None, MOD_ROWS, tn), lambda l, j: (l, 0, j)),
        out_shape=jax.ShapeDtypeStruct((depth, MOD_ROWS, nm), F32),
        compiler_params=_cparams("parallel", "parallel"),
        name="modulation",
    )(cs, mod_w, mod_b.reshape(depth, 1, nm))


_FFN_CHUNK = 512


def _ffn_kernel(x_ref, ng_ref, sh_ref, sc_ref, gt_ref, wg_ref, wu_ref, wd_ref, *rest, final_norm):
    if final_norm:
        fg_ref, o_ref, xn_ref = rest
    else:
        o_ref, xn_ref = rest
    j = pl.program_id(1)

    @pl.when(j == 0)
    def _():
        gain = ng_ref[...] * (1.0 + sc_ref[...])
        xn_ref[...] = (_rms(x_ref[...]) * gain + sh_ref[...]).astype(BF16)
        o_ref[...] = jnp.zeros(o_ref.shape, F32)

    xn = xn_ref[...]
    g = _dot(xn, wg_ref[...])
    u = _dot(xn, wu_ref[...])
    h = (jax.nn.silu(g) * u).astype(BF16)
    o_ref[...] += _dot(h, wd_ref[...])

    @pl.when(j == pl.num_programs(1) - 1)
    def _():
        r = x_ref[...] + 0.5 * gt_ref[...] * o_ref[...]
        if final_norm:
            r = _rms(r) * fg_ref[...]
        o_ref[...] = r


def _half_ffn(x, mods, k0, bidx, norm_g, w_gu, w_down, layer, half, tm, final_g=None):
    rows, d = x.shape
    fc = _FFN_CHUNK
    nf = w_down.shape[2] // fc
    row = pl.BlockSpec((tm, d), lambda i, j: (i, 0))
    vec = pl.BlockSpec((1, d), lambda i, j: (0, 0))
    in_specs = [row, vec, _mod_spec(k0, d, bidx), _mod_spec(k0 + 1, d, bidx), _mod_spec(k0 + 2, d, bidx),
                pl.BlockSpec((None, None, d, fc), lambda i, j: (layer, half, 0, j)),
                pl.BlockSpec((None, None, d, fc), lambda i, j: (layer, half, 0, nf + j)),
                pl.BlockSpec((None, None, fc, d), lambda i, j: (layer, half, j, 0))]
    args = [x, norm_g.reshape(1, d), mods, mods, mods, w_gu, w_gu, w_down]
    if final_g is not None:
        in_specs.append(vec)
        args.append(final_g.reshape(1, d))
    return pl.pallas_call(
        functools.partial(_ffn_kernel, final_norm=final_g is not None),
        grid=(rows // tm, nf),
        in_specs=in_specs,
        out_specs=row,
        out_shape=jax.ShapeDtypeStruct((rows, d), F32),
        scratch_shapes=[pltpu.VMEM((tm, d), BF16)],
        compiler_params=_cparams("parallel", "arbitrary"),
        name="half_ffn",
    )(*args)


def _prenorm_linear_kernel(x_ref, ng_ref, sh_ref, sc_ref, w_ref, *rest, ncol, act, has_extra):
    if has_extra:
        wx_ref, o_ref, ox_ref, xn_ref = rest
    else:
        o_ref, xn_ref = rest

    @pl.when(pl.program_id(1) == 0)
    def _():
        gain = ng_ref[...] * (1.0 + sc_ref[...])
        xn = (_rms(_load_rows(x_ref, ncol)) * gain + sh_ref[...]).astype(BF16)
        xn_ref[...] = xn
        if has_extra:
            ox_ref[...] = _dot(xn, wx_ref[...])

    y = _dot(xn_ref[...], w_ref[...])
    if act == "gelu":
        y = 0.5 * y * (1.0 + lax.erf(y * (0.5 ** 0.5)))
    o_ref[...] = y.astype(o_ref.dtype)


def _widest_column_tile(n, d, tm):
    fixed = 2 * tm * d * 4 + tm * d * 2
    per_column = 2 * (d * 2 + tm * 4)
    limit = (VMEM_LIMIT - fixed) // per_column
    return max(t for t in range(V7X_LANES, n + 1, V7X_LANES) if n % t == 0 and t <= limit)


def _prenorm_linear(x, mods, k0, bidx, norm_g, w, tm, seq, ncol, act=None, w_extra=None):
    rows, d = x.shape
    n = w.shape[1]
    tn = _widest_column_tile(n, d, tm)
    xv, xspec = _row_view(x, tm, seq, ncol)
    vec = pl.BlockSpec((1, d), lambda i, j: (0, 0))
    in_specs = [xspec, vec, _mod_spec(k0, d, bidx), _mod_spec(k0 + 1, d, bidx),
                pl.BlockSpec((d, tn), lambda i, j: (0, j))]
    args = [xv, norm_g.reshape(1, d), mods, mods, w]
    out_specs = [pl.BlockSpec((tm, tn), lambda i, j: (i, j))]
    out_shape = [jax.ShapeDtypeStruct((rows, n), F32)]
    if w_extra is not None:
        nx = w_extra.shape[1]
        in_specs.append(pl.BlockSpec((d, nx), lambda i, j: (0, 0)))
        args.append(w_extra)
        out_specs.append(pl.BlockSpec((tm, nx), lambda i, j: (i, 0)))
        out_shape.append(jax.ShapeDtypeStruct((rows, nx), F32))
    outs = pl.pallas_call(
        functools.partial(_prenorm_linear_kernel, ncol=ncol, act=act, has_extra=w_extra is not None),
        grid=(rows // tm, n // tn),
        in_specs=in_specs,
        out_specs=out_specs,
        out_shape=out_shape,
        scratch_shapes=[pltpu.VMEM((tm, d), BF16)],
        compiler_params=_cparams("parallel", "arbitrary"),
        name="mixer_in_proj",
    )(*args)
    return outs if w_extra is not None else outs[0]


_CONV_PAD = V7X_SUBLANES


def _dn_conv_kernel(p_ref, cw_ref, o_ref, xp_ref, *, seq, sub, n_q, n_qk):
    c = pl.program_id(1)
    cb = p_ref.shape[1]
    half = DN_CONV_K // 2
    xp_ref[0:_CONV_PAD, :] = jnp.zeros((_CONV_PAD, cb), F32)
    xp_ref[_CONV_PAD:_CONV_PAD + seq, :] = p_ref[...]
    xp_ref[_CONV_PAD + seq:, :] = jnp.zeros((_CONV_PAD, cb), F32)
    w = cw_ref[...]

    def conv_tile(i):
        r0 = pl.multiple_of(i * sub, sub)
        win = xp_ref[pl.ds(r0, sub + 2 * _CONV_PAD), :]
        acc = None
        for t in range(DN_CONV_K):
            lo = _CONV_PAD - half + t
            term = win[lo:lo + sub, :] * w[t:t + 1, :]
            acc = term if acc is None else acc + term
        return r0, jax.nn.silu(acc)

    @pl.when(c < n_qk)
    def _():
        scale = jnp.where(c < n_q, DN_HEAD_DIM ** -0.5, 1.0).astype(F32)

        def body(i, carry):
            r0, y = conv_tile(i)
            for hh in range(cb // DN_HEAD_DIM):
                seg = y[:, hh * DN_HEAD_DIM:(hh + 1) * DN_HEAD_DIM]
                seg = seg * lax.rsqrt(jnp.sum(seg * seg, axis=-1, keepdims=True) + EPS)
                o_ref[pl.ds(r0, sub), hh * DN_HEAD_DIM:(hh + 1) * DN_HEAD_DIM] = (seg * scale).astype(o_ref.dtype)
            return carry

        lax.fori_loop(0, seq // sub, body, 0)

    @pl.when(c >= n_qk)
    def _():
        def body(i, carry):
            r0, y = conv_tile(i)
            o_ref[pl.ds(r0, sub), :] = y.astype(o_ref.dtype)
            return carry

        lax.fori_loop(0, seq // sub, body, 0)


def _dn_conv(p, conv_w, seq, qk_dim, conv_dim):
    rows = p.shape[0]
    cb = 256 if seq > 1024 else 1024
    sub = min(seq, 256)
    return pl.pallas_call(
        functools.partial(_dn_conv_kernel, seq=seq, sub=sub, n_q=qk_dim // cb, n_qk=2 * qk_dim // cb),
        grid=(rows // seq, conv_dim // cb),
        in_specs=[pl.BlockSpec((seq, cb), lambda s, c: (s, c)),
                  pl.BlockSpec((DN_CONV_K, cb), lambda s, c: (0, c))],
        out_specs=pl.BlockSpec((seq, cb), lambda s, c: (s, c)),
        out_shape=jax.ShapeDtypeStruct((rows, conv_dim), BF16),
        scratch_shapes=[pltpu.VMEM((seq + 2 * _CONV_PAD, cb), F32)],
        compiler_params=_cparams("parallel", "parallel"),
        name="dn_conv",
    )(p, conv_w)


def _dn_gate_kernel(ab_ref, alog_ref, dtb_ref, pm_ref, gc_ref, gr_ref, *, hv):
    tm, width = ab_ref.shape
    tile = V7X_LANES
    x = ab_ref[...]
    lane_full = lax.broadcasted_iota(jnp.int32, (tm, width), 1)
    lane = lax.broadcasted_iota(jnp.int32, (tile, width), 1)
    is_a = (lane % (2 * hv)) < hv
    is_rev = lane >= 2 * hv
    g = -jnp.exp(alog_ref[...]) * jax.nn.softplus(x + dtb_ref[...])
    raw = jnp.where((lane_full % (2 * hv)) < hv, g, jax.nn.sigmoid(x))
    ri = lax.broadcasted_iota(jnp.int32, (tile, tile), 0)
    ci = lax.broadcasted_iota(jnp.int32, (tile, tile), 1)
    same = (ri // DN_CHUNK) == (ci // DN_CHUNK)
    low = jnp.where(same & (ci <= ri), 1.0, 0.0).astype(F32)
    upp = jnp.where(same & (ci >= ri), 1.0, 0.0).astype(F32)
    for t in range(tm // tile):
        blk = raw[t * tile:(t + 1) * tile, :]
        pre = jnp.dot(low, blk, precision=HIGHEST, preferred_element_type=F32)
        suf = jnp.dot(upp, blk, precision=HIGHEST, preferred_element_type=F32)
        out = jnp.where(is_a, jnp.where(is_rev, suf, pre), blk)
        gc_ref[t * tile:(t + 1) * tile, :] = out
        gr_ref[t] = lax.dot_general(pm_ref[...], out, (((1,), (1,)), ((), ())),
                                    precision=HIGHEST, preferred_element_type=F32)


def _gate_perm(hv):
    pm = np.zeros((4 * hv, 4 * hv), np.float32)
    for hq in range(hv // 2):
        for d in range(2):
            for isb in range(2):
                for j in range(2):
                    pm[hq * 8 + d * 4 + isb * 2 + j, d * 2 * hv + isb * hv + 2 * hq + j] = 1.0
    return jnp.asarray(pm)


def _dn_gates(ab, a_log, dt_bias, hv, tm):
    rows, width = ab.shape
    assert width == 4 * hv == V7X_LANES
    zeros = jnp.zeros_like(a_log)
    alog = jnp.concatenate([a_log, zeros], axis=1).reshape(1, width)
    dtb = jnp.concatenate([dt_bias, zeros], axis=1).reshape(1, width)
    vec = pl.BlockSpec((1, width), lambda i: (0, 0))
    return pl.pallas_call(
        functools.partial(_dn_gate_kernel, hv=hv),
        grid=(rows // tm,),
        in_specs=[pl.BlockSpec((tm, width), lambda i: (i, 0)), vec, vec,
                  pl.BlockSpec((width, width), lambda i: (0, 0))],
        out_specs=[pl.BlockSpec((tm, width), lambda i: (i, 0)),
                   pl.BlockSpec((tm // V7X_LANES, width, V7X_LANES), lambda i: (i, 0, 0))],
        out_shape=[jax.ShapeDtypeStruct((rows, width), F32),
                   jax.ShapeDtypeStruct((rows // V7X_LANES, width, V7X_LANES), F32)],
        compiler_params=_cparams("parallel"),
        name="dn_gates",
    )(ab, alog, dtb, _gate_perm(hv))


_INV_BLOCK = 4
_DN_GROUP_TILES = 2
_DN_TRIP_GROUPS = 8
_DN_HEADS_PER_STEP = 1


def _mm_bf16(ps, qs):
    return [_dot(p.astype(BF16), q.astype(BF16)) for p, q in zip(ps, qs)]


def _pair_diag(x, lo_half):
    zero = jnp.zeros_like(x)
    return jnp.concatenate([jnp.where(lo_half, x, zero), jnp.where(lo_half, zero, x)], axis=0)


def _unit_tri_inverse(mats, ri, ci, lo_half):
    eye = (ri == ci).astype(F32)
    bi, bj = ri // _INV_BLOCK, ci // _INV_BLOCK
    n = mats[0].shape[0]

    def mm(ps, qs):
        return _mm_bf16(ps, [_pair_diag(q, lo_half) for q in qs])

    ds = [jnp.where(bi == bj, a, 0.0) for a in mats]
    ts = [eye - d for d in ds]
    ps = mm(ds, ds)
    yield
    power = 2
    while 2 * power < _INV_BLOCK:
        both = mm([jnp.concatenate([t, p], axis=0) for t, p in zip(ts, ps)], ps)
        ts = [t + b[:n, :] for t, b in zip(ts, both)]
        ps = [b[n:, :] for b in both]
        power *= 2
        yield
    ts = [t + tp for t, tp in zip(ts, mm(ts, ps))]
    yield
    w = 1
    while w * _INV_BLOCK < n:
        off = (bi // (2 * w) == bj // (2 * w)) & (bi // w != bj // w)
        tes = mm(ts, [jnp.where(off, a, 0.0) for a in mats])
        yield
        ts = [t - tet for t, tet in zip(ts, mm(tes, ts))]
        yield
        w *= 2
    return ts


def _issue_pipelined(n_groups, hop, make_head, make_tail):
    heads, values = {}, {}
    tail, tail_g, tick = None, 0, 0
    while tail_g < n_groups:
        if tick % hop == 0 and tick // hop < n_groups:
            heads[tick // hop] = make_head(tick // hop)
        for g in sorted(heads):
            try:
                next(heads[g])
            except StopIteration as stop:
                values[g] = stop.value
                del heads[g]
        while tail_g < n_groups:
            if tail is None:
                if tail_g not in values:
                    break
                tail = make_tail(values.pop(tail_g))
            try:
                next(tail)
                break
            except StopIteration:
                tail, tail_g = None, tail_g + 1
        tick += 1


def _dn_core_kernel(qc_ref, kc_ref, vc_ref, gcc_ref, grc_ref, ql_ref, kl_ref, vl_ref, gcl_ref, grl_ref,
                    oc_ref, ol_ref, s_ref, *, hv):
    dk = DN_HEAD_DIM
    cs = DN_CHUNK
    tile = V7X_LANES
    n_h = _DN_HEADS_PER_STEP
    n_streams = 4 * n_h
    s_ref[...] = jnp.zeros(s_ref.shape, F32)
    oc_ref[...] = jnp.zeros(oc_ref.shape, F32)
    ol_ref[...] = jnp.zeros(ol_ref.shape, F32)
    ri = lax.broadcasted_iota(jnp.int32, (cs, 2 * cs), 0)
    lane = lax.broadcasted_iota(jnp.int32, (cs, 2 * cs), 1)
    shifts = [(tile - 2 * (n_h * pl.program_id(1) + hh)) % tile for hh in range(n_h)]

    n_sub = tile // cs
    lo_half = lane < cs
    ci = jnp.where(lo_half, lane, lane - cs)
    incl = (ri >= ci, ri <= ci)
    strict = (ri > ci, ri < ci)
    last = (cs - 1, 0)

    def state_free(refs, t_fwd, t_bwd):
        q_ref, k_ref, v_ref, gc_ref, gr_ref, _ = refs
        chunks = ([(0, t, c) for t in t_fwd for c in range(n_sub)]
                  + [(1, t, c) for t in t_bwd for c in reversed(range(n_sub))])
        r0s = [pl.multiple_of(t * tile, tile) + c * cs for _, t, c in chunks]
        cells = [(ic, hh) for ic in range(len(chunks)) for hh in range(n_h)]
        cdir = [chunks[ic][0] for ic, _ in cells]
        kbs = [k_ref[pl.ds(r0s[ic], cs), hh * dk:(hh + 1) * dk] for ic, hh in cells]
        qbs = [q_ref[pl.ds(r0s[ic], cs), hh * dk:(hh + 1) * dk] for ic, hh in cells]
        qkks = [_dot_nt(jnp.concatenate([qb, kb], axis=0), jnp.concatenate([kb, kb], axis=0))
                for qb, kb in zip(qbs, kbs)]
        gcts = [pltpu.roll(gc_ref[pl.ds(r0s[ic], cs), :], shifts[hh], 1) for ic, hh in cells]
        grts = [gr_ref[chunks[ic][1], hh * 8:(hh + 1) * 8, :] for ic, hh in cells]
        yield

        def packed_cols(m, off):
            base = cdir[m] * 2 * hv + off
            return jnp.where(lo_half, gcts[m][:, base:base + 1], gcts[m][:, base + 1:base + 2])

        def packed_row(m, off):
            d, _, c = chunks[cells[m][0]]
            r0_, r1_ = grts[m][d * 4 + off:d * 4 + off + 1, :], grts[m][d * 4 + off + 1:d * 4 + off + 2, :]
            if c == 0:
                return jnp.where(lo_half[:1], r0_, pltpu.roll(r1_, cs, 1))
            return jnp.where(lo_half[:1], pltpu.roll(r0_, cs, 1), r1_)

        n_cells = len(cells)
        dec2s = [jnp.exp(jnp.where(incl[cdir[m]], packed_cols(m, 0) - packed_row(m, 0), -jnp.inf))
                 for m in range(n_cells)]
        amats = [jnp.where(strict[cdir[m]], qkks[m][cs:, :] * dec2s[m] * packed_cols(m, hv), 0.0)
                 for m in range(n_cells)]
        qkms = [(qkks[m][:cs, :] * dec2s[m]).astype(BF16) for m in range(n_cells)]
        yield
        tinvs = yield from _unit_tri_inverse(amats, ri, ci, lo_half)
        tinvs = [t.astype(BF16) for t in tinvs]

        scs = [(m, j) for m in range(n_cells) for j in range(2)]
        dirs = [cdir[m] for m, _ in scs]
        gcols = [gcts[m][:, cdir[m] * 2 * hv + j:cdir[m] * 2 * hv + j + 1] for m, j in scs]
        bcols = [gcts[m][:, cdir[m] * 2 * hv + hv + j:cdir[m] * 2 * hv + hv + j + 1] for m, j in scs]
        vcols = [(cells[m][1] * 2 + j) * dk for m, j in scs]
        rows = [r0s[cells[m][0]] for m, _ in scs]
        ks = [kb.astype(F32) for kb in kbs]

        def head_rows(x, j):
            zero = jnp.zeros_like(x)
            return jnp.concatenate([x, zero] if j == 0 else [zero, x], axis=0)

        rhs = [jnp.concatenate([ks[m] * (bc * jnp.exp(gc)), v_ref[pl.ds(r0, cs), vc:vc + dk] * bc],
                               axis=1).astype(BF16)
               for (m, _), bc, gc, r0, vc in zip(scs, bcols, gcols, rows, vcols)]
        wus = [_dot(tinvs[m], head_rows(r, j)).astype(BF16) for (m, j), r in zip(scs, rhs)]
        yield
        qwus = [_dot(qkms[m], head_rows(wu, j)) for (m, j), wu in zip(scs, wus)]
        g_lasts = [gc[last[d]:last[d] + 1, :] for d, gc in zip(dirs, gcols)]
        kwus = [_dot_tn((ks[m] * jnp.exp(gl - gc)).astype(BF16), wu)
                for (m, _), gl, gc, wu in zip(scs, g_lasts, gcols, wus)]
        yield
        lhs = [jnp.concatenate([kwu[:, :dk].astype(BF16), (qbs[m] * jnp.exp(gc) - qwu[:, :dk]).astype(BF16)], axis=0)
               for (m, _), gc, qwu, kwu in zip(scs, gcols, qwus, kwus)]
        yield
        return dict(rows=list(zip(rows, vcols)), lhs=lhs, add=[kwu[:, dk:] for kwu in kwus],
                    intra=[qwu[:, dk:] for qwu in qwus], decay=[jnp.exp(gl) for gl in g_lasts])

    def recurrence(refs, pre, states):
        o_ref = refs[5]
        n_steps = len(pre["lhs"]) // n_streams
        for step in range(n_steps):
            idx = [((d * n_steps + step) * n_h + hh) * 2 + j for d in range(2) for hh in range(n_h) for j in range(2)]
            rs = [_dot(pre["lhs"][i], st.astype(BF16)) for i, st in zip(idx, states)]
            states[:] = [st * pre["decay"][i] + pre["add"][i] - r[:dk, :] for i, st, r in zip(idx, states, rs)]
            for i, r in zip(idx, rs):
                r0, vc = pre["rows"][i]
                o_ref[pl.ds(r0, cs), vc:vc + dk] += r[dk:, :] + pre["intra"][i]
            yield

    def run(refs, n_tiles):
        group = max(p for p in range(1, _DN_GROUP_TILES + 1) if n_tiles % p == 0)
        n_groups = n_tiles // group
        per_trip = max(p for p in range(1, _DN_TRIP_GROUPS + 1) if n_groups % p == 0)
        hop = group * n_sub

        def tiles(g):
            return [g * group + m for m in range(group)], [n_tiles - 1 - g * group - m for m in range(group)]

        def body(i, carry):
            states = [s_ref[si] for si in range(n_streams)]
            _issue_pipelined(per_trip, hop,
                             lambda g: state_free(refs, *tiles(i * per_trip + g)),
                             lambda pre: recurrence(refs, pre, states))
            for si in range(n_streams):
                s_ref[si] = states[si]
            return carry

        lax.fori_loop(0, n_groups // per_trip, body, 0)

    run((qc_ref, kc_ref, vc_ref, gcc_ref, grc_ref, oc_ref), qc_ref.shape[0] // tile)
    run((ql_ref, kl_ref, vl_ref, gcl_ref, grl_ref, ol_ref), ql_ref.shape[0] // tile)


def _dn_core(qkv_c, gc_c, gr_c, qkv_l, gc_l, gr_l, batch, hv):
    dk = DN_HEAD_DIM
    n_h = _DN_HEADS_PER_STEP
    steps = hv // 2 // n_h
    ctx_len = qkv_c.shape[0] // batch
    seq = qkv_l.shape[0] // batch
    width = gc_l.shape[1]

    def specs(t):
        return [pl.BlockSpec((t, n_h * dk), lambda b, h: (b, h)),
                pl.BlockSpec((t, n_h * dk), lambda b, h: (b, steps + h)),
                pl.BlockSpec((t, 2 * n_h * dk), lambda b, h: (b, steps + h)),
                pl.BlockSpec((t, width), lambda b, h: (b, 0)),
                pl.BlockSpec((t // V7X_LANES, 8 * n_h, V7X_LANES), lambda b, h: (b, h, 0))]

    return pl.pallas_call(
        functools.partial(_dn_core_kernel, hv=hv),
        grid=(batch, steps),
        in_specs=specs(ctx_len) + specs(seq),
        out_specs=[pl.BlockSpec((ctx_len, 2 * n_h * dk), lambda b, h: (b, h)),
                   pl.BlockSpec((seq, 2 * n_h * dk), lambda b, h: (b, h))],
        out_shape=[jax.ShapeDtypeStruct((batch * ctx_len, hv * dk), F32),
                   jax.ShapeDtypeStruct((batch * seq, hv * dk), F32)],
        scratch_shapes=[pltpu.VMEM((4 * n_h, dk, dk), F32)],
        compiler_params=_cparams("parallel", "parallel"),
        name="dn_core",
    )(qkv_c, qkv_c, qkv_c, gc_c, gr_c, qkv_l, qkv_l, qkv_l, gc_l, gr_l)


def _dn_out_kernel(o_ref, z_ref, ng_ref, w_ref, x_ref, gt_ref, out_ref, acc_ref, *, ncol):
    k = pl.program_id(1)

    @pl.when(k == 0)
    def _():
        acc_ref[...] = jnp.zeros(acc_ref.shape, F32)

    o = o_ref[...]
    dk = DN_HEAD_DIM
    segs = [_rms(o[:, h * dk:(h + 1) * dk]) * ng_ref[...] for h in range(o.shape[1] // dk)]
    on = jnp.concatenate(segs, axis=1) * jax.nn.silu(z_ref[...])
    acc_ref[...] += _dot(on.astype(BF16), w_ref[...])

    @pl.when(k == pl.num_programs(1) - 1)
    def _():
        _store_rows(out_ref, _load_rows(x_ref, ncol) + gt_ref[...] * acc_ref[...], ncol)


def _dn_out(o, p, z_col0, norm_g, w_out, x, mods, bidx, tm, seq, ncol):
    rows, d = x.shape
    kdim = w_out.shape[0]
    tk = 1024
    xv, xspec = _row_view(x, tm, seq, ncol)
    zb = z_col0 // tk
    out = pl.pallas_call(
        functools.partial(_dn_out_kernel, ncol=ncol),
        grid=(rows // tm, kdim // tk),
        in_specs=[pl.BlockSpec((tm, tk), lambda i, k: (i, k)),
                  pl.BlockSpec((tm, tk), lambda i, k: (i, zb + k)),
                  pl.BlockSpec((1, DN_HEAD_DIM), lambda i, k: (0, 0)),
                  pl.BlockSpec((tk, d), lambda i, k: (k, 0)),
                  xspec, _mod_spec(5, d, bidx)],
        out_specs=xspec,
        out_shape=jax.ShapeDtypeStruct(xv.shape, F32),
        scratch_shapes=[pltpu.VMEM((tm, d), F32)],
        compiler_params=_cparams("parallel", "arbitrary"),
        name="dn_out",
    )(o, p, norm_g.reshape(1, DN_HEAD_DIM), w_out, xv, mods)
    return out.reshape(rows, d)


def _sg_out_kernel(u_ref, v_ref, lg_ref, lb_ref, ws_ref, bs_ref, w_ref, x_ref, gt_ref, out_ref, m_ref, *, ncol):
    v = v_ref[...]
    mu = jnp.mean(v, axis=-1, keepdims=True)
    var = jnp.mean(jnp.square(v - mu), axis=-1, keepdims=True)
    m_ref[...] = ((v - mu) * lax.rsqrt(var + EPS) * lg_ref[...] + lb_ref[...]).astype(BF16)
    tm, sg_dim = v.shape
    gd = sg_dim // SG_GROUPS
    for g in range(SG_GROUPS):
        wsg = ws_ref[g]
        bias = bs_ref[:, g:g + 1]
        for c in range(tm // SG_CHUNK):
            rs = slice(c * SG_CHUNK, (c + 1) * SG_CHUNK)
            cs = slice(g * gd, (g + 1) * gd)
            mixed = _dot(wsg, m_ref[rs, cs]) + bias
            m_ref[rs, cs] = (u_ref[rs, cs] * mixed).astype(BF16)
    y = _dot(m_ref[...], w_ref[...])
    _store_rows(out_ref, _load_rows(x_ref, ncol) + gt_ref[...] * y, ncol)


def _sg_out(uv, ln_g, ln_b, w_s, b_s, w_out, x, mods, bidx, tm, seq, ncol):
    rows, d = x.shape
    sg_dim = w_out.shape[0]
    xv, xspec = _row_view(x, tm, seq, ncol)
    vec = pl.BlockSpec((1, sg_dim), lambda i: (0, 0))
    out = pl.pallas_call(
        functools.partial(_sg_out_kernel, ncol=ncol),
        grid=(rows // tm,),
        in_specs=[pl.BlockSpec((tm, sg_dim), lambda i: (i, 0)),
                  pl.BlockSpec((tm, sg_dim), lambda i: (i, 1)),
                  vec, vec,
                  pl.BlockSpec((SG_GROUPS, SG_CHUNK, SG_CHUNK), lambda i: (0, 0, 0)),
                  pl.BlockSpec((SG_CHUNK, SG_GROUPS), lambda i: (0, 0)),
                  pl.BlockSpec((sg_dim, d), lambda i: (0, 0), pipeline_mode=pl.Buffered(1)),
                  xspec, _mod_spec(5, d, bidx)],
        out_specs=xspec,
        out_shape=jax.ShapeDtypeStruct(xv.shape, F32),
        scratch_shapes=[pltpu.VMEM((tm, sg_dim), BF16)],
        compiler_params=_cparams("parallel"),
        name="sg_out",
    )(uv, uv, ln_g.reshape(1, sg_dim), ln_b.reshape(1, sg_dim), w_s, b_s.T, w_out, xv, mods)
    return out.reshape(rows, d)


def kernel(x, c, ctx, c_ctx, mod_w, mod_b, norm_g, ffn_w_gu, ffn_w_down, dn_w_in, dn_conv_w, dn_a_log, dn_dt_bias,
           dn_norm_g, dn_w_out, sg_w_in, sg_ln_g, sg_ln_b, sg_w_s, sg_b_s, sg_w_out, final_norm_g):
    batch, seq, d = x.shape
    ctx_len = ctx.shape[1]
    depth = mod_w.shape[0]
    nm = mod_w.shape[2]
    hv = dn_a_log.shape[2]
    dk = DN_HEAD_DIM
    qk_dim = hv // 2 * dk
    v_dim = hv * dk
    conv_dim = 2 * qk_dim + v_dim
    assert batch + 1 <= MOD_ROWS

    xl = x.reshape(batch * seq, d)
    xc = ctx.reshape(batch * ctx_len, d)
    tm_l = min(512, seq)
    tm_c = min(512, batch * ctx_len)
    tiles_per_seq = seq // tm_l
    bidx_l = lambda i: i // tiles_per_seq
    bidx_c = lambda i: batch
    ncol_cm = tm_l // (seq // GRID_W)

    cs = jnp.concatenate([c, c_ctx[None, :], jnp.zeros((MOD_ROWS - batch - 1, d), F32)], axis=0)
    mods_all = _modulation(cs, mod_w, mod_b).reshape(depth, MOD_ROWS, 1, nm)

    w_gu = ffn_w_gu.astype(BF16)
    w_down = ffn_w_down.astype(BF16)
    n_mix = 2

    for i in range(depth):
        last = i == depth - 1
        kind = i % n_mix
        j = i // n_mix
        ncol = ncol_cm if (j % 2) == 1 else 0
        need_ctx = not (last and kind == 1)
        mods = mods_all[i]

        xl = _half_ffn(xl, mods, 0, bidx_l, norm_g[i, 0], w_gu, w_down, i, 0, tm_l)
        if need_ctx:
            xc = _half_ffn(xc, mods, 0, bidx_c, norm_g[i, 0], w_gu, w_down, i, 0, tm_c)

        if kind == 0:
            w_in = dn_w_in[j].astype(BF16)
            w_main, w_ab = w_in[:, :conv_dim + v_dim], w_in[:, conv_dim + v_dim:]
            w_out = dn_w_out[j].astype(BF16)
            pl_, ab_l = _prenorm_linear(xl, mods, 3, bidx_l, norm_g[i, 1], w_main, tm_l, seq, ncol, w_extra=w_ab)
            pc_, ab_c = _prenorm_linear(xc, mods, 3, bidx_c, norm_g[i, 1], w_main, tm_c, ctx_len, 0, w_extra=w_ab)
            qkv_l = _dn_conv(pl_, dn_conv_w[j], seq, qk_dim, conv_dim)
            qkv_c = _dn_conv(pc_, dn_conv_w[j], ctx_len, qk_dim, conv_dim)
            gc_l, gr_l = _dn_gates(ab_l, dn_a_log[j], dn_dt_bias[j], hv, tm_l)
            gc_c, gr_c = _dn_gates(ab_c, dn_a_log[j], dn_dt_bias[j], hv, tm_c)
            o_c, o_l = _dn_core(qkv_c, gc_c, gr_c, qkv_l, gc_l, gr_l, batch, hv)
            xl = _dn_out(o_l, pl_, conv_dim, dn_norm_g[j], w_out, xl, mods, bidx_l, tm_l, seq, ncol)
            if not last:
                xc = _dn_out(o_c, pc_, conv_dim, dn_norm_g[j], w_out, xc, mods, bidx_c, tm_c, ctx_len, 0)
        else:
            w_in = sg_w_in[j].astype(BF16)
            w_out = sg_w_out[j].astype(BF16)
            w_s = sg_w_s[j].astype(BF16)
            tm_sl = min(256, seq)
            tm_sc = min(256, batch * ctx_len)
            ncol_s = (tm_sl // (seq // GRID_W)) if ncol else 0
            bidx_sl = lambda i, n=seq // tm_sl: i // n
            uv_l = _prenorm_linear(xl, mods, 3, bidx_l, norm_g[i, 1], w_in, tm_l, seq, ncol, act="gelu")
            xl = _sg_out(uv_l, sg_ln_g[j], sg_ln_b[j], w_s, sg_b_s[j], w_out, xl, mods, bidx_sl, tm_sl, seq, ncol_s)
            if not last:
                uv_c = _prenorm_linear(xc, mods, 3, bidx_c, norm_g[i, 1], w_in, tm_c, ctx_len, 0, act="gelu")
                xc = _sg_out(uv_c, sg_ln_g[j], sg_ln_b[j], w_s, sg_b_s[j], w_out, xc, mods, bidx_c, tm_sc, ctx_len, 0)

        xl = _half_ffn(xl, mods, 6, bidx_l, norm_g[i, 2], w_gu, w_down, i, 1, tm_l,
                       final_g=final_norm_g if last else None)
        if not last:
            xc = _half_ffn(xc, mods, 6, bidx_c, norm_g[i, 2], w_gu, w_down, i, 1, tm_c)

    return xl.reshape(batch, seq, d)
```

```python
import functools

import numpy as np
import jax
import jax.numpy as jnp
from jax import lax
from jax.experimental import pallas as pl
from jax.experimental.pallas import tpu as pltpu

F32 = jnp.float32
BF16 = jnp.bfloat16
EPS = 1e-6
GRID_W = 64
DN_HEAD_DIM = 128
DN_CHUNK = 64
DN_CONV_K = 5
SG_CHUNK = 128
SG_GROUPS = 16

V7X_LANES = 128
V7X_SUBLANES = 8
V7X_VMEM_BYTES = 64 * 1024 * 1024
VMEM_LIMIT = V7X_VMEM_BYTES - 8 * 1024 * 1024
MOD_ROWS = 16
HIGHEST = lax.Precision.HIGHEST


def _cparams(*sem):
    return pltpu.CompilerParams(dimension_semantics=sem, vmem_limit_bytes=VMEM_LIMIT)


def _dot(a, b):
    return jnp.dot(a, b, preferred_element_type=F32)


def _dot_nt(a, b):
    return lax.dot_general(a, b, (((1,), (1,)), ((), ())), preferred_element_type=F32)


def _dot_tn(a, b):
    return lax.dot_general(a, b, (((0,), (0,)), ((), ())), preferred_element_type=F32)


def _rms(x):
    return x * lax.rsqrt(jnp.mean(x * x, axis=-1, keepdims=True) + EPS)


def _load_rows(ref, ncol):
    if not ncol:
        return ref[...]
    width = ref.shape[1] // ncol
    return jnp.concatenate([ref[:, k * width:(k + 1) * width] for k in range(ncol)], axis=0)


def _store_rows(ref, val, ncol):
    if not ncol:
        ref[...] = val
        return
    width = ref.shape[1] // ncol
    rows = ref.shape[0]
    for k in range(ncol):
        ref[:, k * width:(k + 1) * width] = val[k * rows:(k + 1) * rows, :]


def _row_view(arr, tm, seq, ncol):
    if not ncol:
        return arr, pl.BlockSpec((tm, arr.shape[1]), lambda i, *_: (i, 0))
    grid_rows = seq // GRID_W
    per_batch = GRID_W // ncol
    if arr.ndim == 3:
        view, width = arr, arr.shape[2] // GRID_W
    else:
        width = arr.shape[1]
        view = arr.reshape(arr.shape[0] // seq, grid_rows, GRID_W * width)
    return view, pl.BlockSpec((None, grid_rows, ncol * width), lambda i, *_: (i // per_batch, 0, i % per_batch))


def _mod_spec(k, d, bidx):
    return pl.BlockSpec((None, 1, d), lambda i, *_: (bidx(i), 0, k))


def _mod_kernel(c_ref, w_ref, b_ref, o_ref):
    s = jax.nn.silu(c_ref[...]).astype(BF16)
    o_ref[...] = _dot(s, w_ref[...].astype(BF16)) + b_ref[...]


def _modulation(cs, mod_w, mod_b):
    depth, d, nm = mod_w.shape
    tn = 1024
    return pl.pallas_call(
        _mod_kernel,
        grid=(depth, nm // tn),
        in_specs=[pl.BlockSpec((MOD_ROWS, d), lambda l, j: (0, 0)),
                  pl.BlockSpec((None, d, tn), lambda l, j: (l, 0, j)),
                  pl.BlockSpec((None, 1, tn), lambda l, j: (l, 0, j))],
        out_specs=pl.BlockSpec((None, MOD_ROWS, tn), lambda l, j: (l, 0, j)),
        out_shape=jax.ShapeDtypeStruct((depth, MOD_ROWS, nm), F32),
        compiler_params=_cparams("parallel", "parallel"),
        name="modulation",
    )(cs, mod_w, mod_b.reshape(depth, 1, nm))


_FFN_CHUNK = 512


def _ffn_kernel(x_ref, ng_ref, sh_ref, sc_ref, gt_ref, wg_ref, wu_ref, wd_ref, *rest, final_norm, ncol):
    if final_norm:
        fg_ref, *rest = rest
    o_ref, xn_ref, acc_ref = rest if ncol else (*rest, rest[0])
    j = pl.program_id(1)

    @pl.when(j == 0)
    def _():
        gain = ng_ref[...] * (1.0 + sc_ref[...])
        xn_ref[...] = (_rms(_load_rows(x_ref, ncol)) * gain + sh_ref[...]).astype(BF16)
        acc_ref[...] = jnp.zeros(acc_ref.shape, F32)

    xn = xn_ref[...]
    g = _dot(xn, wg_ref[...])
    u = _dot(xn, wu_ref[...])
    h = (jax.nn.silu(g) * u).astype(BF16)
    acc_ref[...] += _dot(h, wd_ref[...])

    @pl.when(j == pl.num_programs(1) - 1)
    def _():
        r = _load_rows(x_ref, ncol) + 0.5 * gt_ref[...] * acc_ref[...]
        if final_norm:
            r = _rms(r) * fg_ref[...]
        _store_rows(o_ref, r, ncol)


def _half_ffn(x, mods, k0, bidx, norm_g, w_gu, w_down, layer, half, tm, seq, ncol, final_g=None):
    d = w_down.shape[3]
    rows = x.size // d
    fc = _FFN_CHUNK
    nf = w_down.shape[2] // fc
    _, row = _row_view(x, tm, seq, ncol)
    vec = pl.BlockSpec((1, d), lambda i, j: (0, 0))
    in_specs = [row, vec, _mod_spec(k0, d, bidx), _mod_spec(k0 + 1, d, bidx), _mod_spec(k0 + 2, d, bidx),
                pl.BlockSpec((None, None, d, fc), lambda i, j: (layer, half, 0, j)),
                pl.BlockSpec((None, None, d, fc), lambda i, j: (layer, half, 0, nf + j)),
                pl.BlockSpec((None, None, fc, d), lambda i, j: (layer, half, j, 0))]
    args = [x, norm_g.reshape(1, d), mods, mods, mods, w_gu, w_gu, w_down]
    if final_g is not None:
        in_specs.append(vec)
        args.append(final_g.reshape(1, d))
    return pl.pallas_call(
        functools.partial(_ffn_kernel, final_norm=final_g is not None, ncol=ncol),
        grid=(rows // tm, nf),
        in_specs=in_specs,
        out_specs=row,
        out_shape=jax.ShapeDtypeStruct(x.shape, F32),
        scratch_shapes=[pltpu.VMEM((tm, d), BF16)] + ([pltpu.VMEM((tm, d), F32)] if ncol else []),
        compiler_params=_cparams("parallel", "arbitrary"),
        name="half_ffn",
    )(*args)


def _prenorm_linear_kernel(x_ref, ng_ref, sh_ref, sc_ref, w_ref, *rest, ncol, act, has_extra):
    if has_extra:
        wx_ref, o_ref, ox_ref, xn_ref = rest
    else:
        o_ref, xn_ref = rest

    @pl.when(pl.program_id(1) == 0)
    def _():
        gain = ng_ref[...] * (1.0 + sc_ref[...])
        xn = (_rms(_load_rows(x_ref, ncol)) * gain + sh_ref[...]).astype(BF16)
        xn_ref[...] = xn
        if has_extra:
            ox_ref[...] = _dot(xn, wx_ref[...])

    y = _dot(xn_ref[...], w_ref[...])
    if act == "gelu":
        y = 0.5 * y * (1.0 + lax.erf(y * (0.5 ** 0.5)))
    o_ref[...] = y.astype(o_ref.dtype)


def _widest_column_tile(n, d, tm):
    fixed = 2 * tm * d * 4 + tm * d * 2
    per_column = 2 * (d * 2 + tm * 4)
    limit = (VMEM_LIMIT - fixed) // per_column
    return max(t for t in range(V7X_LANES, n + 1, V7X_LANES) if n % t == 0 and t <= limit)


def _prenorm_linear(x, mods, k0, bidx, norm_g, w, tm, seq, ncol, act=None, w_extra=None):
    d = w.shape[0]
    rows = x.size // d
    n = w.shape[1]
    tn = _widest_column_tile(n, d, tm)
    xv, xspec = _row_view(x, tm, seq, ncol)
    vec = pl.BlockSpec((1, d), lambda i, j: (0, 0))
    in_specs = [xspec, vec, _mod_spec(k0, d, bidx), _mod_spec(k0 + 1, d, bidx),
                pl.BlockSpec((d, tn), lambda i, j: (0, j))]
    args = [xv, norm_g.reshape(1, d), mods, mods, w]
    out_specs = [pl.BlockSpec((tm, tn), lambda i, j: (i, j))]
    out_shape = [jax.ShapeDtypeStruct((rows, n), F32)]
    if w_extra is not None:
        nx = w_extra.shape[1]
        in_specs.append(pl.BlockSpec((d, nx), lambda i, j: (0, 0)))
        args.append(w_extra)
        out_specs.append(pl.BlockSpec((tm, nx), lambda i, j: (i, 0)))
        out_shape.append(jax.ShapeDtypeStruct((rows, nx), F32))
    outs = pl.pallas_call(
        functools.partial(_prenorm_linear_kernel, ncol=ncol, act=act, has_extra=w_extra is not None),
        grid=(rows // tm, n // tn),
        in_specs=in_specs,
        out_specs=out_specs,
        out_shape=out_shape,
        scratch_shapes=[pltpu.VMEM((tm, d), BF16)],
        compiler_params=_cparams("parallel", "arbitrary"),
        name="mixer_in_proj",
    )(*args)
    return outs if w_extra is not None else outs[0]


_CONV_PAD = V7X_SUBLANES


def _dn_conv_kernel(p_ref, cw_ref, o_ref, xp_ref, *, seq, sub, n_q, n_qk):
    c = pl.program_id(1)
    cb = p_ref.shape[1]
    half = DN_CONV_K // 2
    xp_ref[0:_CONV_PAD, :] = jnp.zeros((_CONV_PAD, cb), F32)
    xp_ref[_CONV_PAD:_CONV_PAD + seq, :] = p_ref[...]
    xp_ref[_CONV_PAD + seq:, :] = jnp.zeros((_CONV_PAD, cb), F32)
    w = cw_ref[...]

    def conv_tile(i):
        r0 = pl.multiple_of(i * sub, sub)
        win = xp_ref[pl.ds(r0, sub + 2 * _CONV_PAD), :]
        acc = None
        for t in range(DN_CONV_K):
            lo = _CONV_PAD - half + t
            term = win[lo:lo + sub, :] * w[t:t + 1, :]
            acc = term if acc is None else acc + term
        return r0, jax.nn.silu(acc)

    @pl.when(c < n_qk)
    def _():
        scale = jnp.where(c < n_q, DN_HEAD_DIM ** -0.5, 1.0).astype(F32)

        def body(i, carry):
            r0, y = conv_tile(i)
            for hh in range(cb // DN_HEAD_DIM):
                seg = y[:, hh * DN_HEAD_DIM:(hh + 1) * DN_HEAD_DIM]
                seg = seg * lax.rsqrt(jnp.sum(seg * seg, axis=-1, keepdims=True) + EPS)
                o_ref[pl.ds(r0, sub), hh * DN_HEAD_DIM:(hh + 1) * DN_HEAD_DIM] = (seg * scale).astype(o_ref.dtype)
            return carry

        lax.fori_loop(0, seq // sub, body, 0)

    @pl.when(c >= n_qk)
    def _():
        def body(i, carry):
            r0, y = conv_tile(i)
            o_ref[pl.ds(r0, sub), :] = y.astype(o_ref.dtype)
            return carry

        lax.fori_loop(0, seq // sub, body, 0)


def _dn_conv(p, conv_w, seq, qk_dim, conv_dim):
    rows = p.shape[0]
    cb = 256 if seq > 1024 else 1024
    sub = min(seq, 256)
    return pl.pallas_call(
        functools.partial(_dn_conv_kernel, seq=seq, sub=sub, n_q=qk_dim // cb, n_qk=2 * qk_dim // cb),
        grid=(rows // seq, conv_dim // cb),
        in_specs=[pl.BlockSpec((seq, cb), lambda s, c: (s, c)),
                  pl.BlockSpec((DN_CONV_K, cb), lambda s, c: (0, c))],
        out_specs=pl.BlockSpec((seq, cb), lambda s, c: (s, c)),
        out_shape=jax.ShapeDtypeStruct((rows, conv_dim), BF16),
        scratch_shapes=[pltpu.VMEM((seq + 2 * _CONV_PAD, cb), F32)],
        compiler_params=_cparams("parallel", "parallel"),
        name="dn_conv",
    )(p, conv_w)


def _dn_gate_kernel(ab_ref, alog_ref, dtb_ref, pm_ref, gc_ref, gr_ref, *, hv):
    tm, width = ab_ref.shape
    tile = V7X_LANES
    x = ab_ref[...]
    lane_full = lax.broadcasted_iota(jnp.int32, (tm, width), 1)
    lane = lax.broadcasted_iota(jnp.int32, (tile, width), 1)
    is_a = (lane % (2 * hv)) < hv
    is_rev = lane >= 2 * hv
    g = -jnp.exp(alog_ref[...]) * jax.nn.softplus(x + dtb_ref[...])
    raw = jnp.where((lane_full % (2 * hv)) < hv, g, jax.nn.sigmoid(x))
    ri = lax.broadcasted_iota(jnp.int32, (tile, tile), 0)
    ci = lax.broadcasted_iota(jnp.int32, (tile, tile), 1)
    same = (ri // DN_CHUNK) == (ci // DN_CHUNK)
    low = jnp.where(same & (ci <= ri), 1.0, 0.0).astype(F32)
    upp = jnp.where(same & (ci >= ri), 1.0, 0.0).astype(F32)
    for t in range(tm // tile):
        blk = raw[t * tile:(t + 1) * tile, :]
        pre = jnp.dot(low, blk, precision=HIGHEST, preferred_element_type=F32)
        suf = jnp.dot(upp, blk, precision=HIGHEST, preferred_element_type=F32)
        out = jnp.where(is_a, jnp.where(is_rev, suf, pre), blk)
        gc_ref[t * tile:(t + 1) * tile, :] = out
        gr_ref[t] = lax.dot_general(pm_ref[...], out, (((1,), (1,)), ((), ())),
                                    precision=HIGHEST, preferred_element_type=F32)


def _gate_perm(hv):
    pm = np.zeros((4 * hv, 4 * hv), np.float32)
    for hq in range(hv // 2):
        for d in range(2):
            for isb in range(2):
                for j in range(2):
                    pm[hq * 8 + d * 4 + isb * 2 + j, d * 2 * hv + isb * hv + 2 * hq + j] = 1.0
    return jnp.asarray(pm)


def _dn_gates(ab, a_log, dt_bias, hv, tm):
    rows, width = ab.shape
    assert width == 4 * hv == V7X_LANES
    zeros = jnp.zeros_like(a_log)
    alog = jnp.concatenate([a_log, zeros], axis=1).reshape(1, width)
    dtb = jnp.concatenate([dt_bias, zeros], axis=1).reshape(1, width)
    vec = pl.BlockSpec((1, width), lambda i: (0, 0))
    return pl.pallas_call(
        functools.partial(_dn_gate_kernel, hv=hv),
        grid=(rows // tm,),
        in_specs=[pl.BlockSpec((tm, width), lambda i: (i, 0)), vec, vec,
                  pl.BlockSpec((width, width), lambda i: (0, 0))],
        out_specs=[pl.BlockSpec((tm, width), lambda i: (i, 0)),
                   pl.BlockSpec((tm // V7X_LANES, width, V7X_LANES), lambda i: (i, 0, 0))],
        out_shape=[jax.ShapeDtypeStruct((rows, width), F32),
                   jax.ShapeDtypeStruct((rows // V7X_LANES, width, V7X_LANES), F32)],
        compiler_params=_cparams("parallel"),
        name="dn_gates",
    )(ab, alog, dtb, _gate_perm(hv))


_INV_BLOCK = 4
_DN_GROUP_TILES = 2
_DN_TRIP_GROUPS = 8
_DN_HEADS_PER_STEP = 1


def _mm_bf16(ps, qs):
    return [_dot(p.astype(BF16), q.astype(BF16)) for p, q in zip(ps, qs)]


def _pair_diag(x, lo_half):
    zero = jnp.zeros_like(x)
    return jnp.concatenate([jnp.where(lo_half, x, zero), jnp.where(lo_half, zero, x)], axis=0)


def _unit_tri_inverse(mats, ri, ci, lo_half):
    eye = (ri == ci).astype(F32)
    bi, bj = ri // _INV_BLOCK, ci // _INV_BLOCK
    n = mats[0].shape[0]

    def mm(ps, qs):
        return _mm_bf16(ps, [_pair_diag(q, lo_half) for q in qs])

    ds = [jnp.where(bi == bj, a, 0.0) for a in mats]
    ts = [eye - d for d in ds]
    ps = mm(ds, ds)
    yield
    power = 2
    while 2 * power < _INV_BLOCK:
        both = mm([jnp.concatenate([t, p], axis=0) for t, p in zip(ts, ps)], ps)
        ts = [t + b[:n, :] for t, b in zip(ts, both)]
        ps = [b[n:, :] for b in both]
        power *= 2
        yield
    ts = [t + tp for t, tp in zip(ts, mm(ts, ps))]
    yield
    w = 1
    while w * _INV_BLOCK < n:
        off = (bi // (2 * w) == bj // (2 * w)) & (bi // w != bj // w)
        tes = mm(ts, [jnp.where(off, a, 0.0) for a in mats])
        yield
        ts = [t - tet for t, tet in zip(ts, mm(tes, ts))]
        yield
        w *= 2
    return ts


def _issue_pipelined(n_groups, hop, make_head, make_tail):
    heads, values = {}, {}
    tail, tail_g, tick = None, 0, 0
    while tail_g < n_groups:
        if tick % hop == 0 and tick // hop < n_groups:
            heads[tick // hop] = make_head(tick // hop)
        for g in sorted(heads):
            try:
                next(heads[g])
            except StopIteration as stop:
                values[g] = stop.value
                del heads[g]
        while tail_g < n_groups:
            if tail is None:
                if tail_g not in values:
                    break
                tail = make_tail(values.pop(tail_g))
            try:
                next(tail)
                break
            except StopIteration:
                tail, tail_g = None, tail_g + 1
        tick += 1


def _dn_core_kernel(qc_ref, kc_ref, vc_ref, gcc_ref, grc_ref, ql_ref, kl_ref, vl_ref, gcl_ref, grl_ref,
                    oc_ref, ol_ref, s_ref, *, hv):
    dk = DN_HEAD_DIM
    cs = DN_CHUNK
    tile = V7X_LANES
    n_h = _DN_HEADS_PER_STEP
    n_streams = 4 * n_h
    s_ref[...] = jnp.zeros(s_ref.shape, F32)
    oc_ref[...] = jnp.zeros(oc_ref.shape, F32)
    ol_ref[...] = jnp.zeros(ol_ref.shape, F32)
    ri = lax.broadcasted_iota(jnp.int32, (cs, 2 * cs), 0)
    lane = lax.broadcasted_iota(jnp.int32, (cs, 2 * cs), 1)
    shifts = [(tile - 2 * (n_h * pl.program_id(1) + hh)) % tile for hh in range(n_h)]

    n_sub = tile // cs
    lo_half = lane < cs
    ci = jnp.where(lo_half, lane, lane - cs)
    incl = (ri >= ci, ri <= ci)
    strict = (ri > ci, ri < ci)
    last = (cs - 1, 0)

    def state_free(refs, t_fwd, t_bwd):
        q_ref, k_ref, v_ref, gc_ref, gr_ref, _ = refs
        chunks = ([(0, t, c) for t in t_fwd for c in range(n_sub)]
                  + [(1, t, c) for t in t_bwd for c in reversed(range(n_sub))])
        r0s = [pl.multiple_of(t * tile, tile) + c * cs for _, t, c in chunks]
        cells = [(ic, hh) for ic in range(len(chunks)) for hh in range(n_h)]
        cdir = [chunks[ic][0] for ic, _ in cells]
        kbs = [k_ref[pl.ds(r0s[ic], cs), hh * dk:(hh + 1) * dk] for ic, hh in cells]
        qbs = [q_ref[pl.ds(r0s[ic], cs), hh * dk:(hh + 1) * dk] for ic, hh in cells]
        qkks = [_dot_nt(jnp.concatenate([qb, kb], axis=0), jnp.concatenate([kb, kb], axis=0))
                for qb, kb in zip(qbs, kbs)]
        gcts = [pltpu.roll(gc_ref[pl.ds(r0s[ic], cs), :], shifts[hh], 1) for ic, hh in cells]
        grts = [gr_ref[chunks[ic][1], hh * 8:(hh + 1) * 8, :] for ic, hh in cells]
        yield

        def packed_cols(m, off):
            base = cdir[m] * 2 * hv + off
            return jnp.where(lo_half, gcts[m][:, base:base + 1], gcts[m][:, base + 1:base + 2])

        def packed_row(m, off):
            d, _, c = chunks[cells[m][0]]
            r0_, r1_ = grts[m][d * 4 + off:d * 4 + off + 1, :], grts[m][d * 4 + off + 1:d * 4 + off + 2, :]
            if c == 0:
                return jnp.where(lo_half[:1], r0_, pltpu.roll(r1_, cs, 1))
            return jnp.where(lo_half[:1], pltpu.roll(r0_, cs, 1), r1_)

        n_cells = len(cells)
        dec2s = [jnp.exp(jnp.where(incl[cdir[m]], packed_cols(m, 0) - packed_row(m, 0), -jnp.inf))
                 for m in range(n_cells)]
        amats = [jnp.where(strict[cdir[m]], qkks[m][cs:, :] * dec2s[m] * packed_cols(m, hv), 0.0)
                 for m in range(n_cells)]
        qkms = [(qkks[m][:cs, :] * dec2s[m]).astype(BF16) for m in range(n_cells)]
        yield
        tinvs = yield from _unit_tri_inverse(amats, ri, ci, lo_half)
        tinvs = [t.astype(BF16) for t in tinvs]

        scs = [(m, j) for m in range(n_cells) for j in range(2)]
        dirs = [cdir[m] for m, _ in scs]
        gcols = [gcts[m][:, cdir[m] * 2 * hv + j:cdir[m] * 2 * hv + j + 1] for m, j in scs]
        bcols = [gcts[m][:, cdir[m] * 2 * hv + hv + j:cdir[m] * 2 * hv + hv + j + 1] for m, j in scs]
        vcols = [(cells[m][1] * 2 + j) * dk for m, j in scs]
        rows = [r0s[cells[m][0]] for m, _ in scs]
        ks = [kb.astype(F32) for kb in kbs]

        def head_rows(x, j):
            zero = jnp.zeros_like(x)
            return jnp.concatenate([x, zero] if j == 0 else [zero, x], axis=0)

        rhs = [jnp.concatenate([ks[m] * (bc * jnp.exp(gc)), v_ref[pl.ds(r0, cs), vc:vc + dk] * bc],
                               axis=1).astype(BF16)
               for (m, _), bc, gc, r0, vc in zip(scs, bcols, gcols, rows, vcols)]
        wus = [_dot(tinvs[m], head_rows(r, j)).astype(BF16) for (m, j), r in zip(scs, rhs)]
        yield
        qwus = [_dot(qkms[m], head_rows(wu, j)) for (m, j), wu in zip(scs, wus)]
        g_lasts = [gc[last[d]:last[d] + 1, :] for d, gc in zip(dirs, gcols)]
        kwus = [_dot_tn((ks[m] * jnp.exp(gl - gc)).astype(BF16), wu)
                for (m, _), gl, gc, wu in zip(scs, g_lasts, gcols, wus)]
        yield
        lhs = [jnp.concatenate([kwu[:, :dk].astype(BF16), (qbs[m] * jnp.exp(gc) - qwu[:, :dk]).astype(BF16)], axis=0)
               for (m, _), gc, qwu, kwu in zip(scs, gcols, qwus, kwus)]
        yield
        return dict(rows=list(zip(rows, vcols)), lhs=lhs, add=[kwu[:, dk:] for kwu in kwus],
                    intra=[qwu[:, dk:] for qwu in qwus], decay=[jnp.exp(gl) for gl in g_lasts])

    def recurrence(refs, pre, states):
        o_ref = refs[5]
        n_steps = len(pre["lhs"]) // n_streams
        for step in range(n_steps):
            idx = [((d * n_steps + step) * n_h + hh) * 2 + j for d in range(2) for hh in range(n_h) for j in range(2)]
            rs = [_dot(pre["lhs"][i], st.astype(BF16)) for i, st in zip(idx, states)]
            states[:] = [st * pre["decay"][i] + pre["add"][i] - r[:dk, :] for i, st, r in zip(idx, states, rs)]
            for i, r in zip(idx, rs):
                r0, vc = pre["rows"][i]
                o_ref[pl.ds(r0, cs), vc:vc + dk] += r[dk:, :] + pre["intra"][i]
            yield

    def run(refs, n_tiles):
        group = max(p for p in range(1, _DN_GROUP_TILES + 1) if n_tiles % p == 0)
        n_groups = n_tiles // group
        per_trip = max(p for p in range(1, _DN_TRIP_GROUPS + 1) if n_groups % p == 0)
        hop = group * n_sub

        def tiles(g):
            return [g * group + m for m in range(group)], [n_tiles - 1 - g * group - m for m in range(group)]

        def body(i, carry):
            states = [s_ref[si] for si in range(n_streams)]
            _issue_pipelined(per_trip, hop,
                             lambda g: state_free(refs, *tiles(i * per_trip + g)),
                             lambda pre: recurrence(refs, pre, states))
            for si in range(n_streams):
                s_ref[si] = states[si]
            return carry

        lax.fori_loop(0, n_groups // per_trip, body, 0)

    run((qc_ref, kc_ref, vc_ref, gcc_ref, grc_ref, oc_ref), qc_ref.shape[0] // tile)
    run((ql_ref, kl_ref, vl_ref, gcl_ref, grl_ref, ol_ref), ql_ref.shape[0] // tile)


def _dn_core(qkv_c, gc_c, gr_c, qkv_l, gc_l, gr_l, batch, hv):
    dk = DN_HEAD_DIM
    n_h = _DN_HEADS_PER_STEP
    steps = hv // 2 // n_h
    ctx_len = qkv_c.shape[0] // batch
    seq = qkv_l.shape[0] // batch
    width = gc_l.shape[1]

    def specs(t):
        return [pl.BlockSpec((t, n_h * dk), lambda b, h: (b, h)),
                pl.BlockSpec((t, n_h * dk), lambda b, h: (b, steps + h)),
                pl.BlockSpec((t, 2 * n_h * dk), lambda b, h: (b, steps + h)),
                pl.BlockSpec((t, width), lambda b, h: (b, 0)),
                pl.BlockSpec((t // V7X_LANES, 8 * n_h, V7X_LANES), lambda b, h: (b, h, 0))]

    return pl.pallas_call(
        functools.partial(_dn_core_kernel, hv=hv),
        grid=(batch, steps),
        in_specs=specs(ctx_len) + specs(seq),
        out_specs=[pl.BlockSpec((ctx_len, 2 * n_h * dk), lambda b, h: (b, h)),
                   pl.BlockSpec((seq, 2 * n_h * dk), lambda b, h: (b, h))],
        out_shape=[jax.ShapeDtypeStruct((batch * ctx_len, hv * dk), F32),
                   jax.ShapeDtypeStruct((batch * seq, hv * dk), F32)],
        scratch_shapes=[pltpu.VMEM((4 * n_h, dk, dk), F32)],
        compiler_params=_cparams("parallel", "parallel"),
        name="dn_core",
    )(qkv_c, qkv_c, qkv_c, gc_c, gr_c, qkv_l, qkv_l, qkv_l, gc_l, gr_l)


def _dn_out_kernel(o_ref, z_ref, ng_ref, w_ref, x_ref, gt_ref, out_ref, acc_ref, *, ncol):
    k = pl.program_id(1)

    @pl.when(k == 0)
    def _():
        acc_ref[...] = jnp.zeros(acc_ref.shape, F32)

    o = o_ref[...]
    dk = DN_HEAD_DIM
    segs = [_rms(o[:, h * dk:(h + 1) * dk]) * ng_ref[...] for h in range(o.shape[1] // dk)]
    on = jnp.concatenate(segs, axis=1) * jax.nn.silu(z_ref[...])
    acc_ref[...] += _dot(on.astype(BF16), w_ref[...])

    @pl.when(k == pl.num_programs(1) - 1)
    def _():
        _store_rows(out_ref, _load_rows(x_ref, ncol) + gt_ref[...] * acc_ref[...], ncol)


def _dn_out(o, p, z_col0, norm_g, w_out, x, mods, bidx, tm, seq, ncol):
    kdim, d = w_out.shape
    rows = x.size // d
    tk = 1024
    xv, xspec = _row_view(x, tm, seq, ncol)
    zb = z_col0 // tk
    out = pl.pallas_call(
        functools.partial(_dn_out_kernel, ncol=ncol),
        grid=(rows // tm, kdim // tk),
        in_specs=[pl.BlockSpec((tm, tk), lambda i, k: (i, k)),
                  pl.BlockSpec((tm, tk), lambda i, k: (i, zb + k)),
                  pl.BlockSpec((1, DN_HEAD_DIM), lambda i, k: (0, 0)),
                  pl.BlockSpec((tk, d), lambda i, k: (k, 0)),
                  xspec, _mod_spec(5, d, bidx)],
        out_specs=xspec,
        out_shape=jax.ShapeDtypeStruct(xv.shape, F32),
        scratch_shapes=[pltpu.VMEM((tm, d), F32)],
        compiler_params=_cparams("parallel", "arbitrary"),
        name="dn_out",
    )(o, p, norm_g.reshape(1, DN_HEAD_DIM), w_out, xv, mods)
    return out.reshape(x.shape)


def _sg_out_kernel(u_ref, v_ref, lg_ref, lb_ref, ws_ref, bs_ref, w_ref, x_ref, gt_ref, out_ref, m_ref, *, ncol):
    v = v_ref[...]
    mu = jnp.mean(v, axis=-1, keepdims=True)
    var = jnp.mean(jnp.square(v - mu), axis=-1, keepdims=True)
    m_ref[...] = ((v - mu) * lax.rsqrt(var + EPS) * lg_ref[...] + lb_ref[...]).astype(BF16)
    tm, sg_dim = v.shape
    gd = sg_dim // SG_GROUPS
    for g in range(SG_GROUPS):
        wsg = ws_ref[g]
        bias = bs_ref[:, g:g + 1]
        for c in range(tm // SG_CHUNK):
            rs = slice(c * SG_CHUNK, (c + 1) * SG_CHUNK)
            cs = slice(g * gd, (g + 1) * gd)
            mixed = _dot(wsg, m_ref[rs, cs]) + bias
            m_ref[rs, cs] = (u_ref[rs, cs] * mixed).astype(BF16)
    y = _dot(m_ref[...], w_ref[...])
    _store_rows(out_ref, _load_rows(x_ref, ncol) + gt_ref[...] * y, ncol)


def _sg_out(uv, ln_g, ln_b, w_s, b_s, w_out, x, mods, bidx, tm, seq, ncol):
    sg_dim, d = w_out.shape
    rows = x.size // d
    xv, xspec = _row_view(x, tm, seq, ncol)
    vec = pl.BlockSpec((1, sg_dim), lambda i: (0, 0))
    out = pl.pallas_call(
        functools.partial(_sg_out_kernel, ncol=ncol),
        grid=(rows // tm,),
        in_specs=[pl.BlockSpec((tm, sg_dim), lambda i: (i, 0)),
                  pl.BlockSpec((tm, sg_dim), lambda i: (i, 1)),
                  vec, vec,
                  pl.BlockSpec((SG_GROUPS, SG_CHUNK, SG_CHUNK), lambda i: (0, 0, 0)),
                  pl.BlockSpec((SG_CHUNK, SG_GROUPS), lambda i: (0, 0)),
                  pl.BlockSpec((sg_dim, d), lambda i: (0, 0), pipeline_mode=pl.Buffered(1)),
                  xspec, _mod_spec(5, d, bidx)],
        out_specs=xspec,
        out_shape=jax.ShapeDtypeStruct(xv.shape, F32),
        scratch_shapes=[pltpu.VMEM((tm, sg_dim), BF16)],
        compiler_params=_cparams("parallel"),
        name="sg_out",
    )(uv, uv, ln_g.reshape(1, sg_dim), ln_b.reshape(1, sg_dim), w_s, b_s.T, w_out, xv, mods)
    return out.reshape(x.shape)


def kernel(x, c, ctx, c_ctx, mod_w, mod_b, norm_g, ffn_w_gu, ffn_w_down, dn_w_in, dn_conv_w, dn_a_log, dn_dt_bias,
           dn_norm_g, dn_w_out, sg_w_in, sg_ln_g, sg_ln_b, sg_w_s, sg_b_s, sg_w_out, final_norm_g):
    batch, seq, d = x.shape
    ctx_len = ctx.shape[1]
    depth = mod_w.shape[0]
    nm = mod_w.shape[2]
    hv = dn_a_log.shape[2]
    dk = DN_HEAD_DIM
    qk_dim = hv // 2 * dk
    v_dim = hv * dk
    conv_dim = 2 * qk_dim + v_dim
    assert batch + 1 <= MOD_ROWS

    xl = x.reshape(batch * seq, d)
    xc = ctx.reshape(batch * ctx_len, d)
    tm_l = min(512, seq)
    tm_c = min(512, batch * ctx_len)
    tiles_per_seq = seq // tm_l
    bidx_l = lambda i: i // tiles_per_seq
    bidx_c = lambda i: batch
    ncol_cm = tm_l // (seq // GRID_W)

    cs = jnp.concatenate([c, c_ctx[None, :], jnp.zeros((MOD_ROWS - batch - 1, d), F32)], axis=0)
    mods_all = _modulation(cs, mod_w, mod_b).reshape(depth, MOD_ROWS, 1, nm)

    w_gu = ffn_w_gu.astype(BF16)
    w_down = ffn_w_down.astype(BF16)
    n_mix = 2

    for i in range(depth):
        last = i == depth - 1
        kind = i % n_mix
        j = i // n_mix
        ncol = ncol_cm if (j % 2) == 1 else 0
        need_ctx = not (last and kind == 1)
        mods = mods_all[i]

        xl = _half_ffn(xl, mods, 0, bidx_l, norm_g[i, 0], w_gu, w_down, i, 0, tm_l, seq,
                       ncol_cm if xl.ndim == 3 else 0)
        if need_ctx:
            xc = _half_ffn(xc, mods, 0, bidx_c, norm_g[i, 0], w_gu, w_down, i, 0, tm_c, ctx_len, 0)
        if ncol and xl.ndim == 2:
            xl = xl.reshape(batch, seq // GRID_W, GRID_W * d)
        elif not ncol and xl.ndim == 3:
            xl = xl.reshape(batch * seq, d)

        if kind == 0:
            w_in = dn_w_in[j].astype(BF16)
            w_main, w_ab = w_in[:, :conv_dim + v_dim], w_in[:, conv_dim + v_dim:]
            w_out = dn_w_out[j].astype(BF16)
            pl_, ab_l = _prenorm_linear(xl, mods, 3, bidx_l, norm_g[i, 1], w_main, tm_l, seq, ncol, w_extra=w_ab)
            pc_, ab_c = _prenorm_linear(xc, mods, 3, bidx_c, norm_g[i, 1], w_main, tm_c, ctx_len, 0, w_extra=w_ab)
            qkv_l = _dn_conv(pl_, dn_conv_w[j], seq, qk_dim, conv_dim)
            qkv_c = _dn_conv(pc_, dn_conv_w[j], ctx_len, qk_dim, conv_dim)
            gc_l, gr_l = _dn_gates(ab_l, dn_a_log[j], dn_dt_bias[j], hv, tm_l)
            gc_c, gr_c = _dn_gates(ab_c, dn_a_log[j], dn_dt_bias[j], hv, tm_c)
            o_c, o_l = _dn_core(qkv_c, gc_c, gr_c, qkv_l, gc_l, gr_l, batch, hv)
            xl = _dn_out(o_l, pl_, conv_dim, dn_norm_g[j], w_out, xl, mods, bidx_l, tm_l, seq, ncol)
            if not last:
                xc = _dn_out(o_c, pc_, conv_dim, dn_norm_g[j], w_out, xc, mods, bidx_c, tm_c, ctx_len, 0)
        else:
            w_in = sg_w_in[j].astype(BF16)
            w_out = sg_w_out[j].astype(BF16)
            w_s = sg_w_s[j].astype(BF16)
            tm_sl = min(256, seq)
            tm_sc = min(256, batch * ctx_len)
            ncol_s = (tm_sl // (seq // GRID_W)) if ncol else 0
            bidx_sl = lambda i, n=seq // tm_sl: i // n
            uv_l = _prenorm_linear(xl, mods, 3, bidx_l, norm_g[i, 1], w_in, tm_l, seq, ncol, act="gelu")
            xl = _sg_out(uv_l, sg_ln_g[j], sg_ln_b[j], w_s, sg_b_s[j], w_out, xl, mods, bidx_sl, tm_sl, seq, ncol_s)
            if not last:
                uv_c = _prenorm_linear(xc, mods, 3, bidx_c, norm_g[i, 1], w_in, tm_c, ctx_len, 0, act="gelu")
                xc = _sg_out(uv_c, sg_ln_g[j], sg_ln_b[j], w_s, sg_b_s[j], w_out, xc, mods, bidx_c, tm_sc, ctx_len, 0)

        xl = _half_ffn(xl, mods, 6, bidx_l, norm_g[i, 2], w_gu, w_down, i, 1, tm_l, seq,
                       ncol_cm if xl.ndim == 3 else 0, final_g=final_norm_g if last else None)
        if not last:
            xc = _half_ffn(xc, mods, 6, bidx_c, norm_g[i, 2], w_gu, w_down, i, 1, tm_c, ctx_len, 0)

    return xl.reshape(batch, seq, d)
```

```python
import functools

import numpy as np
import jax
import jax.numpy as jnp
from jax import lax
from jax.experimental import pallas as pl
from jax.experimental.pallas import tpu as pltpu

F32 = jnp.float32
BF16 = jnp.bfloat16
EPS = 1e-6
GRID_W = 64
DN_HEAD_DIM = 128
DN_CHUNK = 64
DN_CONV_K = 5
SG_CHUNK = 128
SG_GROUPS = 16

V7X_LANES = 128
V7X_SUBLANES = 8
V7X_VMEM_BYTES = 64 * 1024 * 1024
VMEM_LIMIT = V7X_VMEM_BYTES - 8 * 1024 * 1024
MOD_ROWS = 16
HIGHEST = lax.Precision.HIGHEST


def _cparams(*sem):
    return pltpu.CompilerParams(dimension_semantics=sem, vmem_limit_bytes=VMEM_LIMIT)


def _dot(a, b):
    return jnp.dot(a, b, preferred_element_type=F32)


def _dot_nt(a, b):
    return lax.dot_general(a, b, (((1,), (1,)), ((), ())), preferred_element_type=F32)


def _dot_tn(a, b):
    return lax.dot_general(a, b, (((0,), (0,)), ((), ())), preferred_element_type=F32)


def _rms(x):
    return x * lax.rsqrt(jnp.mean(x * x, axis=-1, keepdims=True) + EPS)


def _load_rows(ref, ncol):
    if not ncol:
        return ref[...]
    width = ref.shape[1] // ncol
    return jnp.concatenate([ref[:, k * width:(k + 1) * width] for k in range(ncol)], axis=0)


def _store_rows(ref, val, ncol):
    if not ncol:
        ref[...] = val
        return
    width = ref.shape[1] // ncol
    rows = ref.shape[0]
    for k in range(ncol):
        ref[:, k * width:(k + 1) * width] = val[k * rows:(k + 1) * rows, :]


def _row_view(arr, tm, seq, ncol):
    if not ncol:
        return arr, pl.BlockSpec((tm, arr.shape[1]), lambda i, *_: (i, 0))
    grid_rows = seq // GRID_W
    per_batch = GRID_W // ncol
    if arr.ndim == 3:
        view, width = arr, arr.shape[2] // GRID_W
    else:
        width = arr.shape[1]
        view = arr.reshape(arr.shape[0] // seq, grid_rows, GRID_W * width)
    return view, pl.BlockSpec((None, grid_rows, ncol * width), lambda i, *_: (i // per_batch, 0, i % per_batch))


def _mod_spec(k, d, bidx):
    return pl.BlockSpec((None, 1, d), lambda i, *_: (bidx(i), 0, k))


def _mod_kernel(c_ref, w_ref, b_ref, o_ref):
    s = jax.nn.silu(c_ref[...]).astype(BF16)
    o_ref[...] = _dot(s, w_ref[...].astype(BF16)) + b_ref[...]


def _modulation(cs, mod_w, mod_b):
    depth, d, nm = mod_w.shape
    tn = 1024
    return pl.pallas_call(
        _mod_kernel,
        grid=(depth, nm // tn),
        in_specs=[pl.BlockSpec((MOD_ROWS, d), lambda l, j: (0, 0)),
                  pl.BlockSpec((None, d, tn), lambda l, j: (l, 0, j)),
                  pl.BlockSpec((None, 1, tn), lambda l, j: (l, 0, j))],
        out_specs=pl.BlockSpec((None, MOD_ROWS, tn), lambda l, j: (l, 0, j)),
        out_shape=jax.ShapeDtypeStruct((depth, MOD_ROWS, nm), F32),
        compiler_params=_cparams("parallel", "parallel"),
        name="modulation",
    )(cs, mod_w, mod_b.reshape(depth, 1, nm))


_FFN_CHUNK = 512


_FFN_SLOTS = 3


def _ffn_kernel(x_ref, ng_ref, sh_ref, sc_ref, gt_ref, wgu_hbm, wdn_hbm, *rest, final_norm, ncol, layer, half):
    if final_norm:
        fg_ref, *rest = rest
    *rest, wg_buf, wu_buf, wd_buf, sem = rest
    o_ref, xn_ref, acc_ref = rest if ncol else (*rest, rest[0])
    j = pl.program_id(1)
    nf = pl.num_programs(1)
    fc = wg_buf.shape[2]
    step = pl.program_id(0) * nf + j
    n_steps = pl.num_programs(0) * nf

    def weight_copies(t):
        slot = lax.rem(t, _FFN_SLOTS)
        c0 = pl.multiple_of(lax.rem(t, nf) * fc, fc)
        c1 = pl.multiple_of((lax.rem(t, nf) + nf) * fc, fc)
        return (pltpu.make_async_copy(wgu_hbm.at[layer, half, :, pl.ds(c0, fc)], wg_buf.at[slot], sem.at[0, slot]),
                pltpu.make_async_copy(wgu_hbm.at[layer, half, :, pl.ds(c1, fc)], wu_buf.at[slot], sem.at[1, slot]),
                pltpu.make_async_copy(wdn_hbm.at[layer, half, pl.ds(c0, fc), :], wd_buf.at[slot], sem.at[2, slot]))

    @pl.when(step == 0)
    def _():
        for t in range(_FFN_SLOTS - 1):
            @pl.when(t < n_steps)
            def _():
                for cp in weight_copies(jnp.int32(t)):
                    cp.start()

    @pl.when(step + (_FFN_SLOTS - 1) < n_steps)
    def _():
        for cp in weight_copies(step + (_FFN_SLOTS - 1)):
            cp.start()

    for cp in weight_copies(step):
        cp.wait()
    slot = lax.rem(step, _FFN_SLOTS)
    wg_ref, wu_ref, wd_ref = wg_buf.at[slot], wu_buf.at[slot], wd_buf.at[slot]

    @pl.when(j == 0)
    def _():
        gain = ng_ref[...] * (1.0 + sc_ref[...])
        xn_ref[...] = (_rms(_load_rows(x_ref, ncol)) * gain + sh_ref[...]).astype(BF16)
        acc_ref[...] = jnp.zeros(acc_ref.shape, F32)

    xn = xn_ref[...]
    g = _dot(xn, wg_ref[...])
    u = _dot(xn, wu_ref[...])
    h = (jax.nn.silu(g) * u).astype(BF16)
    acc_ref[...] += _dot(h, wd_ref[...])

    @pl.when(j == pl.num_programs(1) - 1)
    def _():
        r = _load_rows(x_ref, ncol) + 0.5 * gt_ref[...] * acc_ref[...]
        if final_norm:
            r = _rms(r) * fg_ref[...]
        _store_rows(o_ref, r, ncol)


def _half_ffn(x, mods, k0, bidx, norm_g, w_gu, w_down, layer, half, tm, seq, ncol, final_g=None):
    d = w_down.shape[3]
    rows = x.size // d
    fc = _FFN_CHUNK
    nf = w_down.shape[2] // fc
    _, row = _row_view(x, tm, seq, ncol)
    vec = pl.BlockSpec((1, d), lambda i, j: (0, 0))
    in_specs = [row, vec, _mod_spec(k0, d, bidx), _mod_spec(k0 + 1, d, bidx), _mod_spec(k0 + 2, d, bidx),
                pl.BlockSpec(memory_space=pl.ANY), pl.BlockSpec(memory_space=pl.ANY)]
    args = [x, norm_g.reshape(1, d), mods, mods, mods, w_gu, w_down]
    if final_g is not None:
        in_specs.append(vec)
        args.append(final_g.reshape(1, d))
    return pl.pallas_call(
        functools.partial(_ffn_kernel, final_norm=final_g is not None, ncol=ncol, layer=layer, half=half),
        grid=(rows // tm, nf),
        in_specs=in_specs,
        out_specs=row,
        out_shape=jax.ShapeDtypeStruct(x.shape, F32),
        scratch_shapes=([pltpu.VMEM((tm, d), BF16)] + ([pltpu.VMEM((tm, d), F32)] if ncol else [])
                        + [pltpu.VMEM((_FFN_SLOTS, d, fc), BF16), pltpu.VMEM((_FFN_SLOTS, d, fc), BF16),
                           pltpu.VMEM((_FFN_SLOTS, fc, d), BF16), pltpu.SemaphoreType.DMA((3, _FFN_SLOTS))]),
        compiler_params=_cparams("arbitrary", "arbitrary"),
        name="half_ffn",
    )(*args)


def _prenorm_linear_kernel(x_ref, ng_ref, sh_ref, sc_ref, w_ref, *rest, ncol, act, has_extra):
    if has_extra:
        wx_ref, o_ref, ox_ref, xn_ref = rest
    else:
        o_ref, xn_ref = rest

    @pl.when(pl.program_id(1) == 0)
    def _():
        gain = ng_ref[...] * (1.0 + sc_ref[...])
        xn = (_rms(_load_rows(x_ref, ncol)) * gain + sh_ref[...]).astype(BF16)
        xn_ref[...] = xn
        if has_extra:
            ox_ref[...] = _dot(xn, wx_ref[...])

    y = _dot(xn_ref[...], w_ref[...])
    if act == "gelu":
        y = 0.5 * y * (1.0 + lax.erf(y * (0.5 ** 0.5)))
    o_ref[...] = y.astype(o_ref.dtype)


def _widest_column_tile(n, d, tm):
    fixed = 2 * tm * d * 4 + tm * d * 2
    per_column = 2 * (d * 2 + tm * 4)
    limit = (VMEM_LIMIT - fixed) // per_column
    return max(t for t in range(V7X_LANES, n + 1, V7X_LANES) if n % t == 0 and t <= limit)


def _prenorm_linear(x, mods, k0, bidx, norm_g, w, tm, seq, ncol, act=None, w_extra=None):
    d = w.shape[0]
    rows = x.size // d
    n = w.shape[1]
    tn = _widest_column_tile(n, d, tm)
    xv, xspec = _row_view(x, tm, seq, ncol)
    vec = pl.BlockSpec((1, d), lambda i, j: (0, 0))
    in_specs = [xspec, vec, _mod_spec(k0, d, bidx), _mod_spec(k0 + 1, d, bidx),
                pl.BlockSpec((d, tn), lambda i, j: (0, j))]
    args = [xv, norm_g.reshape(1, d), mods, mods, w]
    out_specs = [pl.BlockSpec((tm, tn), lambda i, j: (i, j))]
    out_shape = [jax.ShapeDtypeStruct((rows, n), F32)]
    if w_extra is not None:
        nx = w_extra.shape[1]
        in_specs.append(pl.BlockSpec((d, nx), lambda i, j: (0, 0)))
        args.append(w_extra)
        out_specs.append(pl.BlockSpec((tm, nx), lambda i, j: (i, 0)))
        out_shape.append(jax.ShapeDtypeStruct((rows, nx), F32))
    outs = pl.pallas_call(
        functools.partial(_prenorm_linear_kernel, ncol=ncol, act=act, has_extra=w_extra is not None),
        grid=(rows // tm, n // tn),
        in_specs=in_specs,
        out_specs=out_specs,
        out_shape=out_shape,
        scratch_shapes=[pltpu.VMEM((tm, d), BF16)],
        compiler_params=_cparams("parallel", "arbitrary"),
        name="mixer_in_proj",
    )(*args)
    return outs if w_extra is not None else outs[0]


_CONV_PAD = V7X_SUBLANES


def _dn_conv_kernel(p_ref, cw_ref, o_ref, xp_ref, *, seq, sub, n_q, n_qk):
    c = pl.program_id(1)
    cb = p_ref.shape[1]
    half = DN_CONV_K // 2
    xp_ref[0:_CONV_PAD, :] = jnp.zeros((_CONV_PAD, cb), F32)
    xp_ref[_CONV_PAD:_CONV_PAD + seq, :] = p_ref[...]
    xp_ref[_CONV_PAD + seq:, :] = jnp.zeros((_CONV_PAD, cb), F32)
    w = cw_ref[...]

    def conv_tile(i):
        r0 = pl.multiple_of(i * sub, sub)
        win = xp_ref[pl.ds(r0, sub + 2 * _CONV_PAD), :]
        acc = None
        for t in range(DN_CONV_K):
            lo = _CONV_PAD - half + t
            term = win[lo:lo + sub, :] * w[t:t + 1, :]
            acc = term if acc is None else acc + term
        return r0, jax.nn.silu(acc)

    @pl.when(c < n_qk)
    def _():
        scale = jnp.where(c < n_q, DN_HEAD_DIM ** -0.5, 1.0).astype(F32)

        def body(i, carry):
            r0, y = conv_tile(i)
            for hh in range(cb // DN_HEAD_DIM):
                seg = y[:, hh * DN_HEAD_DIM:(hh + 1) * DN_HEAD_DIM]
                seg = seg * lax.rsqrt(jnp.sum(seg * seg, axis=-1, keepdims=True) + EPS)
                o_ref[pl.ds(r0, sub), hh * DN_HEAD_DIM:(hh + 1) * DN_HEAD_DIM] = (seg * scale).astype(o_ref.dtype)
            return carry

        lax.fori_loop(0, seq // sub, body, 0)

    @pl.when(c >= n_qk)
    def _():
        def body(i, carry):
            r0, y = conv_tile(i)
            o_ref[pl.ds(r0, sub), :] = y.astype(o_ref.dtype)
            return carry

        lax.fori_loop(0, seq // sub, body, 0)


def _dn_conv(p, conv_w, seq, qk_dim, conv_dim):
    rows = p.shape[0]
    cb = 256 if seq > 1024 else 1024
    sub = min(seq, 256)
    return pl.pallas_call(
        functools.partial(_dn_conv_kernel, seq=seq, sub=sub, n_q=qk_dim // cb, n_qk=2 * qk_dim // cb),
        grid=(rows // seq, conv_dim // cb),
        in_specs=[pl.BlockSpec((seq, cb), lambda s, c: (s, c)),
                  pl.BlockSpec((DN_CONV_K, cb), lambda s, c: (0, c))],
        out_specs=pl.BlockSpec((seq, cb), lambda s, c: (s, c)),
        out_shape=jax.ShapeDtypeStruct((rows, conv_dim), BF16),
        scratch_shapes=[pltpu.VMEM((seq + 2 * _CONV_PAD, cb), F32)],
        compiler_params=_cparams("parallel", "parallel"),
        name="dn_conv",
    )(p, conv_w)


def _dn_gate_kernel(ab_ref, alog_ref, dtb_ref, pm_ref, gc_ref, gr_ref, *, hv):
    tm, width = ab_ref.shape
    tile = V7X_LANES
    x = ab_ref[...]
    lane_full = lax.broadcasted_iota(jnp.int32, (tm, width), 1)
    lane = lax.broadcasted_iota(jnp.int32, (tile, width), 1)
    is_a = (lane % (2 * hv)) < hv
    is_rev = lane >= 2 * hv
    g = -jnp.exp(alog_ref[...]) * jax.nn.softplus(x + dtb_ref[...])
    raw = jnp.where((lane_full % (2 * hv)) < hv, g, jax.nn.sigmoid(x))
    ri = lax.broadcasted_iota(jnp.int32, (tile, tile), 0)
    ci = lax.broadcasted_iota(jnp.int32, (tile, tile), 1)
    same = (ri // DN_CHUNK) == (ci // DN_CHUNK)
    low = jnp.where(same & (ci <= ri), 1.0, 0.0).astype(F32)
    upp = jnp.where(same & (ci >= ri), 1.0, 0.0).astype(F32)
    for t in range(tm // tile):
        blk = raw[t * tile:(t + 1) * tile, :]
        pre = jnp.dot(low, blk, precision=HIGHEST, preferred_element_type=F32)
        suf = jnp.dot(upp, blk, precision=HIGHEST, preferred_element_type=F32)
        out = jnp.where(is_a, jnp.where(is_rev, suf, pre), blk)
        gc_ref[t * tile:(t + 1) * tile, :] = out
        gr_ref[t] = lax.dot_general(pm_ref[...], out, (((1,), (1,)), ((), ())),
                                    precision=HIGHEST, preferred_element_type=F32)


def _gate_perm(hv):
    pm = np.zeros((4 * hv, 4 * hv), np.float32)
    for hq in range(hv // 2):
        for d in range(2):
            for isb in range(2):
                for j in range(2):
                    pm[hq * 8 + d * 4 + isb * 2 + j, d * 2 * hv + isb * hv + 2 * hq + j] = 1.0
    return jnp.asarray(pm)


def _dn_gates(ab, a_log, dt_bias, hv, tm):
    rows, width = ab.shape
    assert width == 4 * hv == V7X_LANES
    zeros = jnp.zeros_like(a_log)
    alog = jnp.concatenate([a_log, zeros], axis=1).reshape(1, width)
    dtb = jnp.concatenate([dt_bias, zeros], axis=1).reshape(1, width)
    vec = pl.BlockSpec((1, width), lambda i: (0, 0))
    return pl.pallas_call(
        functools.partial(_dn_gate_kernel, hv=hv),
        grid=(rows // tm,),
        in_specs=[pl.BlockSpec((tm, width), lambda i: (i, 0)), vec, vec,
                  pl.BlockSpec((width, width), lambda i: (0, 0))],
        out_specs=[pl.BlockSpec((tm, width), lambda i: (i, 0)),
                   pl.BlockSpec((tm // V7X_LANES, width, V7X_LANES), lambda i: (i, 0, 0))],
        out_shape=[jax.ShapeDtypeStruct((rows, width), F32),
                   jax.ShapeDtypeStruct((rows // V7X_LANES, width, V7X_LANES), F32)],
        compiler_params=_cparams("parallel"),
        name="dn_gates",
    )(ab, alog, dtb, _gate_perm(hv))


_INV_BLOCK = 4
_DN_GROUP_TILES = 2
_DN_TRIP_GROUPS = 8
_DN_HEADS_PER_STEP = 1


def _mm_bf16(ps, qs):
    return [_dot(p.astype(BF16), q.astype(BF16)) for p, q in zip(ps, qs)]


def _pair_diag(x, lo_half):
    zero = jnp.zeros_like(x)
    return jnp.concatenate([jnp.where(lo_half, x, zero), jnp.where(lo_half, zero, x)], axis=0)


def _unit_tri_inverse(mats, ri, ci, lo_half):
    eye = (ri == ci).astype(F32)
    bi, bj = ri // _INV_BLOCK, ci // _INV_BLOCK
    n = mats[0].shape[0]

    def mm(ps, qs):
        return _mm_bf16(ps, [_pair_diag(q, lo_half) for q in qs])

    ds = [jnp.where(bi == bj, a, 0.0) for a in mats]
    ts = [eye - d for d in ds]
    ps = mm(ds, ds)
    yield
    power = 2
    while 2 * power < _INV_BLOCK:
        both = mm([jnp.concatenate([t, p], axis=0) for t, p in zip(ts, ps)], ps)
        ts = [t + b[:n, :] for t, b in zip(ts, both)]
        ps = [b[n:, :] for b in both]
        power *= 2
        yield
    ts = [t + tp for t, tp in zip(ts, mm(ts, ps))]
    yield
    w = 1
    while w * _INV_BLOCK < n:
        off = (bi // (2 * w) == bj // (2 * w)) & (bi // w != bj // w)
        tes = mm(ts, [jnp.where(off, a, 0.0) for a in mats])
        yield
        ts = [t - tet for t, tet in zip(ts, mm(tes, ts))]
        yield
        w *= 2
    return ts


def _issue_pipelined(n_groups, hop, make_head, make_tail):
    heads, values = {}, {}
    tail, tail_g, tick = None, 0, 0
    while tail_g < n_groups:
        if tick % hop == 0 and tick // hop < n_groups:
            heads[tick // hop] = make_head(tick // hop)
        for g in sorted(heads):
            try:
                next(heads[g])
            except StopIteration as stop:
                values[g] = stop.value
                del heads[g]
        while tail_g < n_groups:
            if tail is None:
                if tail_g not in values:
                    break
                tail = make_tail(values.pop(tail_g))
            try:
                next(tail)
                break
            except StopIteration:
                tail, tail_g = None, tail_g + 1
        tick += 1


def _dn_core_kernel(qc_ref, kc_ref, vc_ref, gcc_ref, grc_ref, ql_ref, kl_ref, vl_ref, gcl_ref, grl_ref,
                    oc_ref, ol_ref, s_ref, *, hv):
    dk = DN_HEAD_DIM
    cs = DN_CHUNK
    tile = V7X_LANES
    n_h = _DN_HEADS_PER_STEP
    n_streams = 4 * n_h
    s_ref[...] = jnp.zeros(s_ref.shape, F32)
    oc_ref[...] = jnp.zeros(oc_ref.shape, F32)
    ol_ref[...] = jnp.zeros(ol_ref.shape, F32)
    ri = lax.broadcasted_iota(jnp.int32, (cs, 2 * cs), 0)
    lane = lax.broadcasted_iota(jnp.int32, (cs, 2 * cs), 1)
    shifts = [(tile - 2 * (n_h * pl.program_id(1) + hh)) % tile for hh in range(n_h)]

    n_sub = tile // cs
    lo_half = lane < cs
    ci = jnp.where(lo_half, lane, lane - cs)
    incl = (ri >= ci, ri <= ci)
    strict = (ri > ci, ri < ci)
    last = (cs - 1, 0)

    def state_free(refs, t_fwd, t_bwd):
        q_ref, k_ref, v_ref, gc_ref, gr_ref, _ = refs
        chunks = ([(0, t, c) for t in t_fwd for c in range(n_sub)]
                  + [(1, t, c) for t in t_bwd for c in reversed(range(n_sub))])
        r0s = [pl.multiple_of(t * tile, tile) + c * cs for _, t, c in chunks]
        cells = [(ic, hh) for ic in range(len(chunks)) for hh in range(n_h)]
        cdir = [chunks[ic][0] for ic, _ in cells]
        kbs = [k_ref[pl.ds(r0s[ic], cs), hh * dk:(hh + 1) * dk] for ic, hh in cells]
        qbs = [q_ref[pl.ds(r0s[ic], cs), hh * dk:(hh + 1) * dk] for ic, hh in cells]
        qkks = [_dot_nt(jnp.concatenate([qb, kb], axis=0), jnp.concatenate([kb, kb], axis=0))
                for qb, kb in zip(qbs, kbs)]
        gcts = [pltpu.roll(gc_ref[pl.ds(r0s[ic], cs), :], shifts[hh], 1) for ic, hh in cells]
        grts = [gr_ref[chunks[ic][1], hh * 8:(hh + 1) * 8, :] for ic, hh in cells]
        yield

        def packed_cols(m, off):
            base = cdir[m] * 2 * hv + off
            return jnp.where(lo_half, gcts[m][:, base:base + 1], gcts[m][:, base + 1:base + 2])

        def packed_row(m, off):
            d, _, c = chunks[cells[m][0]]
            r0_, r1_ = grts[m][d * 4 + off:d * 4 + off + 1, :], grts[m][d * 4 + off + 1:d * 4 + off + 2, :]
            if c == 0:
                return jnp.where(lo_half[:1], r0_, pltpu.roll(r1_, cs, 1))
            return jnp.where(lo_half[:1], pltpu.roll(r0_, cs, 1), r1_)

        n_cells = len(cells)
        dec2s = [jnp.exp(jnp.where(incl[cdir[m]], packed_cols(m, 0) - packed_row(m, 0), -jnp.inf))
                 for m in range(n_cells)]
        amats = [jnp.where(strict[cdir[m]], qkks[m][cs:, :] * dec2s[m] * packed_cols(m, hv), 0.0)
                 for m in range(n_cells)]
        qkms = [(qkks[m][:cs, :] * dec2s[m]).astype(BF16) for m in range(n_cells)]
        yield
        tinvs = yield from _unit_tri_inverse(amats, ri, ci, lo_half)
        tinvs = [t.astype(BF16) for t in tinvs]

        scs = [(m, j) for m in range(n_cells) for j in range(2)]
        dirs = [cdir[m] for m, _ in scs]
        gcols = [gcts[m][:, cdir[m] * 2 * hv + j:cdir[m] * 2 * hv + j + 1] for m, j in scs]
        bcols = [gcts[m][:, cdir[m] * 2 * hv + hv + j:cdir[m] * 2 * hv + hv + j + 1] for m, j in scs]
        vcols = [(cells[m][1] * 2 + j) * dk for m, j in scs]
        rows = [r0s[cells[m][0]] for m, _ in scs]
        ks = [kb.astype(F32) for kb in kbs]

        def head_rows(x, j):
            zero = jnp.zeros_like(x)
            return jnp.concatenate([x, zero] if j == 0 else [zero, x], axis=0)

        rhs = [jnp.concatenate([ks[m] * (bc * jnp.exp(gc)), v_ref[pl.ds(r0, cs), vc:vc + dk] * bc],
                               axis=1).astype(BF16)
               for (m, _), bc, gc, r0, vc in zip(scs, bcols, gcols, rows, vcols)]
        wus = [_dot(tinvs[m], head_rows(r, j)).astype(BF16) for (m, j), r in zip(scs, rhs)]
        yield
        qwus = [_dot(qkms[m], head_rows(wu, j)) for (m, j), wu in zip(scs, wus)]
        g_lasts = [gc[last[d]:last[d] + 1, :] for d, gc in zip(dirs, gcols)]
        kwus = [_dot_tn((ks[m] * jnp.exp(gl - gc)).astype(BF16), wu)
                for (m, _), gl, gc, wu in zip(scs, g_lasts, gcols, wus)]
        yield
        lhs = [jnp.concatenate([kwu[:, :dk].astype(BF16), (qbs[m] * jnp.exp(gc) - qwu[:, :dk]).astype(BF16)], axis=0)
               for (m, _), gc, qwu, kwu in zip(scs, gcols, qwus, kwus)]
        yield
        return dict(rows=list(zip(rows, vcols)), lhs=lhs, add=[kwu[:, dk:] for kwu in kwus],
                    intra=[qwu[:, dk:] for qwu in qwus], decay=[jnp.exp(gl) for gl in g_lasts])

    def recurrence(refs, pre, states):
        o_ref = refs[5]
        n_steps = len(pre["lhs"]) // n_streams
        for step in range(n_steps):
            idx = [((d * n_steps + step) * n_h + hh) * 2 + j for d in range(2) for hh in range(n_h) for j in range(2)]
            rs = [_dot(pre["lhs"][i], st.astype(BF16)) for i, st in zip(idx, states)]
            states[:] = [st * pre["decay"][i] + pre["add"][i] - r[:dk, :] for i, st, r in zip(idx, states, rs)]
            for i, r in zip(idx, rs):
                r0, vc = pre["rows"][i]
                o_ref[pl.ds(r0, cs), vc:vc + dk] += r[dk:, :] + pre["intra"][i]
            yield

    def run(refs, n_tiles):
        group = max(p for p in range(1, _DN_GROUP_TILES + 1) if n_tiles % p == 0)
        n_groups = n_tiles // group
        per_trip = max(p for p in range(1, _DN_TRIP_GROUPS + 1) if n_groups % p == 0)
        hop = group * n_sub

        def tiles(g):
            return [g * group + m for m in range(group)], [n_tiles - 1 - g * group - m for m in range(group)]

        def body(i, carry):
            states = [s_ref[si] for si in range(n_streams)]
            _issue_pipelined(per_trip, hop,
                             lambda g: state_free(refs, *tiles(i * per_trip + g)),
                             lambda pre: recurrence(refs, pre, states))
            for si in range(n_streams):
                s_ref[si] = states[si]
            return carry

        lax.fori_loop(0, n_groups // per_trip, body, 0)

    run((qc_ref, kc_ref, vc_ref, gcc_ref, grc_ref, oc_ref), qc_ref.shape[0] // tile)
    run((ql_ref, kl_ref, vl_ref, gcl_ref, grl_ref, ol_ref), ql_ref.shape[0] // tile)


def _dn_core(qkv_c, gc_c, gr_c, qkv_l, gc_l, gr_l, batch, hv):
    dk = DN_HEAD_DIM
    n_h = _DN_HEADS_PER_STEP
    steps = hv // 2 // n_h
    ctx_len = qkv_c.shape[0] // batch
    seq = qkv_l.shape[0] // batch
    width = gc_l.shape[1]

    def specs(t):
        return [pl.BlockSpec((t, n_h * dk), lambda b, h: (b, h)),
                pl.BlockSpec((t, n_h * dk), lambda b, h: (b, steps + h)),
                pl.BlockSpec((t, 2 * n_h * dk), lambda b, h: (b, steps + h)),
                pl.BlockSpec((t, width), lambda b, h: (b, 0)),
                pl.BlockSpec((t // V7X_LANES, 8 * n_h, V7X_LANES), lambda b, h: (b, h, 0))]

    return pl.pallas_call(
        functools.partial(_dn_core_kernel, hv=hv),
        grid=(batch, steps),
        in_specs=specs(ctx_len) + specs(seq),
        out_specs=[pl.BlockSpec((ctx_len, 2 * n_h * dk), lambda b, h: (b, h)),
                   pl.BlockSpec((seq, 2 * n_h * dk), lambda b, h: (b, h))],
        out_shape=[jax.ShapeDtypeStruct((batch * ctx_len, hv * dk), F32),
                   jax.ShapeDtypeStruct((batch * seq, hv * dk), F32)],
        scratch_shapes=[pltpu.VMEM((4 * n_h, dk, dk), F32)],
        compiler_params=_cparams("parallel", "parallel"),
        name="dn_core",
    )(qkv_c, qkv_c, qkv_c, gc_c, gr_c, qkv_l, qkv_l, qkv_l, gc_l, gr_l)


def _dn_out_kernel(o_ref, z_ref, ng_ref, w_ref, x_ref, gt_ref, out_ref, acc_ref, *, ncol):
    k = pl.program_id(1)

    @pl.when(k == 0)
    def _():
        acc_ref[...] = jnp.zeros(acc_ref.shape, F32)

    o = o_ref[...]
    dk = DN_HEAD_DIM
    segs = [_rms(o[:, h * dk:(h + 1) * dk]) * ng_ref[...] for h in range(o.shape[1] // dk)]
    on = jnp.concatenate(segs, axis=1) * jax.nn.silu(z_ref[...])
    acc_ref[...] += _dot(on.astype(BF16), w_ref[...])

    @pl.when(k == pl.num_programs(1) - 1)
    def _():
        _store_rows(out_ref, _load_rows(x_ref, ncol) + gt_ref[...] * acc_ref[...], ncol)


def _dn_out(o, p, z_col0, norm_g, w_out, x, mods, bidx, tm, seq, ncol):
    kdim, d = w_out.shape
    rows = x.size // d
    tk = 1024
    xv, xspec = _row_view(x, tm, seq, ncol)
    zb = z_col0 // tk
    out = pl.pallas_call(
        functools.partial(_dn_out_kernel, ncol=ncol),
        grid=(rows // tm, kdim // tk),
        in_specs=[pl.BlockSpec((tm, tk), lambda i, k: (i, k)),
                  pl.BlockSpec((tm, tk), lambda i, k: (i, zb + k)),
                  pl.BlockSpec((1, DN_HEAD_DIM), lambda i, k: (0, 0)),
                  pl.BlockSpec((tk, d), lambda i, k: (k, 0)),
                  xspec, _mod_spec(5, d, bidx)],
        out_specs=xspec,
        out_shape=jax.ShapeDtypeStruct(xv.shape, F32),
        scratch_shapes=[pltpu.VMEM((tm, d), F32)],
        compiler_params=_cparams("parallel", "arbitrary"),
        name="dn_out",
    )(o, p, norm_g.reshape(1, DN_HEAD_DIM), w_out, xv, mods)
    return out.reshape(x.shape)


def _sg_out_kernel(u_ref, v_ref, lg_ref, lb_ref, ws_ref, bs_ref, w_ref, x_ref, gt_ref, out_ref, m_ref, *, ncol):
    v = v_ref[...]
    mu = jnp.mean(v, axis=-1, keepdims=True)
    var = jnp.mean(jnp.square(v - mu), axis=-1, keepdims=True)
    m_ref[...] = ((v - mu) * lax.rsqrt(var + EPS) * lg_ref[...] + lb_ref[...]).astype(BF16)
    tm, sg_dim = v.shape
    gd = sg_dim // SG_GROUPS
    for g in range(SG_GROUPS):
        wsg = ws_ref[g]
        bias = bs_ref[:, g:g + 1]
        for c in range(tm // SG_CHUNK):
            rs = slice(c * SG_CHUNK, (c + 1) * SG_CHUNK)
            cs = slice(g * gd, (g + 1) * gd)
            mixed = _dot(wsg, m_ref[rs, cs]) + bias
            m_ref[rs, cs] = (u_ref[rs, cs] * mixed).astype(BF16)
    y = _dot(m_ref[...], w_ref[...])
    _store_rows(out_ref, _load_rows(x_ref, ncol) + gt_ref[...] * y, ncol)


def _sg_out(uv, ln_g, ln_b, w_s, b_s, w_out, x, mods, bidx, tm, seq, ncol):
    sg_dim, d = w_out.shape
    rows = x.size // d
    xv, xspec = _row_view(x, tm, seq, ncol)
    vec = pl.BlockSpec((1, sg_dim), lambda i: (0, 0))
    out = pl.pallas_call(
        functools.partial(_sg_out_kernel, ncol=ncol),
        grid=(rows // tm,),
        in_specs=[pl.BlockSpec((tm, sg_dim), lambda i: (i, 0)),
                  pl.BlockSpec((tm, sg_dim), lambda i: (i, 1)),
                  vec, vec,
                  pl.BlockSpec((SG_GROUPS, SG_CHUNK, SG_CHUNK), lambda i: (0, 0, 0)),
                  pl.BlockSpec((SG_CHUNK, SG_GROUPS), lambda i: (0, 0)),
                  pl.BlockSpec((sg_dim, d), lambda i: (0, 0), pipeline_mode=pl.Buffered(1)),
                  xspec, _mod_spec(5, d, bidx)],
        out_specs=xspec,
        out_shape=jax.ShapeDtypeStruct(xv.shape, F32),
        scratch_shapes=[pltpu.VMEM((tm, sg_dim), BF16)],
        compiler_params=_cparams("parallel"),
        name="sg_out",
    )(uv, uv, ln_g.reshape(1, sg_dim), ln_b.reshape(1, sg_dim), w_s, b_s.T, w_out, xv, mods)
    return out.reshape(x.shape)


def kernel(x, c, ctx, c_ctx, mod_w, mod_b, norm_g, ffn_w_gu, ffn_w_down, dn_w_in, dn_conv_w, dn_a_log, dn_dt_bias,
           dn_norm_g, dn_w_out, sg_w_in, sg_ln_g, sg_ln_b, sg_w_s, sg_b_s, sg_w_out, final_norm_g):
    batch, seq, d = x.shape
    ctx_len = ctx.shape[1]
    depth = mod_w.shape[0]
    nm = mod_w.shape[2]
    hv = dn_a_log.shape[2]
    dk = DN_HEAD_DIM
    qk_dim = hv // 2 * dk
    v_dim = hv * dk
    conv_dim = 2 * qk_dim + v_dim
    assert batch + 1 <= MOD_ROWS

    xl = x.reshape(batch * seq, d)
    xc = ctx.reshape(batch * ctx_len, d)
    tm_l = min(512, seq)
    tm_c = min(512, batch * ctx_len)
    tiles_per_seq = seq // tm_l
    bidx_l = lambda i: i // tiles_per_seq
    bidx_c = lambda i: batch
    ncol_cm = tm_l // (seq // GRID_W)

    cs = jnp.concatenate([c, c_ctx[None, :], jnp.zeros((MOD_ROWS - batch - 1, d), F32)], axis=0)
    mods_all = _modulation(cs, mod_w, mod_b).reshape(depth, MOD_ROWS, 1, nm)

    w_gu = ffn_w_gu.astype(BF16)
    w_down = ffn_w_down.astype(BF16)
    n_mix = 2

    for i in range(depth):
        last = i == depth - 1
        kind = i % n_mix
        j = i // n_mix
        ncol = ncol_cm if (j % 2) == 1 else 0
        need_ctx = not (last and kind == 1)
        mods = mods_all[i]

        xl = _half_ffn(xl, mods, 0, bidx_l, norm_g[i, 0], w_gu, w_down, i, 0, tm_l, seq,
                       ncol_cm if xl.ndim == 3 else 0)
        if need_ctx:
            xc = _half_ffn(xc, mods, 0, bidx_c, norm_g[i, 0], w_gu, w_down, i, 0, tm_c, ctx_len, 0)
        if ncol and xl.ndim == 2:
            xl = xl.reshape(batch, seq // GRID_W, GRID_W * d)
        elif not ncol and xl.ndim == 3:
            xl = xl.reshape(batch * seq, d)

        if kind == 0:
            w_in = dn_w_in[j].astype(BF16)
            w_main, w_ab = w_in[:, :conv_dim + v_dim], w_in[:, conv_dim + v_dim:]
            w_out = dn_w_out[j].astype(BF16)
            pl_, ab_l = _prenorm_linear(xl, mods, 3, bidx_l, norm_g[i, 1], w_main, tm_l, seq, ncol, w_extra=w_ab)
            pc_, ab_c = _prenorm_linear(xc, mods, 3, bidx_c, norm_g[i, 1], w_main, tm_c, ctx_len, 0, w_extra=w_ab)
            qkv_l = _dn_conv(pl_, dn_conv_w[j], seq, qk_dim, conv_dim)
            qkv_c = _dn_conv(pc_, dn_conv_w[j], ctx_len, qk_dim, conv_dim)
            gc_l, gr_l = _dn_gates(ab_l, dn_a_log[j], dn_dt_bias[j], hv, tm_l)
            gc_c, gr_c = _dn_gates(ab_c, dn_a_log[j], dn_dt_bias[j], hv, tm_c)
            o_c, o_l = _dn_core(qkv_c, gc_c, gr_c, qkv_l, gc_l, gr_l, batch, hv)
            xl = _dn_out(o_l, pl_, conv_dim, dn_norm_g[j], w_out, xl, mods, bidx_l, tm_l, seq, ncol)
            if not last:
                xc = _dn_out(o_c, pc_, conv_dim, dn_norm_g[j], w_out, xc, mods, bidx_c, tm_c, ctx_len, 0)
        else:
            w_in = sg_w_in[j].astype(BF16)
            w_out = sg_w_out[j].astype(BF16)
            w_s = sg_w_s[j].astype(BF16)
            tm_sl = min(256, seq)
            tm_sc = min(256, batch * ctx_len)
            ncol_s = (tm_sl // (seq // GRID_W)) if ncol else 0
            bidx_sl = lambda i, n=seq // tm_sl: i // n
            uv_l = _prenorm_linear(xl, mods, 3, bidx_l, norm_g[i, 1], w_in, tm_l, seq, ncol, act="gelu")
            xl = _sg_out(uv_l, sg_ln_g[j], sg_ln_b[j], w_s, sg_b_s[j], w_out, xl, mods, bidx_sl, tm_sl, seq, ncol_s)
            if not last:
                uv_c = _prenorm_linear(xc, mods, 3, bidx_c, norm_g[i, 1], w_in, tm_c, ctx_len, 0, act="gelu")
                xc = _sg_out(uv_c, sg_ln_g[j], sg_ln_b[j], w_s, sg_b_s[j], w_out, xc, mods, bidx_c, tm_sc, ctx_len, 0)

        xl = _half_ffn(xl, mods, 6, bidx_l, norm_g[i, 2], w_gu, w_down, i, 1, tm_l, seq,
                       ncol_cm if xl.ndim == 3 else 0, final_g=final_norm_g if last else None)
        if not last:
            xc = _half_ffn(xc, mods, 6, bidx_c, norm_g[i, 2], w_gu, w_down, i, 1, tm_c, ctx_len, 0)

    return xl.reshape(batch, seq, d)
```
